```python
import math
import jax
import jax.numpy as jnp
from jax import lax
import numpy as np

D_MODEL = 2048
BATCH = 2
SEQ = 8192
DEPTH = 2

GRID_W = 64
CTX_LEN = 256
EPS = 1e-6
CHUNK = 64
CONV_K = 5
ML_H = 4
ML_DK = 128
ML_DV = 256
ML_QK = ML_H * ML_DK
ML_V = ML_H * ML_DV
HG_H = 8
HG_DK = 128
HG_DV = 128
HG_K = HG_H * HG_DK
HG_V = HG_H * HG_DV
AB_IN = 2 * ML_QK + 2 * ML_V + 4 * ML_H + 3 * HG_K + 2 * HG_V
AB_OUT = ML_V + HG_V
SSD_EXPAND = 2
D_INNER = SSD_EXPAND * D_MODEL
SSD_P = 64
SSD_H = D_INNER // SSD_P
SSD_G = 8
SSD_R = SSD_H // SSD_G
SSD_N = 128
SSD_CONV = D_INNER + 2 * SSD_G * SSD_N
SSD_IN = D_INNER + SSD_CONV + 2 * SSD_H
N_EXPERTS = 16
N_GROUPS = 4
EXPERTS_PER_GROUP = N_EXPERTS // N_GROUPS
TOP_K = 2
D_EXPERT = 1024

kernel_name = 'hybrid_mlstm_hgrn2_ssd_moe_prefix_dit'

F32 = jnp.float32


def rms(x, g):
    xf = x.astype(F32)
    y = xf * lax.rsqrt(jnp.mean(xf * xf, -1, keepdims=True) + EPS) * g.astype(F32)
    return y.astype(x.dtype)


def split_cols(a, sizes):
    return jnp.split(a, np.cumsum(sizes)[:-1].tolist(), axis=-1)


def to_heads(a, n):
    B, T, _ = a.shape
    return a.reshape(B, T, n, -1).transpose(0, 2, 1, 3)


def to_chunks(a, axis):
    T = a.shape[axis]
    a = a.reshape(a.shape[:axis] + (T // CHUNK, CHUNK) + a.shape[axis + 1:])
    return jnp.moveaxis(a, axis, 0)


def from_chunks(a, axis):
    a = jnp.moveaxis(a, 0, axis)
    return a.reshape(a.shape[:axis] + (-1,) + a.shape[axis + 2:])


def dwconv_centred(u, w, b):
    K, C = w.shape
    out = lax.conv_general_dilated(u, w[:, None, :].astype(u.dtype), window_strides=(1,),
                                   padding=[(K // 2, K // 2)], dimension_numbers=('NWC', 'WIO', 'NWC'),
                                   feature_group_count=C)
    return out + b.astype(u.dtype)


def grid_transpose(u, rows, cols):
    B, _, Dm = u.shape
    return u.reshape(B, rows, cols, Dm).transpose(0, 2, 1, 3).reshape(B, rows * cols, Dm)


def causal_mask():
    return jnp.tril(jnp.ones((CHUNK, CHUNK), dtype=bool))


def mlstm_scan(q, k, v, ig, lf, state):
    tril = causal_mask()

    def step(carry, inp):
        C, n, m = carry
        qc, kc, vc, ic, fc = inp
        b = jnp.cumsum(fc, -1)
        logd = jnp.where(tril, b[..., :, None] - b[..., None, :] + ic[..., None, :], -jnp.inf)
        inter = b + m[..., None]
        m_t = jnp.maximum(inter, jnp.max(logd, -1))
        s = jnp.einsum('bhtd,bhsd->bhts', qc, kc) * jnp.exp(logd - m_t[..., None])
        sc = jnp.exp(inter - m_t)
        num = jnp.einsum('bhts,bhsv->bhtv', s, vc) + sc[..., None] * jnp.einsum('bhvd,bhtd->bhtv', C, qc)
        den = jnp.sum(s, -1) + sc * jnp.einsum('bhd,bhtd->bht', n, qc)
        h = num / jnp.maximum(jnp.abs(den), jnp.exp(-m_t))[..., None]
        b_last = b[..., -1]
        wlog = b_last[..., None] - b + ic
        m_new = jnp.maximum(b_last + m, jnp.max(wlog, -1))
        w = jnp.exp(wlog - m_new[..., None])
        dec = jnp.exp(b_last + m - m_new)
        C = dec[..., None, None] * C + jnp.einsum('bhs,bhsv,bhsd->bhvd', w, vc, kc)
        n = dec[..., None] * n + jnp.einsum('bhs,bhsd->bhd', w, kc)
        return (C, n, m_new), h

    state, h = lax.scan(step, state, (to_chunks(q, 2), to_chunks(k, 2), to_chunks(v, 2),
                                      to_chunks(ig, 2), to_chunks(lf, 2)))
    return from_chunks(h, 2), state


def hgrn2_scan(q, k, v, lg, S):
    tril = causal_mask()

    def step(S, inp):
        qc, kc, vc, gc = inp
        A = jnp.cumsum(gc, axis=2)
        dec = jnp.exp(jnp.where(tril[:, :, None], A[:, :, :, None, :] - A[:, :, None, :, :], -jnp.inf))
        s = jnp.einsum('bhtc,bhsc,bhtsc->bhts', qc, kc, dec)
        o = jnp.einsum('bhts,bhsv->bhtv', s, vc) + jnp.einsum('bhtc,bhcv->bhtv', qc * jnp.exp(A), S)
        A_last = A[:, :, -1]
        S = jnp.exp(A_last)[..., None] * S + jnp.einsum('bhsc,bhsv->bhcv', kc * jnp.exp(A_last[:, :, None] - A), vc)
        return S, o

    S, o = lax.scan(step, S, (to_chunks(q, 2), to_chunks(k, 2), to_chunks(v, 2), to_chunks(lg, 2)))
    return from_chunks(o, 2), S


def ab_features(h, w_in, conv_w, conv_b, ig_b, fg_b, lb):
    B, T, _ = h.shape
    p = jnp.einsum('btd,de->bte', h, w_in)
    qk, v, og, ig, fg, hq, hf, hi, hgate = split_cols(
        p, [2 * ML_QK, ML_V, ML_V, 2 * ML_H, 2 * ML_H, HG_K, 2 * HG_K, HG_V, HG_V])
    qk = jax.nn.silu(dwconv_centred(qk, conv_w, conv_b)).astype(F32)
    q = to_heads(qk[..., :ML_QK], ML_H)
    k = to_heads(qk[..., ML_QK:], ML_H) * (ML_DK ** -0.5)
    v = to_heads(v.astype(F32), ML_H)
    ig = (ig.astype(F32).reshape(B, T, 2, ML_H) + ig_b.astype(F32)).transpose(2, 0, 3, 1)
    lf = jax.nn.log_sigmoid(fg.astype(F32).reshape(B, T, 2, ML_H) + fg_b.astype(F32)).transpose(2, 0, 3, 1)
    f = lb + (1.0 - lb) * jax.nn.sigmoid(hf.astype(F32).reshape(B, T, 2, HG_K))
    f = f.reshape(B, T, 2, HG_H, HG_DK).transpose(2, 0, 3, 1, 4)
    hq = to_heads(hq.astype(F32), HG_H)
    hv = to_heads(hi.astype(F32), HG_H)
    return (q, k, v, ig, lf, hq, 1.0 - f, jnp.log(f), hv, og, hgate)


def ab_direction(f, d, st_ml, st_hg, reverse):
    q, k, v, ig, lf, hq, hk, hlg, hv = f[:9]
    ml_in = (q, k, v, ig[d], lf[d])
    hg_in = (hq, hk[d], hv, hlg[d])
    if reverse:
        ml_in = tuple(jnp.flip(a, 2) for a in ml_in)
        hg_in = tuple(jnp.flip(a, 2) for a in hg_in)
    hm, st_ml = mlstm_scan(*ml_in, st_ml)
    ho, st_hg = hgrn2_scan(*hg_in, st_hg)
    if reverse:
        hm, ho = jnp.flip(hm, 2), jnp.flip(ho, 2)
    return hm, ho, st_ml, st_hg


def head_rms(h, g):
    B, H, T, d = h.shape
    h = h * lax.rsqrt(jnp.mean(h * h, -1, keepdims=True) + EPS)
    return h.transpose(0, 2, 1, 3).reshape(B, T, H * d) * g.astype(F32)


def ab_output(hm, ho, f, ml_g, hg_g, w_out, dtype):
    og, hgate = f[9], f[10]
    y_ml = jax.nn.sigmoid(og.astype(F32)) * head_rms(hm, ml_g)
    y_hg = jax.nn.silu(hgate.astype(F32)) * head_rms(ho, hg_g)
    y = jnp.concatenate([y_ml, y_hg], axis=-1).astype(dtype)
    return jnp.einsum('bte,ed->btd', y, w_out)


def ab_mixer(hl, hc, w_in, conv_w, conv_b, ig_b, fg_b, lb, ml_g, hg_g, w_out, keep_ctx):
    fl = ab_features(hl, w_in, conv_w, conv_b, ig_b, fg_b, lb)
    fc = ab_features(hc, w_in, conv_w, conv_b, ig_b, fg_b, lb)
    B = hl.shape[0]
    lat, con = [], []
    for d, rev in ((0, False), (1, True)):
        st_ml = (jnp.zeros((B, ML_H, ML_DV, ML_DK), F32), jnp.zeros((B, ML_H, ML_DK), F32),
                 jnp.zeros((B, ML_H), F32))
        st_hg = jnp.zeros((B, HG_H, HG_DK, HG_DV), F32)
        hm_c, ho_c, st_ml, st_hg = ab_direction(fc, d, st_ml, st_hg, rev)
        hm_l, ho_l, _, _ = ab_direction(fl, d, st_ml, st_hg, rev)
        con.append((hm_c, ho_c))
        lat.append((hm_l, ho_l))
    yl = ab_output(lat[0][0] + lat[1][0], lat[0][1] + lat[1][1], fl, ml_g, hg_g, w_out, hl.dtype)
    yc = ab_output(con[0][0] + con[1][0], con[0][1] + con[1][1], fc, ml_g, hg_g, w_out, hc.dtype) if keep_ctx else None
    return yl, yc


def ssd_features(h, w_in, conv_w, conv_b, dt_b, a_log):
    B, T, _ = h.shape
    p = jnp.einsum('btd,de->bte', h, w_in)
    z, xbc, dt = split_cols(p, [D_INNER, SSD_CONV, 2 * SSD_H])
    xbc = jax.nn.silu(dwconv_centred(xbc, conv_w, conv_b)).astype(F32)
    xs, bm, cm = split_cols(xbc, [D_INNER, SSD_G * SSD_N, SSD_G * SSD_N])
    xs = xs.reshape(B, T, SSD_G, SSD_R, SSD_P).transpose(0, 2, 3, 1, 4)
    bm = bm.reshape(B, T, SSD_G, SSD_N).transpose(0, 2, 1, 3)
    cm = cm.reshape(B, T, SSD_G, SSD_N).transpose(0, 2, 1, 3)
    dt = jax.nn.softplus(dt.astype(F32).reshape(B, T, 2, SSD_H) + dt_b.astype(F32))
    la = dt * -jnp.exp(a_log.astype(F32))
    dt = dt.reshape(B, T, 2, SSD_G, SSD_R).transpose(2, 0, 3, 4, 1)
    la = la.reshape(B, T, 2, SSD_G, SSD_R).transpose(2, 0, 3, 4, 1)
    return z, xs, bm, cm, dt, la


def ssd_scan(xs, bm, cm, dt, la, h0):
    tril = causal_mask()

    def step(h, inp):
        xc, bc, cc, dc, lc = inp
        cum = jnp.cumsum(lc, -1)
        seg = jnp.exp(jnp.where(tril, cum[..., :, None] - cum[..., None, :], -jnp.inf)) * dc[..., None, :]
        cb = jnp.einsum('bgtn,bgsn->bgts', cc, bc)
        y = jnp.einsum('bgts,bgrts,bgrsp->bgrtp', cb, seg, xc)
        y = y + jnp.exp(cum)[..., None] * jnp.einsum('bgtn,bgrnp->bgrtp', cc, h)
        wgt = jnp.exp(cum[..., -1:] - cum) * dc
        h = jnp.exp(cum[..., -1])[..., None, None] * h + jnp.einsum('bgrs,bgsn,bgrsp->bgrnp', wgt, bc, xc)
        return h, y

    h, y = lax.scan(step, h0, (to_chunks(xs, 3), to_chunks(bm, 2), to_chunks(cm, 2),
                               to_chunks(dt, 3), to_chunks(la, 3)))
    return from_chunks(y, 3), h


def ssd_direction(f, d, h0, reverse):
    _, xs, bm, cm, dt, la = f
    dt, la = dt[d], la[d]
    if reverse:
        xs, bm, cm, dt, la = jnp.flip(xs, 3), jnp.flip(bm, 2), jnp.flip(cm, 2), jnp.flip(dt, 3), jnp.flip(la, 3)
    y, h = ssd_scan(xs, bm, cm, dt, la, h0)
    if reverse:
        y = jnp.flip(y, 3)
    return y, h


def ssd_output(y, f, d_skip, norm_g, w_out, dtype):
    z, xs = f[0], f[1]
    B, T, _ = z.shape
    y = y + d_skip.astype(F32).reshape(SSD_G, SSD_R)[:, :, None, None] * xs
    y = y.transpose(0, 3, 1, 2, 4).reshape(B, T, D_INNER)
    u = (y * jax.nn.silu(z.astype(F32))).reshape(B, T, SSD_G, D_INNER // SSD_G)
    u = u * lax.rsqrt(jnp.mean(u * u, -1, keepdims=True) + EPS)
    u = (u.reshape(B, T, D_INNER) * norm_g.astype(F32)).astype(dtype)
    return jnp.einsum('bte,ed->btd', u, w_out)


def ssd_mixer(hl, hc, w_in, conv_w, conv_b, dt_b, a_log, d_skip, norm_g, w_out, keep_ctx):
    fl = ssd_features(hl, w_in, conv_w, conv_b, dt_b, a_log)
    fc = ssd_features(hc, w_in, conv_w, conv_b, dt_b, a_log)
    B = hl.shape[0]
    lat, con = [], []
    for d, rev in ((0, False), (1, True)):
        h0 = jnp.zeros((B, SSD_G, SSD_R, SSD_N, SSD_P), F32)
        y_c, h_c = ssd_direction(fc, d, h0, rev)
        y_l, _ = ssd_direction(fl, d, h_c, rev)
        con.append(y_c)
        lat.append(y_l)
    yl = ssd_output(lat[0] + lat[1], fl, d_skip, norm_g, w_out, hl.dtype)
    yc = ssd_output(con[0] + con[1], fc, d_skip, norm_g, w_out, hc.dtype) if keep_ctx else None
    return yl, yc


def moe_ffn(h, router_w, router_b, w1, w3, w2):
    B, T, _ = h.shape
    score = jax.nn.sigmoid(jnp.einsum('btd,de->bte', h, router_w).astype(F32))
    biased = score + router_b.astype(F32)
    grouped = biased.reshape(B, T, N_GROUPS, EXPERTS_PER_GROUP)
    group_score = jnp.sum(lax.top_k(grouped, TOP_K)[0], axis=-1)
    best = jnp.argmax(group_score, axis=-1)
    in_group = (jnp.arange(N_EXPERTS) // EXPERTS_PER_GROUP) == best[..., None]
    _, idx = lax.top_k(jnp.where(in_group, biased, -jnp.inf), TOP_K)
    sel = jnp.take_along_axis(score, idx, axis=-1)
    wts = sel / jnp.sum(sel, -1, keepdims=True)
    gate = jnp.sum(jax.nn.one_hot(idx, N_EXPERTS, dtype=F32) * wts[..., None], axis=-2).astype(h.dtype)
    out = jnp.zeros_like(h)
    for e in range(N_EXPERTS):
        a = jax.nn.silu(jnp.einsum('btd,df->btf', h, w1[e])) * jnp.einsum('btd,df->btf', h, w3[e])
        out = out + gate[..., e:e + 1] * jnp.einsum('btf,fd->btd', a, w2[e])
    return out


def setup_inputs(seed: int = 0) -> dict:
    key = jax.random.key(seed)
    ks = iter(jax.random.split(key, 48))

    def nrm(shape, scale):
        return jax.random.normal(next(ks), shape, F32) * scale

    def gain(shape):
        return 1.0 + nrm(shape, 0.02)

    n_ab = (DEPTH + 1) // 2
    n_ssd = DEPTH // 2
    D = D_MODEL
    dt0 = jnp.exp(jax.random.uniform(next(ks), (n_ssd, 2, SSD_H), F32, math.log(1e-3), math.log(1e-1)))
    return {
        'x': nrm((BATCH, SEQ, D), 1.0),
        'c': nrm((BATCH, D), 1.0),
        'ctx': nrm((BATCH, CTX_LEN, D), 1.0),
        'c_ctx': nrm((D,), 1.0),
        'ada_w': nrm((DEPTH, D, 6 * D), 0.5 * D ** -0.5),
        'ada_b': nrm((DEPTH, 6 * D), 0.02),
        'norm1_g': gain((DEPTH, D)),
        'norm2_g': gain((DEPTH, D)),
        'ab_w_in': nrm((n_ab, D, AB_IN), D ** -0.5),
        'ab_conv_w': nrm((n_ab, CONV_K, 2 * ML_QK), CONV_K ** -0.5),
        'ab_conv_b': nrm((n_ab, 2 * ML_QK), 0.02),
        'ml_ig_b': nrm((n_ab, 2, ML_H), 0.1),
        'ml_fg_b': 3.0 + nrm((n_ab, 2, ML_H), 0.5),
        'hg_lb': nrm((DEPTH + 1, HG_K), 0.1),
        'ml_norm_g': gain((n_ab, ML_V)),
        'hg_norm_g': gain((n_ab, HG_V)),
        'ab_w_out': nrm((n_ab, AB_OUT, D), AB_OUT ** -0.5),
        'ssd_w_in': nrm((n_ssd, D, SSD_IN), D ** -0.5),
        'ssd_conv_w': nrm((n_ssd, CONV_K, SSD_CONV), CONV_K ** -0.5),
        'ssd_conv_b': nrm((n_ssd, SSD_CONV), 0.02),
        'ssd_dt_b': dt0 + jnp.log(-jnp.expm1(-dt0)),
        'ssd_a_log': jnp.log(jax.random.uniform(next(ks), (n_ssd, 2, SSD_H), F32, 1.0, 16.0)),
        'ssd_d': 1.0 + nrm((n_ssd, SSD_H), 0.1),
        'ssd_norm_g': gain((n_ssd, D_INNER)),
        'ssd_w_out': nrm((n_ssd, D_INNER, D), D_INNER ** -0.5),
        'router_w': nrm((D, N_EXPERTS), D ** -0.5),
        'router_b': nrm((N_EXPERTS,), 0.01),
        'moe_w1': nrm((DEPTH, N_EXPERTS, D, D_EXPERT), D ** -0.5),
        'moe_w3': nrm((DEPTH, N_EXPERTS, D, D_EXPERT), D ** -0.5),
        'moe_w2': nrm((DEPTH, N_EXPERTS, D_EXPERT, D), D_EXPERT ** -0.5),
        'final_g': gain((D,)),
    }


def reference(x, c, ctx, c_ctx, ada_w, ada_b, norm1_g, norm2_g, ab_w_in, ab_conv_w, ab_conv_b,
              ml_ig_b, ml_fg_b, hg_lb, ml_norm_g, hg_norm_g, ab_w_out, ssd_w_in, ssd_conv_w, ssd_conv_b,
              ssd_dt_b, ssd_a_log, ssd_d, ssd_norm_g, ssd_w_out, router_w, router_b, moe_w1, moe_w3,
              moe_w2, final_g):
    rows = x.shape[1] // GRID_W
    lb_all = jnp.cumsum(jax.nn.softmax(hg_lb.astype(F32), axis=0), axis=0)
    xl, xc = x, ctx
    for l in range(DEPTH):
        mod = (jax.nn.silu(c) @ ada_w[l] + ada_b[l])[:, None, :]
        modc = (jax.nn.silu(c_ctx) @ ada_w[l] + ada_b[l])[None, None, :]
        sh1, sc1, g1, sh2, sc2, g2 = jnp.split(mod, 6, axis=-1)
        csh1, csc1, cg1, csh2, csc2, cg2 = jnp.split(modc, 6, axis=-1)
        keep_ctx = l < DEPTH - 1
        hl = rms(xl, norm1_g[l]) * (1.0 + sc1) + sh1
        hc = rms(xc, norm1_g[l]) * (1.0 + csc1) + csh1
        j = l // 2
        if l % 2 == 0:
            yl, yc = ab_mixer(hl, hc, ab_w_in[j], ab_conv_w[j], ab_conv_b[j], ml_ig_b[j], ml_fg_b[j],
                              lb_all[l], ml_norm_g[j], hg_norm_g[j], ab_w_out[j], keep_ctx)
        else:
            yl, yc = ssd_mixer(grid_transpose(hl, rows, GRID_W), hc, ssd_w_in[j], ssd_conv_w[j],
                               ssd_conv_b[j], ssd_dt_b[j], ssd_a_log[j], ssd_d[j], ssd_norm_g[j],
                               ssd_w_out[j], keep_ctx)
            yl = grid_transpose(yl, GRID_W, rows)
        xl = xl + g1 * yl
        xl = xl + g2 * moe_ffn(rms(xl, norm2_g[l]) * (1.0 + sc2) + sh2, router_w, router_b,
                               moe_w1[l], moe_w3[l], moe_w2[l])
        if keep_ctx:
            xc = xc + cg1 * yc
            xc = xc + cg2 * moe_ffn(rms(xc, norm2_g[l]) * (1.0 + csc2) + csh2, router_w, router_b,
                                    moe_w1[l], moe_w3[l], moe_w2[l])
    return rms(xl, final_g)
```

```python
import functools
import math

import jax
import jax.numpy as jnp
from jax import lax
from jax.experimental import pallas as pl
from jax.experimental.pallas import tpu as pltpu

F32 = jnp.float32
BF16 = jnp.bfloat16
HI = lax.Precision.HIGHEST

D_MODEL = 2048
GRID_W = 64
EPS = 1e-6
CHUNK = 64
CONV_K = 5
ML_H, ML_DK, ML_DV = 4, 128, 256
HG_H, HG_DK, HG_DV = 8, 128, 128
ML_QK, ML_V = ML_H * ML_DK, ML_H * ML_DV
HG_K, HG_V = HG_H * HG_DK, HG_H * HG_DV
AB_OUT = ML_V + HG_V
D_INNER = 2 * D_MODEL
SSD_P, SSD_G, SSD_N = 64, 8, 128
SSD_H = D_INNER // SSD_P
SSD_R = SSD_H // SSD_G
N_EXPERTS, N_GROUPS, TOP_K, D_EXPERT = 16, 4, 2, 1024
EXPERTS_PER_GROUP = N_EXPERTS // N_GROUPS

LANES = 128
SUBLANES = 8
VMEM_LIMIT = 56 * 1024 * 1024

AB_S_Q, AB_S_K, AB_S_V, AB_S_OG, AB_S_HQ, AB_S_HF, AB_S_HI, AB_S_HG, AB_S_GATE = 0, 4, 8, 16, 24, 32, 48, 56, 64
AB_SLABS = 66
SSD_S_Z, SSD_S_X, SSD_S_B, SSD_S_C, SSD_S_DT = 0, 32, 64, 72, 80
SSD_SLABS = 84


def _cparams(sem):
    return pltpu.CompilerParams(dimension_semantics=sem, vmem_limit_bytes=VMEM_LIMIT)


def _silu(x):
    return x * jax.nn.sigmoid(x)


def _dot(a, b):
    return jnp.dot(a.astype(BF16), b.astype(BF16), preferred_element_type=F32)


def _dot_nt(a, b):
    return lax.dot_general(a.astype(BF16), b.astype(BF16), (((1,), (1,)), ((), ())),
                           preferred_element_type=F32)


def _dot_tn(a, b):
    return lax.dot_general(a.astype(BF16), b.astype(BF16), (((0,), (0,)), ((), ())),
                           preferred_element_type=F32)


def _dot_hi(a, b):
    return jnp.dot(a, b, precision=HI, preferred_element_type=F32)


def _tri(rev):
    t = lax.broadcasted_iota(jnp.int32, (CHUNK, CHUNK), 0)
    s = lax.broadcasted_iota(jnp.int32, (CHUNK, CHUNK), 1)
    return (s >= t) if rev else (s <= t)


def _mod_kernel(c_ref, w_ref, b_ref, o_ref):
    c = c_ref[...]
    o_ref[...] = _dot(_silu(c), w_ref[...]) + b_ref[...]


def _modulation(c8, ada_w, ada_b):
    depth, d, d6 = ada_w.shape
    tn = 1024
    return pl.pallas_call(
        _mod_kernel,
        grid=(depth, d6 // tn),
        in_specs=[pl.BlockSpec((SUBLANES, d), lambda l, j: (0, 0)),
                  pl.BlockSpec((None, d, tn), lambda l, j: (l, 0, j)),
                  pl.BlockSpec((None, 1, tn), lambda l, j: (l, 0, j))],
        out_specs=pl.BlockSpec((None, SUBLANES, tn), lambda l, j: (l, 0, j)),
        out_shape=jax.ShapeDtypeStruct((depth, SUBLANES, d6), F32),
        compiler_params=_cparams(("parallel", "parallel")),
        name="modulation",
    )(c8, ada_w, ada_b.reshape(depth, 1, d6))


def _inproj_kernel(x_ref, g_ref, sc_ref, sh_ref, w_ref, o_ref, h_ref):
    @pl.when(pl.program_id(1) == 0)
    def _():
        x = x_ref[...]
        y = x * lax.rsqrt(jnp.mean(x * x, axis=-1, keepdims=True) + EPS) * g_ref[...]
        h_ref[...] = (y * (1.0 + sc_ref[...]) + sh_ref[...]).astype(BF16)

    acc = jnp.dot(h_ref[...], w_ref[...], preferred_element_type=F32)
    for s in range(o_ref.shape[0]):
        o_ref[s] = acc[:, s * LANES:(s + 1) * LANES]


def _inproj(x, g, sc, sh, w, seg_fn, tm, tn):
    n, d = x.shape
    ncols = w.shape[1]
    nsl = tn // LANES
    return pl.pallas_call(
        _inproj_kernel,
        grid=(n // tm, ncols // tn),
        in_specs=[pl.BlockSpec((tm, d), lambda i, j: (i, 0)),
                  pl.BlockSpec((1, d), lambda i, j: (0, 0)),
                  pl.BlockSpec((None, 1, d), lambda i, j: (seg_fn(i * tm), 0, 0)),
                  pl.BlockSpec((None, 1, d), lambda i, j: (seg_fn(i * tm), 0, 0)),
                  pl.BlockSpec((d, tn), lambda i, j: (0, j))],
        out_specs=pl.BlockSpec((nsl, tm, LANES), lambda i, j: (j, i, 0)),
        out_shape=jax.ShapeDtypeStruct((ncols // LANES, n, LANES), F32),
        scratch_shapes=[pltpu.VMEM((tm, d), BF16)],
        compiler_params=_cparams(("parallel", "arbitrary")),
        name="inproj",
    )(x, g.reshape(1, d), sc, sh, w)


def _conv_kernel(prev_ref, cur_ref, next_ref, w_ref, o_ref, *, tt, tiles_lat, n_lat_tiles, tiles_ctx):
    i = pl.program_id(1)
    in_lat = i < n_lat_tiles
    pos = jnp.where(in_lat, i % tiles_lat, (i - n_lat_tiles) % tiles_ctx)
    last = jnp.where(in_lat, tiles_lat - 1, tiles_ctx - 1)
    keep_prev = (pos != 0).astype(F32)
    keep_next = (pos != last).astype(F32)
    ext = jnp.concatenate([prev_ref[...] * keep_prev, cur_ref[...], next_ref[...] * keep_next], axis=1)
    w = w_ref[...]
    acc = jnp.zeros(cur_ref.shape, F32) + w[:, CONV_K:CONV_K + 1, :]
    for k in range(CONV_K):
        shift = (CONV_K // 2 - k) % (tt + 2 * SUBLANES)
        r = ext if shift == 0 else pltpu.roll(ext, shift, 1)
        acc = acc + r[:, SUBLANES:SUBLANES + tt, :] * w[:, k:k + 1, :]
    o_ref[...] = _silu(acc) * w[:, CONV_K + 1:CONV_K + 2, :]


def _conv_slabs(p3, slab0, nslab, wpack, dims, sb):
    b, seq, ctxl = dims
    n = p3.shape[1]
    tt = min(256, ctxl)
    t8 = tt // SUBLANES
    nblk8 = n // SUBLANES
    s0 = slab0 // sb
    kern = functools.partial(_conv_kernel, tt=tt, tiles_lat=seq // tt, n_lat_tiles=b * seq // tt,
                             tiles_ctx=ctxl // tt)
    return pl.pallas_call(
        kern,
        grid=(nslab // sb, n // tt),
        in_specs=[pl.BlockSpec((sb, SUBLANES, LANES), lambda s, i: (s0 + s, jnp.maximum(i * t8 - 1, 0), 0)),
                  pl.BlockSpec((sb, tt, LANES), lambda s, i: (s0 + s, i, 0)),
                  pl.BlockSpec((sb, SUBLANES, LANES),
                               lambda s, i: (s0 + s, jnp.minimum((i + 1) * t8, nblk8 - 1), 0)),
                  pl.BlockSpec((sb, SUBLANES, LANES), lambda s, i: (s, 0, 0))],
        out_specs=pl.BlockSpec((sb, tt, LANES), lambda s, i: (s, i, 0)),
        out_shape=jax.ShapeDtypeStruct((nslab, n, LANES), F32),
        compiler_params=_cparams(("parallel", "parallel")),
        name="conv",
    )(p3, p3, p3, wpack)


def _conv_pack(conv_w, conv_b, scale):
    c = conv_w.shape[1]
    rows = jnp.concatenate([conv_w, conv_b[None], scale[None], jnp.zeros((1, c), F32)], axis=0)
    return rows.reshape(SUBLANES, c // LANES, LANES).transpose(1, 0, 2)


def _chunk_maps(dims):
    b, seq, ctxl = dims
    ncc, nlc = ctxl // CHUNK, seq // CHUNK

    def fwd(bi, i):
        return jnp.where(i < ncc, b * nlc + bi * ncc + i, bi * nlc + (i - ncc))

    def bwd(bi, i):
        return jnp.where(i < ncc, b * nlc + bi * ncc + (ncc - 1 - i), bi * nlc + (nlc - 1 - (i - ncc)))

    return fwd, bwd, ncc + nlc


def _mlstm_kernel(qkf_ref, vf_ref, gcf_ref, grf_ref, qkb_ref, vb_ref, gcb_ref, grb_ref, bc_ref, br_ref,
                  of_ref, ob_ref, c_ref, n_ref, m_ref):
    @pl.when(pl.program_id(1) == 0)
    def _():
        c_ref[...] = jnp.zeros_like(c_ref)
        n_ref[...] = jnp.zeros_like(n_ref)
        m_ref[...] = jnp.zeros_like(m_ref)

    ng = 2 * ML_H
    for d, (qk_ref, v_ref, gc_ref, gr_ref, o_ref) in enumerate(
            ((qkf_ref, vf_ref, gcf_ref, grf_ref, of_ref), (qkb_ref, vb_ref, gcb_ref, grb_ref, ob_ref))):
        rev = d == 1
        mask = _tri(rev)
        gcol = gc_ref[0][:, :2 * ng] + bc_ref[...]
        grow = gr_ref[...] + br_ref[...]
        ic_all = gcol[:, d * ML_H:(d + 1) * ML_H]
        lfc_all = jax.nn.log_sigmoid(gcol[:, ng + d * ML_H:ng + (d + 1) * ML_H])
        ir_all = grow[d * ML_H:(d + 1) * ML_H, :]
        lfr_all = jax.nn.log_sigmoid(grow[ng + d * ML_H:ng + (d + 1) * ML_H, :])
        incl = mask.astype(F32)
        bcol_all = _dot_hi(incl, lfc_all)
        brow_all = _dot_hi(lfr_all, _tri(not rev).astype(F32))
        last = 0 if rev else CHUNK - 1
        for h in range(ML_H):
            q = qk_ref[h]
            k = qk_ref[ML_H + h]
            v = jnp.concatenate([v_ref[2 * h], v_ref[2 * h + 1]], axis=1)
            bcol, icol = bcol_all[:, h:h + 1], ic_all[:, h:h + 1]
            brow, irow = brow_all[h:h + 1, :], ir_all[h:h + 1, :]
            cst = c_ref[d, h]
            nst = n_ref[d, h]
            mprev = m_ref[d, h][:, :1]
            logd = jnp.where(mask, bcol - brow + irow, -jnp.inf)
            inter = bcol + mprev
            m_t = jnp.maximum(inter, jnp.max(logd, axis=1, keepdims=True))
            s = _dot_nt(q, k) * jnp.exp(logd - m_t)
            sc = jnp.exp(inter - m_t)
            num = _dot(s, v) + sc * _dot_nt(q, cst)
            den = jnp.sum(s, axis=1, keepdims=True) + sc * jnp.sum(q * nst, axis=1, keepdims=True)
            hout = num / jnp.maximum(jnp.abs(den), jnp.exp(-m_t))
            o_ref[2 * h] = hout[:, :LANES]
            o_ref[2 * h + 1] = hout[:, LANES:]
            b_last = bcol[last:last + 1, :]
            wlog = b_last - bcol + icol
            m_new = jnp.maximum(b_last + mprev, jnp.max(wlog, axis=0, keepdims=True))
            w = jnp.exp(wlog - m_new)
            dec = jnp.exp(b_last + mprev - m_new)
            c_ref[d, h] = dec * cst + _dot_tn(w * v, k)
            n_ref[d, h] = dec * nst + jnp.sum(w * k, axis=0, keepdims=True)
            m_ref[d, h] = jnp.broadcast_to(m_new, (1, LANES))


def _mlstm(qk, p3, grow, bcol, brow, dims):
    b = dims[0]
    n = p3.shape[1]
    fwd, bwd, nch = _chunk_maps(dims)

    def specs(cm):
        return [pl.BlockSpec((2 * ML_H, CHUNK, LANES), lambda bi, i: (0, cm(bi, i), 0)),
                pl.BlockSpec((2 * ML_H, CHUNK, LANES), lambda bi, i: (AB_S_V // (2 * ML_H), cm(bi, i), 0)),
                pl.BlockSpec((1, CHUNK, LANES), lambda bi, i: (AB_S_GATE, cm(bi, i), 0)),
                pl.BlockSpec((None, 4 * ML_H, CHUNK), lambda bi, i: (cm(bi, i), 0, 0))]

    out_sd = jax.ShapeDtypeStruct((2 * ML_H, n, LANES), F32)
    return pl.pallas_call(
        _mlstm_kernel,
        grid=(b, nch),
        in_specs=specs(fwd) + specs(bwd) + [pl.BlockSpec((1, 4 * ML_H), lambda bi, i: (0, 0)),
                                            pl.BlockSpec((4 * ML_H, 1), lambda bi, i: (0, 0))],
        out_specs=[pl.BlockSpec((2 * ML_H, CHUNK, LANES), lambda bi, i: (0, fwd(bi, i), 0)),
                   pl.BlockSpec((2 * ML_H, CHUNK, LANES), lambda bi, i: (0, bwd(bi, i), 0))],
        out_shape=[out_sd, out_sd],
        scratch_shapes=[pltpu.VMEM((2, ML_H, ML_DV, ML_DK), F32),
                        pltpu.VMEM((2, ML_H, 1, ML_DK), F32),
                        pltpu.VMEM((2, ML_H, 1, LANES), F32)],
        compiler_params=_cparams(("parallel", "arbitrary")),
        name="mlstm",
    )(qk, p3, p3, grow, qk, p3, p3, grow, bcol, brow)


def _bcast_rows(a, rows, span):
    parts = [jnp.broadcast_to(a[r:r + 1, :], (span, a.shape[1])) for r in rows]
    return parts[0] if len(parts) == 1 else jnp.concatenate(parts, axis=0)


def _hgrn2_kernel(qf_ref, ff_ref, vf_ref, qb_ref, fb_ref, vb_ref, lb_ref, of_ref, ob_ref, s_ref):
    @pl.when(pl.program_id(1) == 0)
    def _():
        s_ref[...] = jnp.zeros_like(s_ref)

    t = lax.broadcasted_iota(jnp.int32, (CHUNK, CHUNK), 0)
    s = lax.broadcasted_iota(jnp.int32, (CHUNK, CHUNK), 1)
    for d, (q_ref, f_ref, v_ref, o_ref) in enumerate(((qf_ref, ff_ref, vf_ref, of_ref),
                                                      (qb_ref, fb_ref, vb_ref, ob_ref))):
        rev = d == 1
        mask = _tri(rev)
        incl = mask.astype(F32)
        last = 0 if rev else CHUNK - 1
        for h in range(HG_H):
            q = q_ref[h]
            lb = lb_ref[h]
            f = lb + (1.0 - lb) * jax.nn.sigmoid(f_ref[h])
            k = 1.0 - f
            lg = jnp.log(f)
            v = v_ref[h]
            a = _dot_hi(incl, lg)
            scores = jnp.zeros((CHUNK, CHUNK), F32)
            for m in (32, 16, 8):
                nb = CHUNK // (2 * m)
                ref_rows = [bi * 2 * m + (m if rev else m - 1) for bi in range(nb)]
                aref = _bcast_rows(a, ref_rows, 2 * m)
                eq = jnp.exp(jnp.minimum(a - aref, 0.0))
                ek = jnp.exp(jnp.minimum(aref - a, 0.0))
                same = (t // (2 * m)) == (s // (2 * m))
                t_late = ((t // m) % 2 == 0) if rev else ((t // m) % 2 == 1)
                s_early = ((s // m) % 2 == 1) if rev else ((s // m) % 2 == 0)
                lvl = same & t_late & s_early
                scores = scores + jnp.where(lvl, _dot_nt(q * eq, k * ek), 0.0)
            aex = a - lg
            ref_rows = [bi * SUBLANES + (SUBLANES - 1 if rev else 0) for bi in range(CHUNK // SUBLANES)]
            aref = _bcast_rows(aex, ref_rows, SUBLANES)
            diag = ((t // SUBLANES) == (s // SUBLANES)) & mask
            scores = scores + jnp.where(diag, _dot_nt(q * jnp.exp(a - aref), k * jnp.exp(aref - a)), 0.0)
            st = s_ref[d, h]
            o_ref[h] = _dot(scores, v) + _dot_nt(q * jnp.exp(a), st)
            a_last = a[last:last + 1, :]
            s_ref[d, h] = st * jnp.exp(a_last) + _dot_tn(v, k * jnp.exp(a_last - a))


def _hgrn2(p3, lb, dims):
    b = dims[0]
    n = p3.shape[1]
    fwd, bwd, nch = _chunk_maps(dims)

    def specs(cm, d):
        return [pl.BlockSpec((HG_H, CHUNK, LANES), lambda bi, i: (AB_S_HQ // HG_H, cm(bi, i), 0)),
                pl.BlockSpec((HG_H, CHUNK, LANES), lambda bi, i: (AB_S_HF // HG_H + d, cm(bi, i), 0)),
                pl.BlockSpec((HG_H, CHUNK, LANES), lambda bi, i: (AB_S_HI // HG_H, cm(bi, i), 0))]

    out_sd = jax.ShapeDtypeStruct((HG_H, n, LANES), F32)
    return pl.pallas_call(
        _hgrn2_kernel,
        grid=(b, nch),
        in_specs=specs(fwd, 0) + specs(bwd, 1) + [pl.BlockSpec((HG_H, 1, LANES), lambda bi, i: (0, 0, 0))],
        out_specs=[pl.BlockSpec((HG_H, CHUNK, LANES), lambda bi, i: (0, fwd(bi, i), 0)),
                   pl.BlockSpec((HG_H, CHUNK, LANES), lambda bi, i: (0, bwd(bi, i), 0))],
        out_shape=[out_sd, out_sd],
        scratch_shapes=[pltpu.VMEM((2, HG_H, HG_DV, HG_DK), F32)],
        compiler_params=_cparams(("parallel", "arbitrary")),
        name="hgrn2",
    )(p3, p3, p3, p3, p3, p3, lb)


def _ab_out_kernel(mf_ref, mb_ref, hf_ref, hb_ref, og_ref, hg_ref, mlg_ref, hgg_ref, w_ref, x_ref, g1_ref,
                   o_ref, lhs_ref):
    for h in range(ML_H):
        hs = jnp.concatenate([mf_ref[2 * h] + mb_ref[2 * h], mf_ref[2 * h + 1] + mb_ref[2 * h + 1]], axis=1)
        r = hs * lax.rsqrt(jnp.mean(hs * hs, axis=-1, keepdims=True) + EPS)
        og = jnp.concatenate([og_ref[2 * h], og_ref[2 * h + 1]], axis=1)
        y = jax.nn.sigmoid(og) * (r * mlg_ref[:, h * ML_DV:(h + 1) * ML_DV])
        lhs_ref[:, h * ML_DV:(h + 1) * ML_DV] = y.astype(BF16)
    for h in range(HG_H):
        hs = hf_ref[h] + hb_ref[h]
        r = hs * lax.rsqrt(jnp.mean(hs * hs, axis=-1, keepdims=True) + EPS)
        y = _silu(hg_ref[h]) * (r * hgg_ref[:, h * HG_DV:(h + 1) * HG_DV])
        lhs_ref[:, ML_V + h * HG_DV:ML_V + (h + 1) * HG_DV] = y.astype(BF16)
    acc = jnp.dot(lhs_ref[...], w_ref[...], preferred_element_type=F32)
    o_ref[...] = x_ref[...] + g1_ref[...] * acc


def _ab_out(hm, ho, p3, ml_g, hg_g, w_out, x, g1, seg_fn, n, tm):
    d = x.shape[1]
    slab8 = lambda idx: pl.BlockSpec((SUBLANES, tm, LANES), lambda i: (idx, i, 0))
    return pl.pallas_call(
        _ab_out_kernel,
        grid=(n // tm,),
        in_specs=[slab8(0), slab8(0), slab8(0), slab8(0), slab8(AB_S_OG // SUBLANES), slab8(AB_S_HG // SUBLANES),
                  pl.BlockSpec((1, ML_V), lambda i: (0, 0)),
                  pl.BlockSpec((1, HG_V), lambda i: (0, 0)),
                  pl.BlockSpec((AB_OUT, d), lambda i: (0, 0)),
                  pl.BlockSpec((tm, d), lambda i: (i, 0)),
                  pl.BlockSpec((None, 1, d), lambda i: (seg_fn(i * tm), 0, 0))],
        out_specs=pl.BlockSpec((tm, d), lambda i: (i, 0)),
        out_shape=jax.ShapeDtypeStruct((n, d), F32),
        scratch_shapes=[pltpu.VMEM((tm, AB_OUT), BF16)],
        compiler_params=_cparams(("parallel",)),
        name="ab_out",
    )(hm[0], hm[1], ho[0], ho[1], p3, p3, ml_g.reshape(1, ML_V), hg_g.reshape(1, HG_V), w_out, x, g1)


def _ab_weight(w_in):
    d = w_in.shape[0]
    o = [0, 2 * ML_QK, 2 * ML_QK + ML_V, 2 * ML_QK + 2 * ML_V]
    g0 = o[3]
    h0 = g0 + 4 * ML_H
    pad = AB_SLABS * LANES - (w_in.shape[1])
    return jnp.concatenate([w_in[:, :g0], w_in[:, h0:], w_in[:, g0:h0], jnp.zeros((d, pad), w_in.dtype)],
                           axis=1).astype(BF16)


def _ab_layer(x, dims, seg_fn, tm, n_out, g_norm, sc, sh, g1, w_in, conv_w, conv_b, ig_b, fg_b, lb, ml_g, hg_g,
              w_out):
    p3 = _inproj(x, g_norm, sc, sh, _ab_weight(w_in), seg_fn, tm, 6 * LANES)
    kscale = jnp.concatenate([jnp.ones((ML_QK,), F32), jnp.full((ML_QK,), ML_DK ** -0.5, F32)])
    qk = _conv_slabs(p3, AB_S_Q, 2 * ML_H, _conv_pack(conv_w, conv_b, kscale), dims, 2 * ML_H)
    n = x.shape[0]
    gates = p3[AB_S_GATE, :, :4 * ML_H]
    grow = gates.reshape(n // CHUNK, CHUNK, 4 * ML_H).transpose(0, 2, 1)
    gbias = jnp.concatenate([ig_b.reshape(-1), fg_b.reshape(-1)])
    hm = _mlstm(qk, p3, grow, gbias.reshape(1, -1), gbias.reshape(-1, 1), dims)
    ho = _hgrn2(p3, lb.reshape(HG_H, 1, HG_DK), dims)
    return _ab_out(hm, ho, p3, ml_g, hg_g, w_out.astype(BF16), x, g1, seg_fn, n_out, min(tm, 256))


def _ssd_kernel(xf_ref, dcf_ref, drf_ref, xb_ref, dcb_ref, drb_ref, bc_ref, br_ref, ac_ref, ar_ref,
                of_ref, ob_ref, h_ref):
    @pl.when(pl.program_id(1) == 0)
    def _():
        h_ref[...] = jnp.zeros_like(h_ref)

    for d, (x_ref, dc_ref, dr_ref, o_ref) in enumerate(((xf_ref, dcf_ref, drf_ref, of_ref),
                                                        (xb_ref, dcb_ref, drb_ref, ob_ref))):
        rev = d == 1
        mask = _tri(rev)
        last = 0 if rev else CHUNK - 1
        hs = slice(d * SSD_H, (d + 1) * SSD_H)
        dtc = jax.nn.softplus(dc_ref[0][:, hs] + bc_ref[:, hs])
        lac = dtc * ac_ref[:, hs]
        dtr = jax.nn.softplus(dr_ref[hs, :] + br_ref[hs, :])
        lar = dtr * ar_ref[hs, :]
        cum_c = _dot_hi(mask.astype(F32), lac)
        cum_r = _dot_hi(lar, _tri(not rev).astype(F32))
        ecum = jnp.exp(cum_c)
        wgt = jnp.exp(cum_c[last:last + 1, :] - cum_c) * dtc
        for g in range(SSD_G):
            x = jnp.concatenate([x_ref[4 * g + j] for j in range(4)], axis=1)
            bm = x_ref[4 * SSD_G + g]
            cm = x_ref[5 * SSD_G + g]
            cb = _dot_nt(cm, bm)
            hst = h_ref[d, g]
            yoff = _dot(cm, hst)
            ys, xw, dec = [], [], []
            for r in range(SSD_R):
                hh = g * SSD_R + r
                ps = slice(r * SSD_P, (r + 1) * SSD_P)
                seg = jnp.exp(jnp.where(mask, cum_c[:, hh:hh + 1] - cum_r[hh:hh + 1, :], -jnp.inf))
                seg = seg * dtr[hh:hh + 1, :]
                ys.append(_dot(cb * seg, x[:, ps]) + ecum[:, hh:hh + 1] * yoff[:, ps])
                xw.append(x[:, ps] * wgt[:, hh:hh + 1])
                dec.append(jnp.broadcast_to(ecum[last:last + 1, hh:hh + 1], (1, SSD_P)))
            y = jnp.concatenate(ys, axis=1)
            for j in range(4):
                o_ref[4 * g + j] = y[:, j * LANES:(j + 1) * LANES]
            h_ref[d, g] = hst * jnp.concatenate(dec, axis=1) + _dot_tn(bm, jnp.concatenate(xw, axis=1))


def _ssd_scan(xbc, p3, dtrow, dt_b, neg_a, dims):
    b = dims[0]
    n = p3.shape[1]
    fwd, bwd, nch = _chunk_maps(dims)
    nxs = xbc.shape[0]

    def specs(cm):
        return [pl.BlockSpec((nxs, CHUNK, LANES), lambda bi, i: (0, cm(bi, i), 0)),
                pl.BlockSpec((1, CHUNK, LANES), lambda bi, i: (SSD_S_DT, cm(bi, i), 0)),
                pl.BlockSpec((None, 2 * SSD_H, CHUNK), lambda bi, i: (cm(bi, i), 0, 0))]

    vec = lambda shape: pl.BlockSpec(shape, lambda bi, i: (0, 0))
    out_sd = jax.ShapeDtypeStruct((D_INNER // LANES, n, LANES), F32)
    return pl.pallas_call(
        _ssd_kernel,
        grid=(b, nch),
        in_specs=specs(fwd) + specs(bwd) + [vec((1, 2 * SSD_H)), vec((2 * SSD_H, 1)),
                                            vec((1, 2 * SSD_H)), vec((2 * SSD_H, 1))],
        out_specs=[pl.BlockSpec((D_INNER // LANES, CHUNK, LANES), lambda bi, i: (0, fwd(bi, i), 0)),
                   pl.BlockSpec((D_INNER // LANES, CHUNK, LANES), lambda bi, i: (0, bwd(bi, i), 0))],
        out_shape=[out_sd, out_sd],
        scratch_shapes=[pltpu.VMEM((2, SSD_G, SSD_N, SSD_R * SSD_P), F32)],
        compiler_params=_cparams(("parallel", "arbitrary")),
        name="ssd_scan",
    )(xbc, p3, dtrow, xbc, p3, dtrow, dt_b.reshape(1, -1), dt_b.reshape(-1, 1),
      neg_a.reshape(1, -1), neg_a.reshape(-1, 1))


def _ssd_out_kernel(yf_ref, yb_ref, xs_ref, z_ref, dsk_ref, ng_ref, w_ref, x_ref, g1_ref, o_ref, lhs_ref, acc_ref):
    k = pl.program_id(1)

    @pl.when(k == 0)
    def _():
        acc_ref[...] = jnp.zeros_like(acc_ref)

    gw = D_INNER // SSD_G
    for gg in range(lhs_ref.shape[1] // gw):
        cat = lambda ref: jnp.concatenate([ref[4 * gg + j] for j in range(4)], axis=1)
        cs = slice(gg * gw, (gg + 1) * gw)
        y = cat(yf_ref) + cat(yb_ref) + dsk_ref[:, cs] * cat(xs_ref)
        u = y * _silu(cat(z_ref))
        u = u * lax.rsqrt(jnp.mean(u * u, axis=-1, keepdims=True) + EPS) * ng_ref[:, cs]
        lhs_ref[:, cs] = u.astype(BF16)
    acc_ref[...] += jnp.dot(lhs_ref[...], w_ref[...], preferred_element_type=F32)

    @pl.when(k == pl.num_programs(1) - 1)
    def _():
        o_ref[...] = x_ref[...] + g1_ref[...] * acc_ref[...]


def _ssd_out(yf, yb, xbc, p3, dskip, norm_g, w_out, x, g1, seg_fn, n_rows, tm):
    d = x.shape[1]
    tk = 1024
    nsl = tk // LANES
    slab = lambda: pl.BlockSpec((nsl, tm, LANES), lambda i, k: (k, i, 0))
    return pl.pallas_call(
        _ssd_out_kernel,
        grid=(n_rows // tm, D_INNER // tk),
        in_specs=[slab(), slab(), slab(), slab(),
                  pl.BlockSpec((1, tk), lambda i, k: (0, k)),
                  pl.BlockSpec((1, tk), lambda i, k: (0, k)),
                  pl.BlockSpec((tk, d), lambda i, k: (k, 0)),
                  pl.BlockSpec((tm, d), lambda i, k: (i, 0)),
                  pl.BlockSpec((None, 1, d), lambda i, k: (seg_fn(i * tm), 0, 0))],
        out_specs=pl.BlockSpec((tm, d), lambda i, k: (i, 0)),
        out_shape=jax.ShapeDtypeStruct((n_rows, d), F32),
        scratch_shapes=[pltpu.VMEM((tm, tk), BF16), pltpu.VMEM((tm, d), F32)],
        compiler_params=_cparams(("parallel", "arbitrary")),
        name="ssd_out",
    )(yf, yb, xbc, p3, dskip, norm_g.reshape(1, -1), w_out, x, g1)


def _ssd_layer(x, dims, seg_fn, tm, n_out, g_norm, sc, sh, g1, w_in, conv_w, conv_b, dt_b, a_log, d_skip,
               norm_g, w_out):
    d = x.shape[1]
    n = x.shape[0]
    pad = SSD_SLABS * LANES - w_in.shape[1]
    w = jnp.concatenate([w_in, jnp.zeros((d, pad), w_in.dtype)], axis=1).astype(BF16)
    p3 = _inproj(x, g_norm, sc, sh, w, seg_fn, tm, 7 * LANES)
    nconv = conv_w.shape[1]
    xbc = _conv_slabs(p3, SSD_S_X, nconv // LANES, _conv_pack(conv_w, conv_b, jnp.ones((nconv,), F32)), dims, 16)
    dt = p3[SSD_S_DT]
    dtrow = dt.reshape(n // CHUNK, CHUNK, 2 * SSD_H).transpose(0, 2, 1)
    yf, yb = _ssd_scan(xbc, p3, dtrow, dt_b.reshape(-1), -jnp.exp(a_log.astype(F32)).reshape(-1), dims)
    dskip = jnp.repeat(d_skip.astype(F32), SSD_P).reshape(1, D_INNER)
    return _ssd_out(yf, yb, xbc, p3, dskip, norm_g, w_out.astype(BF16), x, g1, seg_fn, n_out, min(tm, 256))


def _regroup_kernel(x_ref, o_ref):
    a = x_ref.shape[0]
    for cc in range(SUBLANES):
        o_ref[cc * a:(cc + 1) * a, :] = x_ref[:, cc, :]


def _regroup(x, b, a, c, n_out):
    d = x.shape[1]
    cblk = c // SUBLANES
    return pl.pallas_call(
        _regroup_kernel,
        grid=(b, cblk),
        in_specs=[pl.BlockSpec((a, SUBLANES, d), lambda bi, j: (bi, j, 0))],
        out_specs=pl.BlockSpec((a * SUBLANES, d), lambda bi, j: (bi * cblk + j, 0)),
        out_shape=jax.ShapeDtypeStruct((n_out, d), F32),
        compiler_params=_cparams(("parallel", "parallel")),
        name="regroup",
    )(x.reshape(x.shape[0] // c, c, d))


MOE_TILE = 512


def _router_kernel(x_ref, g_ref, sc_ref, sh_ref, wt_ref, rb_ref, hn_ref, ii_ref, iw_ref, cnt_ref, carry_ref):
    @pl.when(pl.program_id(0) == 0)
    def _():
        carry_ref[...] = jnp.zeros_like(carry_ref)

    x = x_ref[...]
    tm = x.shape[0]
    y = x * lax.rsqrt(jnp.mean(x * x, axis=-1, keepdims=True) + EPS) * g_ref[...]
    hn = y * (1.0 + sc_ref[...]) + sh_ref[...]
    hn_ref[...] = hn
    logits = lax.dot_general(wt_ref[...], hn, (((1,), (1,)), ((), ())), precision=HI,
                             preferred_element_type=F32)
    score = jax.nn.sigmoid(logits)
    biased = score + rb_ref[...]
    rb = [biased[e:e + 1, :] for e in range(N_EXPERTS)]
    rs = [score[e:e + 1, :] for e in range(N_EXPERTS)]
    gsc = []
    for g in range(N_GROUPS):
        a, b, c, d = rb[EXPERTS_PER_GROUP * g:EXPERTS_PER_GROUP * (g + 1)]
        hi1, lo1, hi2, lo2 = jnp.maximum(a, b), jnp.minimum(a, b), jnp.maximum(c, d), jnp.minimum(c, d)
        gsc.append(jnp.maximum(hi1, hi2) + jnp.maximum(jnp.minimum(hi1, hi2), jnp.maximum(lo1, lo2)))
    best = jnp.zeros((1, tm), jnp.int32)
    bsc = gsc[0]
    for g in range(1, N_GROUPS):
        upd = gsc[g] > bsc
        best = jnp.where(upd, g, best)
        bsc = jnp.where(upd, gsc[g], bsc)

    def pick(rows, p):
        out = rows[p]
        for g in range(1, N_GROUPS):
            out = jnp.where(best == g, rows[EXPERTS_PER_GROUP * g + p], out)
        return out

    vals = [pick(rb, p) for p in range(EXPERTS_PER_GROUP)]
    scs = [pick(rs, p) for p in range(EXPERTS_PER_GROUP)]
    p1, v1, s1 = jnp.zeros((1, tm), jnp.int32), vals[0], scs[0]
    for p in range(1, EXPERTS_PER_GROUP):
        upd = vals[p] > v1
        p1, v1, s1 = jnp.where(upd, p, p1), jnp.where(upd, vals[p], v1), jnp.where(upd, scs[p], s1)
    p2 = jnp.zeros((1, tm), jnp.int32)
    v2 = jnp.full((1, tm), -jnp.inf, F32)
    s2 = jnp.zeros((1, tm), F32)
    for p in range(EXPERTS_PER_GROUP):
        upd = (p1 != p) & (vals[p] > v2)
        p2, v2, s2 = jnp.where(upd, p, p2), jnp.where(upd, vals[p], v2), jnp.where(upd, scs[p], s2)
    e1 = best * EXPERTS_PER_GROUP + p1
    e2 = best * EXPERTS_PER_GROUP + p2
    tot = s1 + s2
    eiota = lax.broadcasted_iota(jnp.int32, (N_EXPERTS, tm), 0)
    oh1 = (eiota == e1).astype(F32)
    oh2 = (eiota == e2).astype(F32)
    oh = oh1 + oh2
    before = (lax.broadcasted_iota(jnp.int32, (tm, tm), 0) < lax.broadcasted_iota(jnp.int32, (tm, tm), 1))
    cnt = _dot(oh, before.astype(BF16)) + carry_ref[:, :1]
    r1 = jnp.sum(oh1 * cnt, axis=0, keepdims=True).astype(jnp.int32)
    r2 = jnp.sum(oh2 * cnt, axis=0, keepdims=True).astype(jnp.int32)
    zi = jnp.zeros((SUBLANES - 4, tm), jnp.int32)
    ii_ref[...] = jnp.concatenate([e1, e2, r1, r2, zi], axis=0)
    iw_ref[...] = jnp.concatenate([s1 / tot, s2 / tot, jnp.zeros((SUBLANES - 2, tm), F32)], axis=0)
    carry = carry_ref[...] + jnp.sum(oh, axis=1, keepdims=True)
    carry_ref[...] = carry
    cnt_ref[...] = carry


def _router(x, n_rows, g, sc, sh, seg_fn, router_wt, router_b):
    d = x.shape[1]
    tm = MOE_TILE
    return pl.pallas_call(
        _router_kernel,
        grid=(n_rows // tm,),
        in_specs=[pl.BlockSpec((tm, d), lambda i: (i, 0)),
                  pl.BlockSpec((1, d), lambda i: (0, 0)),
                  pl.BlockSpec((None, 1, d), lambda i: (seg_fn(i * tm), 0, 0)),
                  pl.BlockSpec((None, 1, d), lambda i: (seg_fn(i * tm), 0, 0)),
                  pl.BlockSpec((N_EXPERTS, d), lambda i: (0, 0)),
                  pl.BlockSpec((N_EXPERTS, 1), lambda i: (0, 0))],
        out_specs=[pl.BlockSpec((tm, d), lambda i: (i, 0)),
                   pl.BlockSpec((SUBLANES, tm), lambda i: (0, i)),
                   pl.BlockSpec((SUBLANES, tm), lambda i: (0, i)),
                   pl.BlockSpec((N_EXPERTS, LANES), lambda i: (0, 0))],
        out_shape=[jax.ShapeDtypeStruct((n_rows, d), F32),
                   jax.ShapeDtypeStruct((SUBLANES, n_rows), jnp.int32),
                   jax.ShapeDtypeStruct((SUBLANES, n_rows), F32),
                   jax.ShapeDtypeStruct((N_EXPERTS, LANES), F32)],
        scratch_shapes=[pltpu.VMEM((N_EXPERTS, LANES), F32)],
        compiler_params=_cparams(("arbitrary",)),
        name="router",
    )(x, g.reshape(1, d), sc, sh, router_wt, router_b.reshape(N_EXPERTS, 1))


def _row_copy(src_ref, o_ref, sem, src_row, dst_row):
    return pltpu.make_async_copy(src_ref.at[pl.ds(src_row, 1)], o_ref.at[pl.ds(dst_row, 1)], sem)


def _gather_kernel(idx_ref, src_ref, o_ref, sem, *, rows):
    base = pl.program_id(0) * rows

    def issue(r, carry):
        _row_copy(src_ref, o_ref, sem, idx_ref[r], base + r).start()
        return carry

    def drain(r, carry):
        _row_copy(src_ref, o_ref, sem, 0, base + r).wait()
        return carry

    lax.fori_loop(0, rows, issue, 0)
    lax.fori_loop(0, rows, drain, 0)


def _gather_rows(idx, src, n_out):
    rows = MOE_TILE
    return pl.pallas_call(
        functools.partial(_gather_kernel, rows=rows),
        grid=(n_out // rows,),
        in_specs=[pl.BlockSpec((rows,), lambda i: (i,), memory_space=pltpu.SMEM),
                  pl.BlockSpec(memory_space=pl.ANY)],
        out_specs=pl.BlockSpec(memory_space=pl.ANY),
        out_shape=jax.ShapeDtypeStruct((n_out, src.shape[1]), src.dtype),
        scratch_shapes=[pltpu.SemaphoreType.DMA],
        compiler_params=_cparams(("arbitrary",)),
        name="gather_rows",
    )(idx, src)


def _expert_kernel(te_ref, nu_ref, x_ref, w1_ref, w3_ref, w2_ref, o_ref, xb_ref, acc_ref):
    i, f = pl.program_id(0), pl.program_id(1)
    nf = pl.num_programs(1)
    used = i < nu_ref[0]

    @pl.when(used & (f == 0))
    def _():
        xb_ref[...] = x_ref[...].astype(BF16)
        acc_ref[...] = jnp.zeros_like(acc_ref)

    @pl.when(used)
    def _():
        xb = xb_ref[...]
        a = _silu(_dot(xb, w1_ref[...])) * _dot(xb, w3_ref[...])
        acc_ref[...] += _dot(a, w2_ref[...])

    @pl.when(f == nf - 1)
    def _():
        o_ref[...] = jnp.where(used, acc_ref[...], 0.0)


def _experts(tile_e, n_used, xs, w1, w3, w2):
    p, d = xs.shape
    te, tf = MOE_TILE, 256
    nf = D_EXPERT // tf
    fidx = lambda i, f, nu: jnp.where(i < nu[0], f, nf - 1)
    grid_spec = pltpu.PrefetchScalarGridSpec(
        num_scalar_prefetch=2,
        grid=(p // te, nf),
        in_specs=[pl.BlockSpec((te, d), lambda i, f, te_r, nu: (i, 0)),
                  pl.BlockSpec((None, d, tf), lambda i, f, te_r, nu: (te_r[i], 0, fidx(i, f, nu))),
                  pl.BlockSpec((None, d, tf), lambda i, f, te_r, nu: (te_r[i], 0, fidx(i, f, nu))),
                  pl.BlockSpec((None, tf, d), lambda i, f, te_r, nu: (te_r[i], fidx(i, f, nu), 0))],
        out_specs=pl.BlockSpec((te, d), lambda i, f, te_r, nu: (i, 0)),
        scratch_shapes=[pltpu.VMEM((te, d), BF16), pltpu.VMEM((te, d), F32)])
    return pl.pallas_call(
        _expert_kernel,
        grid_spec=grid_spec,
        out_shape=jax.ShapeDtypeStruct((p, d), F32),
        compiler_params=_cparams(("arbitrary", "arbitrary")),
        name="experts",
    )(tile_e, n_used, xs, w1, w3, w2)


def _combine_kernel(x_ref, y1_ref, y2_ref, w_ref, g2_ref, fg_ref, o_ref, *, final):
    w = w_ref[...]
    out = x_ref[...] + g2_ref[...] * (w[:, 0:1] * y1_ref[...] + w[:, 1:2] * y2_ref[...])
    if final:
        out = out * lax.rsqrt(jnp.mean(out * out, axis=-1, keepdims=True) + EPS) * fg_ref[...]
    o_ref[...] = out


def _combine(x, n_rows, yg, wts, g2, seg_fn, final_g):
    d = x.shape[1]
    tm = 256
    nt = n_rows // tm
    fg = jnp.ones((1, d), F32) if final_g is None else final_g.reshape(1, d)
    return pl.pallas_call(
        functools.partial(_combine_kernel, final=final_g is not None),
        grid=(nt,),
        in_specs=[pl.BlockSpec((tm, d), lambda i: (i, 0)),
                  pl.BlockSpec((tm, d), lambda i: (i, 0)),
                  pl.BlockSpec((tm, d), lambda i: (nt + i, 0)),
                  pl.BlockSpec((tm, 2), lambda i: (i, 0)),
                  pl.BlockSpec((None, 1, d), lambda i: (seg_fn(i * tm), 0, 0)),
                  pl.BlockSpec((1, d), lambda i: (0, 0))],
        out_specs=pl.BlockSpec((tm, d), lambda i: (i, 0)),
        out_shape=jax.ShapeDtypeStruct((n_rows, d), F32),
        compiler_params=_cparams(("parallel",)),
        name="combine",
    )(x, yg, yg, wts, g2, fg)


def _moe(x, n_rows, seg_fn, g, sc, sh, g2, router_wt, router_b, w1, w3, w2, final_g=None):
    te = MOE_TILE
    hn, ii, iw, cnt = _router(x, n_rows, g, sc, sh, seg_fn, router_wt, router_b)
    counts = cnt[:, 0].astype(jnp.int32)
    padded = (counts + te - 1) // te * te
    ends = jnp.cumsum(padded)
    base = ends - padded
    pos1 = base[ii[0]] + ii[2]
    pos2 = base[ii[1]] + ii[3]
    p = 2 * n_rows + N_EXPERTS * te
    ntile = p // te
    n_used = ends[-1] // te
    tiles = jnp.arange(ntile, dtype=jnp.int32)
    tile_e = jnp.minimum(jnp.searchsorted(ends, tiles * te, side='right'), N_EXPERTS - 1).astype(jnp.int32)
    tile_e = jnp.where(tiles < n_used, tile_e, tile_e[jnp.maximum(n_used - 1, 0)])
    tok = jnp.arange(n_rows, dtype=jnp.int32)
    src = jnp.zeros((p,), jnp.int32).at[pos1].set(tok).at[pos2].set(tok)
    xs = _gather_rows(src, hn, p)
    ys = _experts(tile_e, n_used.reshape(1).astype(jnp.int32), xs, w1, w3, w2)
    yg = _gather_rows(jnp.concatenate([pos1, pos2]), ys, 2 * n_rows)
    return _combine(x, n_rows, yg, iw[:2].T, g2, seg_fn, final_g)


def kernel(x, c, ctx, c_ctx, ada_w, ada_b, norm1_g, norm2_g, ab_w_in, ab_conv_w, ab_conv_b, ml_ig_b, ml_fg_b,
           hg_lb, ml_norm_g, hg_norm_g, ab_w_out, ssd_w_in, ssd_conv_w, ssd_conv_b, ssd_dt_b, ssd_a_log, ssd_d,
           ssd_norm_g, ssd_w_out, router_w, router_b, moe_w1, moe_w3, moe_w2, final_g):
    b, seq, d = x.shape
    ctxl = ctx.shape[1]
    depth = ada_w.shape[0]
    dims = (b, seq, ctxl)
    nl, nc = b * seq, b * ctxl
    rows = seq // GRID_W
    tm = min(512, nc)
    seg_fn = lambda row: jnp.where(row < nl, row // seq, b)
    c8 = jnp.concatenate([c, c_ctx[None], jnp.zeros((SUBLANES - b - 1, d), F32)])
    mods = _modulation(c8, ada_w, ada_b)[:, :b + 1].reshape(depth, b + 1, 6, 1, d)
    lb_all = jnp.cumsum(jax.nn.softmax(hg_lb.astype(F32), axis=0), axis=0)
    router_wt = router_w.T
    xr = jnp.concatenate([x.reshape(nl, d), ctx.reshape(nc, d)])
    transposed = False
    for l in range(depth):
        sh1, sc1, g1, sh2, sc2, g2 = (mods[l][:, k] for k in range(6))
        keep_ctx = l < depth - 1
        n_out = nl + nc if keep_ctx else nl
        j = l // 2
        if (l % 2 == 1) != transposed:
            xt = _regroup(xr, b, GRID_W if transposed else rows, rows if transposed else GRID_W, nl)
            xr = jnp.concatenate([xt, xr[nl:]])
            transposed = not transposed
        if l % 2 == 0:
            xr = _ab_layer(xr, dims, seg_fn, tm, n_out, norm1_g[l], sc1, sh1, g1, ab_w_in[j], ab_conv_w[j],
                           ab_conv_b[j], ml_ig_b[j], ml_fg_b[j], lb_all[l], ml_norm_g[j], hg_norm_g[j], ab_w_out[j])
        else:
            xr = _ssd_layer(xr, dims, seg_fn, tm, n_out, norm1_g[l], sc1, sh1, g1, ssd_w_in[j], ssd_conv_w[j],
                            ssd_conv_b[j], ssd_dt_b[j], ssd_a_log[j], ssd_d[j], ssd_norm_g[j], ssd_w_out[j])
        xr = _moe(xr, n_out, seg_fn, norm2_g[l], sc2, sh2, g2, router_wt, router_b, moe_w1[l], moe_w3[l],
                  moe_w2[l], final_g if l == depth - 1 else None)
    if transposed:
        xr = _regroup(xr, b, GRID_W, rows, nl)
    return xr[:nl].reshape(b, seq, d)
```

```python
import functools
import math

import jax
import jax.numpy as jnp
from jax import lax
from jax.experimental import pallas as pl
from jax.experimental.pallas import tpu as pltpu

F32 = jnp.float32
BF16 = jnp.bfloat16
HI = lax.Precision.HIGHEST

D_MODEL = 2048
GRID_W = 64
EPS = 1e-6
CHUNK = 64
CONV_K = 5
ML_H, ML_DK, ML_DV = 4, 128, 256
HG_H, HG_DK, HG_DV = 8, 128, 128
ML_QK, ML_V = ML_H * ML_DK, ML_H * ML_DV
HG_K, HG_V = HG_H * HG_DK, HG_H * HG_DV
AB_OUT = ML_V + HG_V
D_INNER = 2 * D_MODEL
SSD_P, SSD_G, SSD_N = 64, 8, 128
SSD_H = D_INNER // SSD_P
SSD_R = SSD_H // SSD_G
N_EXPERTS, N_GROUPS, TOP_K, D_EXPERT = 16, 4, 2, 1024
EXPERTS_PER_GROUP = N_EXPERTS // N_GROUPS

LANES = 128
SUBLANES = 8
VMEM_LIMIT = 56 * 1024 * 1024

AB_S_Q, AB_S_K, AB_S_V, AB_S_OG, AB_S_HQ, AB_S_HF, AB_S_HI, AB_S_HG, AB_S_GATE = 0, 4, 8, 16, 24, 32, 48, 56, 64
AB_SLABS = 66
SSD_S_Z, SSD_S_X, SSD_S_B, SSD_S_C, SSD_S_DT = 0, 32, 64, 72, 80
SSD_SLABS = 81


def _cparams(sem):
    return pltpu.CompilerParams(dimension_semantics=sem, vmem_limit_bytes=VMEM_LIMIT)


def _silu(x):
    return x * jax.nn.sigmoid(x)


def _dot(a, b):
    return jnp.dot(a.astype(BF16), b.astype(BF16), preferred_element_type=F32)


def _dot_nt(a, b):
    return lax.dot_general(a.astype(BF16), b.astype(BF16), (((1,), (1,)), ((), ())),
                           preferred_element_type=F32)


def _dot_tn(a, b):
    return lax.dot_general(a.astype(BF16), b.astype(BF16), (((0,), (0,)), ((), ())),
                           preferred_element_type=F32)


def _dot_hi(a, b):
    return jnp.dot(a, b, precision=HI, preferred_element_type=F32)


def _tri(rev):
    t = lax.broadcasted_iota(jnp.int32, (CHUNK, CHUNK), 0)
    s = lax.broadcasted_iota(jnp.int32, (CHUNK, CHUNK), 1)
    return (s >= t) if rev else (s <= t)


def _mod_kernel(c_ref, w_ref, b_ref, o_ref):
    c = c_ref[...]
    o_ref[...] = _dot(_silu(c), w_ref[...]) + b_ref[...]


def _modulation(c8, ada_w, ada_b):
    depth, d, d6 = ada_w.shape
    tn = 1024
    return pl.pallas_call(
        _mod_kernel,
        grid=(depth, d6 // tn),
        in_specs=[pl.BlockSpec((SUBLANES, d), lambda l, j: (0, 0)),
                  pl.BlockSpec((None, d, tn), lambda l, j: (l, 0, j)),
                  pl.BlockSpec((None, 1, tn), lambda l, j: (l, 0, j))],
        out_specs=pl.BlockSpec((None, SUBLANES, tn), lambda l, j: (l, 0, j)),
        out_shape=jax.ShapeDtypeStruct((depth, SUBLANES, d6), F32),
        compiler_params=_cparams(("parallel", "parallel")),
        name="modulation",
    )(c8, ada_w, ada_b.reshape(depth, 1, d6))


def _inproj_kernel(x_ref, g_ref, sc_ref, sh_ref, w_ref, o_ref, h_ref):
    @pl.when(pl.program_id(1) == 0)
    def _():
        x = x_ref[...]
        y = x * lax.rsqrt(jnp.mean(x * x, axis=-1, keepdims=True) + EPS) * g_ref[...]
        h_ref[...] = (y * (1.0 + sc_ref[...]) + sh_ref[...]).astype(BF16)

    acc = jnp.dot(h_ref[...], w_ref[...], preferred_element_type=F32)
    for s in range(o_ref.shape[0]):
        o_ref[s] = acc[:, s * LANES:(s + 1) * LANES]


def _inproj(x, g, sc, sh, w, seg_fn, tm, tn):
    n, d = x.shape
    ncols = w.shape[1]
    nsl = tn // LANES
    return pl.pallas_call(
        _inproj_kernel,
        grid=(n // tm, ncols // tn),
        in_specs=[pl.BlockSpec((tm, d), lambda i, j: (i, 0)),
                  pl.BlockSpec((1, d), lambda i, j: (0, 0)),
                  pl.BlockSpec((None, 1, d), lambda i, j: (seg_fn(i * tm), 0, 0)),
                  pl.BlockSpec((None, 1, d), lambda i, j: (seg_fn(i * tm), 0, 0)),
                  pl.BlockSpec((d, tn), lambda i, j: (0, j))],
        out_specs=pl.BlockSpec((nsl, tm, LANES), lambda i, j: (j, i, 0)),
        out_shape=jax.ShapeDtypeStruct((ncols // LANES, n, LANES), F32),
        scratch_shapes=[pltpu.VMEM((tm, d), BF16)],
        compiler_params=_cparams(("parallel", "arbitrary")),
        name="inproj",
    )(x, g.reshape(1, d), sc, sh, w)


def _conv_kernel(prev_ref, cur_ref, next_ref, w_ref, o_ref, *, tt, tiles_lat, n_lat_tiles, tiles_ctx):
    i = pl.program_id(1)
    in_lat = i < n_lat_tiles
    pos = jnp.where(in_lat, i % tiles_lat, (i - n_lat_tiles) % tiles_ctx)
    last = jnp.where(in_lat, tiles_lat - 1, tiles_ctx - 1)
    keep_prev = (pos != 0).astype(F32)
    keep_next = (pos != last).astype(F32)
    ext = jnp.concatenate([prev_ref[...] * keep_prev, cur_ref[...], next_ref[...] * keep_next], axis=1)
    w = w_ref[...]
    acc = jnp.zeros(cur_ref.shape, F32) + w[:, CONV_K:CONV_K + 1, :]
    for k in range(CONV_K):
        shift = (CONV_K // 2 - k) % (tt + 2 * SUBLANES)
        r = ext if shift == 0 else pltpu.roll(ext, shift, 1)
        acc = acc + r[:, SUBLANES:SUBLANES + tt, :] * w[:, k:k + 1, :]
    o_ref[...] = _silu(acc) * w[:, CONV_K + 1:CONV_K + 2, :]


def _conv_slabs(p3, slab0, nslab, wpack, dims, sb):
    b, seq, ctxl = dims
    n = p3.shape[1]
    tt = min(256, ctxl)
    t8 = tt // SUBLANES
    nblk8 = n // SUBLANES
    s0 = slab0 // sb
    kern = functools.partial(_conv_kernel, tt=tt, tiles_lat=seq // tt, n_lat_tiles=b * seq // tt,
                             tiles_ctx=ctxl // tt)
    return pl.pallas_call(
        kern,
        grid=(nslab // sb, n // tt),
        in_specs=[pl.BlockSpec((sb, SUBLANES, LANES), lambda s, i: (s0 + s, jnp.maximum(i * t8 - 1, 0), 0)),
                  pl.BlockSpec((sb, tt, LANES), lambda s, i: (s0 + s, i, 0)),
                  pl.BlockSpec((sb, SUBLANES, LANES),
                               lambda s, i: (s0 + s, jnp.minimum((i + 1) * t8, nblk8 - 1), 0)),
                  pl.BlockSpec((sb, SUBLANES, LANES), lambda s, i: (s, 0, 0))],
        out_specs=pl.BlockSpec((sb, tt, LANES), lambda s, i: (s, i, 0)),
        out_shape=jax.ShapeDtypeStruct((nslab, n, LANES), F32),
        compiler_params=_cparams(("parallel", "parallel")),
        name="conv",
    )(p3, p3, p3, wpack)


def _conv_pack(conv_w, conv_b, scale):
    c = conv_w.shape[1]
    rows = jnp.concatenate([conv_w, conv_b[None], scale[None], jnp.zeros((1, c), F32)], axis=0)
    return rows.reshape(SUBLANES, c // LANES, LANES).transpose(1, 0, 2)


def _chunk_maps(dims):
    b, seq, ctxl = dims
    ncc, nlc = ctxl // CHUNK, seq // CHUNK

    def fwd(bi, i):
        return jnp.where(i < ncc, b * nlc + bi * ncc + i, bi * nlc + (i - ncc))

    def bwd(bi, i):
        return jnp.where(i < ncc, b * nlc + bi * ncc + (ncc - 1 - i), bi * nlc + (nlc - 1 - (i - ncc)))

    return fwd, bwd, ncc + nlc


def _mlstm_kernel(qkf_ref, vf_ref, gcf_ref, grf_ref, qkb_ref, vb_ref, gcb_ref, grb_ref, bc_ref, br_ref,
                  of_ref, ob_ref, c_ref, n_ref, m_ref):
    @pl.when(pl.program_id(1) == 0)
    def _():
        c_ref[...] = jnp.zeros_like(c_ref)
        n_ref[...] = jnp.zeros_like(n_ref)
        m_ref[...] = jnp.zeros_like(m_ref)

    ng = 2 * ML_H
    for d, (qk_ref, v_ref, gc_ref, gr_ref, o_ref) in enumerate(
            ((qkf_ref, vf_ref, gcf_ref, grf_ref, of_ref), (qkb_ref, vb_ref, gcb_ref, grb_ref, ob_ref))):
        rev = d == 1
        mask = _tri(rev)
        gcol = gc_ref[0][:, :2 * ng] + bc_ref[...]
        grow = gr_ref[...] + br_ref[...]
        ic_all = gcol[:, d * ML_H:(d + 1) * ML_H]
        lfc_all = jax.nn.log_sigmoid(gcol[:, ng + d * ML_H:ng + (d + 1) * ML_H])
        ir_all = grow[d * ML_H:(d + 1) * ML_H, :]
        lfr_all = jax.nn.log_sigmoid(grow[ng + d * ML_H:ng + (d + 1) * ML_H, :])
        incl = mask.astype(F32)
        bcol_all = _dot_hi(incl, lfc_all)
        brow_all = _dot_hi(lfr_all, _tri(not rev).astype(F32))
        last = 0 if rev else CHUNK - 1
        for h in range(ML_H):
            q = qk_ref[h]
            k = qk_ref[ML_H + h]
            v = jnp.concatenate([v_ref[2 * h], v_ref[2 * h + 1]], axis=1)
            bcol, icol = bcol_all[:, h:h + 1], ic_all[:, h:h + 1]
            brow, irow = brow_all[h:h + 1, :], ir_all[h:h + 1, :]
            cst = c_ref[d, h]
            nst = n_ref[d, h]
            mprev = m_ref[d, h][:, :1]
            logd = jnp.where(mask, bcol - brow + irow, -jnp.inf)
            inter = bcol + mprev
            m_t = jnp.maximum(inter, jnp.max(logd, axis=1, keepdims=True))
            s = _dot_nt(q, k) * jnp.exp(logd - m_t)
            sc = jnp.exp(inter - m_t)
            num = _dot(s, v) + sc * _dot_nt(q, cst)
            den = jnp.sum(s, axis=1, keepdims=True) + sc * jnp.sum(q * nst, axis=1, keepdims=True)
            hout = num / jnp.maximum(jnp.abs(den), jnp.exp(-m_t))
            o_ref[2 * h] = hout[:, :LANES]
            o_ref[2 * h + 1] = hout[:, LANES:]
            b_last = bcol[last:last + 1, :]
            wlog = b_last - bcol + icol
            m_new = jnp.maximum(b_last + mprev, jnp.max(wlog, axis=0, keepdims=True))
            w = jnp.exp(wlog - m_new)
            dec = jnp.exp(b_last + mprev - m_new)
            c_ref[d, h] = dec * cst + _dot_tn(w * v, k)
            n_ref[d, h] = dec * nst + jnp.sum(w * k, axis=0, keepdims=True)
            m_ref[d, h] = jnp.broadcast_to(m_new, (1, LANES))


def _mlstm(qk, p3, grow, bcol, brow, dims):
    b = dims[0]
    n = p3.shape[1]
    fwd, bwd, nch = _chunk_maps(dims)

    def specs(cm):
        return [pl.BlockSpec((2 * ML_H, CHUNK, LANES), lambda bi, i: (0, cm(bi, i), 0)),
                pl.BlockSpec((2 * ML_H, CHUNK, LANES), lambda bi, i: (AB_S_V // (2 * ML_H), cm(bi, i), 0)),
                pl.BlockSpec((1, CHUNK, LANES), lambda bi, i: (AB_S_GATE, cm(bi, i), 0)),
                pl.BlockSpec((None, 4 * ML_H, CHUNK), lambda bi, i: (cm(bi, i), 0, 0))]

    out_sd = jax.ShapeDtypeStruct((2 * ML_H, n, LANES), F32)
    return pl.pallas_call(
        _mlstm_kernel,
        grid=(b, nch),
        in_specs=specs(fwd) + specs(bwd) + [pl.BlockSpec((1, 4 * ML_H), lambda bi, i: (0, 0)),
                                            pl.BlockSpec((4 * ML_H, 1), lambda bi, i: (0, 0))],
        out_specs=[pl.BlockSpec((2 * ML_H, CHUNK, LANES), lambda bi, i: (0, fwd(bi, i), 0)),
                   pl.BlockSpec((2 * ML_H, CHUNK, LANES), lambda bi, i: (0, bwd(bi, i), 0))],
        out_shape=[out_sd, out_sd],
        scratch_shapes=[pltpu.VMEM((2, ML_H, ML_DV, ML_DK), F32),
                        pltpu.VMEM((2, ML_H, 1, ML_DK), F32),
                        pltpu.VMEM((2, ML_H, 1, LANES), F32)],
        compiler_params=_cparams(("parallel", "arbitrary")),
        name="mlstm",
    )(qk, p3, p3, grow, qk, p3, p3, grow, bcol, brow)


def _bcast_rows(a, rows, span):
    parts = [jnp.broadcast_to(a[r:r + 1, :], (span, a.shape[1])) for r in rows]
    return parts[0] if len(parts) == 1 else jnp.concatenate(parts, axis=0)


def _hgrn2_kernel(qf_ref, ff_ref, vf_ref, qb_ref, fb_ref, vb_ref, lb_ref, of_ref, ob_ref, s_ref):
    @pl.when(pl.program_id(1) == 0)
    def _():
        s_ref[...] = jnp.zeros_like(s_ref)

    t = lax.broadcasted_iota(jnp.int32, (CHUNK, CHUNK), 0)
    s = lax.broadcasted_iota(jnp.int32, (CHUNK, CHUNK), 1)
    for d, (q_ref, f_ref, v_ref, o_ref) in enumerate(((qf_ref, ff_ref, vf_ref, of_ref),
                                                      (qb_ref, fb_ref, vb_ref, ob_ref))):
        rev = d == 1
        mask = _tri(rev)
        incl = mask.astype(F32)
        last = 0 if rev else CHUNK - 1
        for h in range(HG_H):
            q = q_ref[h]
            lb = lb_ref[h]
            f = lb + (1.0 - lb) * jax.nn.sigmoid(f_ref[h])
            k = 1.0 - f
            lg = jnp.log(f)
            v = v_ref[h]
            a = _dot_hi(incl, lg)
            scores = jnp.zeros((CHUNK, CHUNK), F32)
            for m in (32, 16, 8):
                nb = CHUNK // (2 * m)
                ref_rows = [bi * 2 * m + (m if rev else m - 1) for bi in range(nb)]
                aref = _bcast_rows(a, ref_rows, 2 * m)
                eq = jnp.exp(jnp.minimum(a - aref, 0.0))
                ek = jnp.exp(jnp.minimum(aref - a, 0.0))
                same = (t // (2 * m)) == (s // (2 * m))
                t_late = ((t // m) % 2 == 0) if rev else ((t // m) % 2 == 1)
                s_early = ((s // m) % 2 == 1) if rev else ((s // m) % 2 == 0)
                lvl = same & t_late & s_early
                scores = scores + jnp.where(lvl, _dot_nt(q * eq, k * ek), 0.0)
            aex = a - lg
            ref_rows = [bi * SUBLANES + (SUBLANES - 1 if rev else 0) for bi in range(CHUNK // SUBLANES)]
            aref = _bcast_rows(aex, ref_rows, SUBLANES)
            diag = ((t // SUBLANES) == (s // SUBLANES)) & mask
            scores = scores + jnp.where(diag, _dot_nt(q * jnp.exp(a - aref), k * jnp.exp(aref - a)), 0.0)
            st = s_ref[d, h]
            o_ref[h] = _dot(scores, v) + _dot_nt(q * jnp.exp(a), st)
            a_last = a[last:last + 1, :]
            s_ref[d, h] = st * jnp.exp(a_last) + _dot_tn(v, k * jnp.exp(a_last - a))


def _hgrn2(p3, lb, dims):
    b = dims[0]
    n = p3.shape[1]
    fwd, bwd, nch = _chunk_maps(dims)

    def specs(cm, d):
        return [pl.BlockSpec((HG_H, CHUNK, LANES), lambda bi, i: (AB_S_HQ // HG_H, cm(bi, i), 0)),
                pl.BlockSpec((HG_H, CHUNK, LANES), lambda bi, i: (AB_S_HF // HG_H + d, cm(bi, i), 0)),
                pl.BlockSpec((HG_H, CHUNK, LANES), lambda bi, i: (AB_S_HI // HG_H, cm(bi, i), 0))]

    out_sd = jax.ShapeDtypeStruct((HG_H, n, LANES), F32)
    return pl.pallas_call(
        _hgrn2_kernel,
        grid=(b, nch),
        in_specs=specs(fwd, 0) + specs(bwd, 1) + [pl.BlockSpec((HG_H, 1, LANES), lambda bi, i: (0, 0, 0))],
        out_specs=[pl.BlockSpec((HG_H, CHUNK, LANES), lambda bi, i: (0, fwd(bi, i), 0)),
                   pl.BlockSpec((HG_H, CHUNK, LANES), lambda bi, i: (0, bwd(bi, i), 0))],
        out_shape=[out_sd, out_sd],
        scratch_shapes=[pltpu.VMEM((2, HG_H, HG_DV, HG_DK), F32)],
        compiler_params=_cparams(("parallel", "arbitrary")),
        name="hgrn2",
    )(p3, p3, p3, p3, p3, p3, lb)


def _ab_out_kernel(mf_ref, mb_ref, hf_ref, hb_ref, og_ref, hg_ref, mlg_ref, hgg_ref, w_ref, x_ref, g1_ref,
                   o_ref, lhs_ref):
    for h in range(ML_H):
        hs = jnp.concatenate([mf_ref[2 * h] + mb_ref[2 * h], mf_ref[2 * h + 1] + mb_ref[2 * h + 1]], axis=1)
        r = hs * lax.rsqrt(jnp.mean(hs * hs, axis=-1, keepdims=True) + EPS)
        og = jnp.concatenate([og_ref[2 * h], og_ref[2 * h + 1]], axis=1)
        y = jax.nn.sigmoid(og) * (r * mlg_ref[:, h * ML_DV:(h + 1) * ML_DV])
        lhs_ref[:, h * ML_DV:(h + 1) * ML_DV] = y.astype(BF16)
    for h in range(HG_H):
        hs = hf_ref[h] + hb_ref[h]
        r = hs * lax.rsqrt(jnp.mean(hs * hs, axis=-1, keepdims=True) + EPS)
        y = _silu(hg_ref[h]) * (r * hgg_ref[:, h * HG_DV:(h + 1) * HG_DV])
        lhs_ref[:, ML_V + h * HG_DV:ML_V + (h + 1) * HG_DV] = y.astype(BF16)
    acc = jnp.dot(lhs_ref[...], w_ref[...], preferred_element_type=F32)
    o_ref[...] = x_ref[...] + g1_ref[...] * acc


def _ab_out(hm, ho, p3, ml_g, hg_g, w_out, x, g1, seg_fn, n, tm):
    d = x.shape[1]
    slab8 = lambda idx: pl.BlockSpec((SUBLANES, tm, LANES), lambda i: (idx, i, 0))
    return pl.pallas_call(
        _ab_out_kernel,
        grid=(n // tm,),
        in_specs=[slab8(0), slab8(0), slab8(0), slab8(0), slab8(AB_S_OG // SUBLANES), slab8(AB_S_HG // SUBLANES),
                  pl.BlockSpec((1, ML_V), lambda i: (0, 0)),
                  pl.BlockSpec((1, HG_V), lambda i: (0, 0)),
                  pl.BlockSpec((AB_OUT, d), lambda i: (0, 0)),
                  pl.BlockSpec((tm, d), lambda i: (i, 0)),
                  pl.BlockSpec((None, 1, d), lambda i: (seg_fn(i * tm), 0, 0))],
        out_specs=pl.BlockSpec((tm, d), lambda i: (i, 0)),
        out_shape=jax.ShapeDtypeStruct((n, d), F32),
        scratch_shapes=[pltpu.VMEM((tm, AB_OUT), BF16)],
        compiler_params=_cparams(("parallel",)),
        name="ab_out",
    )(hm[0], hm[1], ho[0], ho[1], p3, p3, ml_g.reshape(1, ML_V), hg_g.reshape(1, HG_V), w_out, x, g1)


def _ab_weight(w_in):
    d = w_in.shape[0]
    o = [0, 2 * ML_QK, 2 * ML_QK + ML_V, 2 * ML_QK + 2 * ML_V]
    g0 = o[3]
    h0 = g0 + 4 * ML_H
    pad = AB_SLABS * LANES - (w_in.shape[1])
    return jnp.concatenate([w_in[:, :g0], w_in[:, h0:], w_in[:, g0:h0], jnp.zeros((d, pad), w_in.dtype)],
                           axis=1).astype(BF16)


def _ab_layer(x, dims, seg_fn, tm, n_out, g_norm, sc, sh, g1, w_in, conv_w, conv_b, ig_b, fg_b, lb, ml_g, hg_g,
              w_out):
    p3 = _inproj(x, g_norm, sc, sh, _ab_weight(w_in), seg_fn, tm, 6 * LANES)
    kscale = jnp.concatenate([jnp.ones((ML_QK,), F32), jnp.full((ML_QK,), ML_DK ** -0.5, F32)])
    qk = _conv_slabs(p3, AB_S_Q, 2 * ML_H, _conv_pack(conv_w, conv_b, kscale), dims, 2 * ML_H)
    n = x.shape[0]
    gates = p3[AB_S_GATE, :, :4 * ML_H]
    grow = gates.reshape(n // CHUNK, CHUNK, 4 * ML_H).transpose(0, 2, 1)
    gbias = jnp.concatenate([ig_b.reshape(-1), fg_b.reshape(-1)])
    hm = _mlstm(qk, p3, grow, gbias.reshape(1, -1), gbias.reshape(-1, 1), dims)
    ho = _hgrn2(p3, lb.reshape(HG_H, 1, HG_DK), dims)
    return _ab_out(hm, ho, p3, ml_g, hg_g, w_out.astype(BF16), x, g1, seg_fn, n_out, min(tm, 256))


def _ssd_kernel(xf_ref, dcf_ref, drf_ref, xb_ref, dcb_ref, drb_ref, bc_ref, br_ref, ac_ref, ar_ref,
                of_ref, ob_ref, h_ref):
    @pl.when(pl.program_id(1) == 0)
    def _():
        h_ref[...] = jnp.zeros_like(h_ref)

    for d, (x_ref, dc_ref, dr_ref, o_ref) in enumerate(((xf_ref, dcf_ref, drf_ref, of_ref),
                                                        (xb_ref, dcb_ref, drb_ref, ob_ref))):
        rev = d == 1
        mask = _tri(rev)
        last = 0 if rev else CHUNK - 1
        hs = slice(d * SSD_H, (d + 1) * SSD_H)
        dtc = jax.nn.softplus(dc_ref[0][:, hs] + bc_ref[:, hs])
        lac = dtc * ac_ref[:, hs]
        dtr = jax.nn.softplus(dr_ref[hs, :] + br_ref[hs, :])
        lar = dtr * ar_ref[hs, :]
        cum_c = _dot_hi(mask.astype(F32), lac)
        cum_r = _dot_hi(lar, _tri(not rev).astype(F32))
        ecum = jnp.exp(cum_c)
        wgt = jnp.exp(cum_c[last:last + 1, :] - cum_c) * dtc
        for g in range(SSD_G):
            x = jnp.concatenate([x_ref[4 * g + j] for j in range(4)], axis=1)
            bm = x_ref[4 * SSD_G + g]
            cm = x_ref[5 * SSD_G + g]
            cb = _dot_nt(cm, bm)
            hst = h_ref[d, g]
            yoff = _dot(cm, hst)
            ys, xw, dec = [], [], []
            for r in range(SSD_R):
                hh = g * SSD_R + r
                ps = slice(r * SSD_P, (r + 1) * SSD_P)
                seg = jnp.exp(jnp.where(mask, cum_c[:, hh:hh + 1] - cum_r[hh:hh + 1, :], -jnp.inf))
                seg = seg * dtr[hh:hh + 1, :]
                ys.append(_dot(cb * seg, x[:, ps]) + ecum[:, hh:hh + 1] * yoff[:, ps])
                xw.append(x[:, ps] * wgt[:, hh:hh + 1])
                dec.append(jnp.broadcast_to(ecum[last:last + 1, hh:hh + 1], (1, SSD_P)))
            y = jnp.concatenate(ys, axis=1)
            for j in range(4):
                o_ref[4 * g + j] = y[:, j * LANES:(j + 1) * LANES]
            h_ref[d, g] = hst * jnp.concatenate(dec, axis=1) + _dot_tn(bm, jnp.concatenate(xw, axis=1))


def _ssd_scan(xbc, p3, dtrow, dt_b, neg_a, dims):
    b = dims[0]
    n = p3.shape[1]
    fwd, bwd, nch = _chunk_maps(dims)
    nxs = xbc.shape[0]

    def specs(cm):
        return [pl.BlockSpec((nxs, CHUNK, LANES), lambda bi, i: (0, cm(bi, i), 0)),
                pl.BlockSpec((1, CHUNK, LANES), lambda bi, i: (SSD_S_DT, cm(bi, i), 0)),
                pl.BlockSpec((None, 2 * SSD_H, CHUNK), lambda bi, i: (cm(bi, i), 0, 0))]

    vec = lambda shape: pl.BlockSpec(shape, lambda bi, i: (0, 0))
    out_sd = jax.ShapeDtypeStruct((D_INNER // LANES, n, LANES), F32)
    return pl.pallas_call(
        _ssd_kernel,
        grid=(b, nch),
        in_specs=specs(fwd) + specs(bwd) + [vec((1, 2 * SSD_H)), vec((2 * SSD_H, 1)),
                                            vec((1, 2 * SSD_H)), vec((2 * SSD_H, 1))],
        out_specs=[pl.BlockSpec((D_INNER // LANES, CHUNK, LANES), lambda bi, i: (0, fwd(bi, i), 0)),
                   pl.BlockSpec((D_INNER // LANES, CHUNK, LANES), lambda bi, i: (0, bwd(bi, i), 0))],
        out_shape=[out_sd, out_sd],
        scratch_shapes=[pltpu.VMEM((2, SSD_G, SSD_N, SSD_R * SSD_P), F32)],
        compiler_params=_cparams(("parallel", "arbitrary")),
        name="ssd_scan",
    )(xbc, p3, dtrow, xbc, p3, dtrow, dt_b.reshape(1, -1), dt_b.reshape(-1, 1),
      neg_a.reshape(1, -1), neg_a.reshape(-1, 1))


def _ssd_out_kernel(yf_ref, yb_ref, xs_ref, z_ref, dsk_ref, ng_ref, w_ref, x_ref, g1_ref, o_ref, lhs_ref, acc_ref):
    k = pl.program_id(1)

    @pl.when(k == 0)
    def _():
        acc_ref[...] = jnp.zeros_like(acc_ref)

    gw = D_INNER // SSD_G
    for gg in range(lhs_ref.shape[1] // gw):
        cat = lambda ref: jnp.concatenate([ref[4 * gg + j] for j in range(4)], axis=1)
        cs = slice(gg * gw, (gg + 1) * gw)
        y = cat(yf_ref) + cat(yb_ref) + dsk_ref[:, cs] * cat(xs_ref)
        u = y * _silu(cat(z_ref))
        u = u * lax.rsqrt(jnp.mean(u * u, axis=-1, keepdims=True) + EPS) * ng_ref[:, cs]
        lhs_ref[:, cs] = u.astype(BF16)
    acc_ref[...] += jnp.dot(lhs_ref[...], w_ref[...], preferred_element_type=F32)

    @pl.when(k == pl.num_programs(1) - 1)
    def _():
        o_ref[...] = x_ref[...] + g1_ref[...] * acc_ref[...]


def _ssd_out(yf, yb, xbc, p3, dskip, norm_g, w_out, x, g1, seg_fn, n_rows, tm):
    d = x.shape[1]
    tk = 1024
    nsl = tk // LANES
    slab = lambda: pl.BlockSpec((nsl, tm, LANES), lambda i, k: (k, i, 0))
    return pl.pallas_call(
        _ssd_out_kernel,
        grid=(n_rows // tm, D_INNER // tk),
        in_specs=[slab(), slab(), slab(), slab(),
                  pl.BlockSpec((1, tk), lambda i, k: (0, k)),
                  pl.BlockSpec((1, tk), lambda i, k: (0, k)),
                  pl.BlockSpec((tk, d), lambda i, k: (k, 0)),
                  pl.BlockSpec((tm, d), lambda i, k: (i, 0)),
                  pl.BlockSpec((None, 1, d), lambda i, k: (seg_fn(i * tm), 0, 0))],
        out_specs=pl.BlockSpec((tm, d), lambda i, k: (i, 0)),
        out_shape=jax.ShapeDtypeStruct((n_rows, d), F32),
        scratch_shapes=[pltpu.VMEM((tm, tk), BF16), pltpu.VMEM((tm, d), F32)],
        compiler_params=_cparams(("parallel", "arbitrary")),
        name="ssd_out",
    )(yf, yb, xbc, p3, dskip, norm_g.reshape(1, -1), w_out, x, g1)


def _ssd_layer(x, dims, seg_fn, tm, n_out, g_norm, sc, sh, g1, w_in, conv_w, conv_b, dt_b, a_log, d_skip,
               norm_g, w_out):
    d = x.shape[1]
    n = x.shape[0]
    p3 = _inproj(x, g_norm, sc, sh, w_in.astype(BF16), seg_fn, tm, 9 * LANES)
    nconv = conv_w.shape[1]
    xbc = _conv_slabs(p3, SSD_S_X, nconv // LANES, _conv_pack(conv_w, conv_b, jnp.ones((nconv,), F32)), dims, 16)
    dt = p3[SSD_S_DT]
    dtrow = dt.reshape(n // CHUNK, CHUNK, 2 * SSD_H).transpose(0, 2, 1)
    yf, yb = _ssd_scan(xbc, p3, dtrow, dt_b.reshape(-1), -jnp.exp(a_log.astype(F32)).reshape(-1), dims)
    dskip = jnp.repeat(d_skip.astype(F32), SSD_P).reshape(1, D_INNER)
    return _ssd_out(yf, yb, xbc, p3, dskip, norm_g, w_out.astype(BF16), x, g1, seg_fn, n_out, min(tm, 256))


def _regroup_kernel(x_ref, o_ref):
    a = x_ref.shape[0]
    for cc in range(SUBLANES):
        o_ref[cc * a:(cc + 1) * a, :] = x_ref[:, cc, :]


def _regroup(x, b, a, c, n_out):
    d = x.shape[1]
    cblk = c // SUBLANES
    return pl.pallas_call(
        _regroup_kernel,
        grid=(b, cblk),
        in_specs=[pl.BlockSpec((a, SUBLANES, d), lambda bi, j: (bi, j, 0))],
        out_specs=pl.BlockSpec((a * SUBLANES, d), lambda bi, j: (bi * cblk + j, 0)),
        out_shape=jax.ShapeDtypeStruct((n_out, d), F32),
        compiler_params=_cparams(("parallel", "parallel")),
        name="regroup",
    )(x.reshape(x.shape[0] // c, c, d))


MOE_TILE = 512


def _router_kernel(x_ref, g_ref, sc_ref, sh_ref, wt_ref, rb_ref, hn_ref, ii_ref, iw_ref, cnt_ref, carry_ref):
    @pl.when(pl.program_id(0) == 0)
    def _():
        carry_ref[...] = jnp.zeros_like(carry_ref)

    x = x_ref[...]
    tm = x.shape[0]
    y = x * lax.rsqrt(jnp.mean(x * x, axis=-1, keepdims=True) + EPS) * g_ref[...]
    hn = y * (1.0 + sc_ref[...]) + sh_ref[...]
    _store_token_rows(hn_ref, hn)
    logits = lax.dot_general(wt_ref[...], hn, (((1,), (1,)), ((), ())), precision=HI,
                             preferred_element_type=F32)
    score = jax.nn.sigmoid(logits)
    biased = score + rb_ref[...]
    rb = [biased[e:e + 1, :] for e in range(N_EXPERTS)]
    rs = [score[e:e + 1, :] for e in range(N_EXPERTS)]
    gsc = []
    for g in range(N_GROUPS):
        a, b, c, d = rb[EXPERTS_PER_GROUP * g:EXPERTS_PER_GROUP * (g + 1)]
        hi1, lo1, hi2, lo2 = jnp.maximum(a, b), jnp.minimum(a, b), jnp.maximum(c, d), jnp.minimum(c, d)
        gsc.append(jnp.maximum(hi1, hi2) + jnp.maximum(jnp.minimum(hi1, hi2), jnp.maximum(lo1, lo2)))
    best = jnp.zeros((1, tm), jnp.int32)
    bsc = gsc[0]
    for g in range(1, N_GROUPS):
        upd = gsc[g] > bsc
        best = jnp.where(upd, g, best)
        bsc = jnp.where(upd, gsc[g], bsc)

    def pick(rows, p):
        out = rows[p]
        for g in range(1, N_GROUPS):
            out = jnp.where(best == g, rows[EXPERTS_PER_GROUP * g + p], out)
        return out

    vals = [pick(rb, p) for p in range(EXPERTS_PER_GROUP)]
    scs = [pick(rs, p) for p in range(EXPERTS_PER_GROUP)]
    p1, v1, s1 = jnp.zeros((1, tm), jnp.int32), vals[0], scs[0]
    for p in range(1, EXPERTS_PER_GROUP):
        upd = vals[p] > v1
        p1, v1, s1 = jnp.where(upd, p, p1), jnp.where(upd, vals[p], v1), jnp.where(upd, scs[p], s1)
    p2 = jnp.zeros((1, tm), jnp.int32)
    v2 = jnp.full((1, tm), -jnp.inf, F32)
    s2 = jnp.zeros((1, tm), F32)
    for p in range(EXPERTS_PER_GROUP):
        upd = (p1 != p) & (vals[p] > v2)
        p2, v2, s2 = jnp.where(upd, p, p2), jnp.where(upd, vals[p], v2), jnp.where(upd, scs[p], s2)
    e1 = best * EXPERTS_PER_GROUP + p1
    e2 = best * EXPERTS_PER_GROUP + p2
    tot = s1 + s2
    eiota = lax.broadcasted_iota(jnp.int32, (N_EXPERTS, tm), 0)
    oh1 = (eiota == e1).astype(F32)
    oh2 = (eiota == e2).astype(F32)
    oh = oh1 + oh2
    before = (lax.broadcasted_iota(jnp.int32, (tm, tm), 0) < lax.broadcasted_iota(jnp.int32, (tm, tm), 1))
    cnt = _dot(oh, before.astype(BF16)) + carry_ref[:, :1]
    r1 = jnp.sum(oh1 * cnt, axis=0, keepdims=True).astype(jnp.int32)
    r2 = jnp.sum(oh2 * cnt, axis=0, keepdims=True).astype(jnp.int32)
    zi = jnp.zeros((SUBLANES - 4, tm), jnp.int32)
    ii_ref[...] = jnp.concatenate([e1, e2, r1, r2, zi], axis=0)
    iw_ref[...] = jnp.concatenate([s1 / tot, s2 / tot, jnp.zeros((SUBLANES - 2, tm), F32)], axis=0)
    carry = carry_ref[...] + jnp.sum(oh, axis=1, keepdims=True)
    carry_ref[...] = carry
    cnt_ref[...] = carry


def _router(x, n_rows, g, sc, sh, seg_fn, router_wt, router_b):
    d = x.shape[1]
    tm = MOE_TILE
    return pl.pallas_call(
        _router_kernel,
        grid=(n_rows // tm,),
        in_specs=[pl.BlockSpec((tm, d), lambda i: (i, 0)),
                  pl.BlockSpec((1, d), lambda i: (0, 0)),
                  pl.BlockSpec((None, 1, d), lambda i: (seg_fn(i * tm), 0, 0)),
                  pl.BlockSpec((None, 1, d), lambda i: (seg_fn(i * tm), 0, 0)),
                  pl.BlockSpec((N_EXPERTS, d), lambda i: (0, 0)),
                  pl.BlockSpec((N_EXPERTS, 1), lambda i: (0, 0))],
        out_specs=[pl.BlockSpec((tm, d // LANES, LANES), lambda i: (i, 0, 0)),
                   pl.BlockSpec((SUBLANES, tm), lambda i: (0, i)),
                   pl.BlockSpec((SUBLANES, tm), lambda i: (0, i)),
                   pl.BlockSpec((N_EXPERTS, LANES), lambda i: (0, 0))],
        out_shape=[jax.ShapeDtypeStruct((n_rows, d // LANES, LANES), F32),
                   jax.ShapeDtypeStruct((SUBLANES, n_rows), jnp.int32),
                   jax.ShapeDtypeStruct((SUBLANES, n_rows), F32),
                   jax.ShapeDtypeStruct((N_EXPERTS, LANES), F32)],
        scratch_shapes=[pltpu.VMEM((N_EXPERTS, LANES), F32)],
        compiler_params=_cparams(("arbitrary",)),
        name="router",
    )(x, g.reshape(1, d), sc, sh, router_wt, router_b.reshape(N_EXPERTS, 1))


TOK_SUB = D_MODEL // LANES


def _store_token_rows(ref, val):
    for s in range(TOK_SUB):
        ref[:, s, :] = val[:, s * LANES:(s + 1) * LANES]


def _row_copy(src_ref, o_ref, sem, src_row, dst_row):
    return pltpu.make_async_copy(src_ref.at[pl.ds(src_row, 1)], o_ref.at[pl.ds(dst_row, 1)], sem)


def _gather_kernel(idx_ref, src_ref, o_ref, sem, *, rows):
    base = pl.program_id(0) * rows

    def issue(r8, carry):
        for u in range(SUBLANES):
            r = r8 * SUBLANES + u
            _row_copy(src_ref, o_ref, sem, idx_ref[r], base + r).start()
        return carry

    def drain(r, carry):
        _row_copy(src_ref, o_ref, sem, 0, base + r).wait()
        return carry

    lax.fori_loop(0, rows // SUBLANES, issue, 0)
    lax.fori_loop(0, rows, drain, 0)


def _gather_rows(idx, src, n_out):
    rows = MOE_TILE
    return pl.pallas_call(
        functools.partial(_gather_kernel, rows=rows),
        grid=(n_out // rows,),
        in_specs=[pl.BlockSpec((rows,), lambda i: (i,), memory_space=pltpu.SMEM),
                  pl.BlockSpec(memory_space=pl.ANY)],
        out_specs=pl.BlockSpec(memory_space=pl.ANY),
        out_shape=jax.ShapeDtypeStruct((n_out,) + src.shape[1:], src.dtype),
        scratch_shapes=[pltpu.SemaphoreType.DMA],
        compiler_params=_cparams(("arbitrary",)),
        name="gather_rows",
    )(idx, src)


def _expert_kernel(te_ref, nu_ref, x_ref, w1_ref, w3_ref, w2_ref, o_ref, xb_ref, acc_ref):
    i, f = pl.program_id(0), pl.program_id(1)
    nf = pl.num_programs(1)
    used = i < nu_ref[0]

    @pl.when(used & (f == 0))
    def _():
        for s in range(TOK_SUB):
            xb_ref[:, s * LANES:(s + 1) * LANES] = x_ref[:, s, :].astype(BF16)
        acc_ref[...] = jnp.zeros_like(acc_ref)

    @pl.when(used)
    def _():
        xb = xb_ref[...]
        a = _silu(_dot(xb, w1_ref[...])) * _dot(xb, w3_ref[...])
        acc_ref[...] += _dot(a, w2_ref[...])

    @pl.when(f == nf - 1)
    def _():
        _store_token_rows(o_ref, jnp.where(used, acc_ref[...], 0.0))


def _experts(tile_e, n_used, xs, w1, w3, w2):
    p = xs.shape[0]
    d = D_MODEL
    te, tf = MOE_TILE, 256
    nf = D_EXPERT // tf
    fidx = lambda i, f, nu: jnp.where(i < nu[0], f, nf - 1)
    grid_spec = pltpu.PrefetchScalarGridSpec(
        num_scalar_prefetch=2,
        grid=(p // te, nf),
        in_specs=[pl.BlockSpec((te, TOK_SUB, LANES), lambda i, f, te_r, nu: (i, 0, 0)),
                  pl.BlockSpec((None, d, tf), lambda i, f, te_r, nu: (te_r[i], 0, fidx(i, f, nu))),
                  pl.BlockSpec((None, d, tf), lambda i, f, te_r, nu: (te_r[i], 0, fidx(i, f, nu))),
                  pl.BlockSpec((None, tf, d), lambda i, f, te_r, nu: (te_r[i], fidx(i, f, nu), 0))],
        out_specs=pl.BlockSpec((te, TOK_SUB, LANES), lambda i, f, te_r, nu: (i, 0, 0)),
        scratch_shapes=[pltpu.VMEM((te, d), BF16), pltpu.VMEM((te, d), F32)])
    return pl.pallas_call(
        _expert_kernel,
        grid_spec=grid_spec,
        out_shape=jax.ShapeDtypeStruct((p, TOK_SUB, LANES), F32),
        compiler_params=_cparams(("arbitrary", "arbitrary")),
        name="experts",
    )(tile_e, n_used, xs, w1, w3, w2)


def _combine_kernel(x_ref, y1_ref, y2_ref, w_ref, g2_ref, fg_ref, o_ref, *, final):
    w = w_ref[...]
    for s in range(TOK_SUB):
        cs = slice(s * LANES, (s + 1) * LANES)
        o_ref[:, cs] = x_ref[:, cs] + g2_ref[:, cs] * (w[:, 0:1] * y1_ref[:, s, :] + w[:, 1:2] * y2_ref[:, s, :])
    if final:
        out = o_ref[...]
        o_ref[...] = out * lax.rsqrt(jnp.mean(out * out, axis=-1, keepdims=True) + EPS) * fg_ref[...]


def _combine(x, n_rows, yg, wts, g2, seg_fn, final_g):
    d = x.shape[1]
    tm = 256
    nt = n_rows // tm
    fg = jnp.ones((1, d), F32) if final_g is None else final_g.reshape(1, d)
    return pl.pallas_call(
        functools.partial(_combine_kernel, final=final_g is not None),
        grid=(nt,),
        in_specs=[pl.BlockSpec((tm, d), lambda i: (i, 0)),
                  pl.BlockSpec((tm, TOK_SUB, LANES), lambda i: (i, 0, 0)),
                  pl.BlockSpec((tm, TOK_SUB, LANES), lambda i: (nt + i, 0, 0)),
                  pl.BlockSpec((tm, 2), lambda i: (i, 0)),
                  pl.BlockSpec((None, 1, d), lambda i: (seg_fn(i * tm), 0, 0)),
                  pl.BlockSpec((1, d), lambda i: (0, 0))],
        out_specs=pl.BlockSpec((tm, d), lambda i: (i, 0)),
        out_shape=jax.ShapeDtypeStruct((n_rows, d), F32),
        compiler_params=_cparams(("parallel",)),
        name="combine",
    )(x, yg, yg, wts, g2, fg)


def _moe(x, n_rows, seg_fn, g, sc, sh, g2, router_wt, router_b, w1, w3, w2, final_g=None):
    te = MOE_TILE
    hn, ii, iw, cnt = _router(x, n_rows, g, sc, sh, seg_fn, router_wt, router_b)
    counts = cnt[:, 0].astype(jnp.int32)
    padded = (counts + te - 1) // te * te
    ends = jnp.cumsum(padded)
    base = ends - padded
    pos1 = base[ii[0]] + ii[2]
    pos2 = base[ii[1]] + ii[3]
    p = 2 * n_rows + N_EXPERTS * te
    ntile = p // te
    n_used = ends[-1] // te
    tiles = jnp.arange(ntile, dtype=jnp.int32)
    tile_e = jnp.minimum(jnp.searchsorted(ends, tiles * te, side='right'), N_EXPERTS - 1).astype(jnp.int32)
    tile_e = jnp.where(tiles < n_used, tile_e, tile_e[jnp.maximum(n_used - 1, 0)])
    tok = jnp.arange(n_rows, dtype=jnp.int32)
    src = jnp.zeros((p,), jnp.int32).at[pos1].set(tok).at[pos2].set(tok)
    xs = _gather_rows(src, hn, p)
    ys = _experts(tile_e, n_used.reshape(1).astype(jnp.int32), xs, w1, w3, w2)
    yg = _gather_rows(jnp.concatenate([pos1, pos2]), ys, 2 * n_rows)
    return _combine(x, n_rows, yg, iw[:2].T, g2, seg_fn, final_g)


def kernel(x, c, ctx, c_ctx, ada_w, ada_b, norm1_g, norm2_g, ab_w_in, ab_conv_w, ab_conv_b, ml_ig_b, ml_fg_b,
           hg_lb, ml_norm_g, hg_norm_g, ab_w_out, ssd_w_in, ssd_conv_w, ssd_conv_b, ssd_dt_b, ssd_a_log, ssd_d,
           ssd_norm_g, ssd_w_out, router_w, router_b, moe_w1, moe_w3, moe_w2, final_g):
    b, seq, d = x.shape
    ctxl = ctx.shape[1]
    depth = ada_w.shape[0]
    dims = (b, seq, ctxl)
    nl, nc = b * seq, b * ctxl
    rows = seq // GRID_W
    tm = min(512, nc)
    seg_fn = lambda row: jnp.where(row < nl, row // seq, b)
    c8 = jnp.concatenate([c, c_ctx[None], jnp.zeros((SUBLANES - b - 1, d), F32)])
    mods = _modulation(c8, ada_w, ada_b)[:, :b + 1].reshape(depth, b + 1, 6, 1, d)
    lb_all = jnp.cumsum(jax.nn.softmax(hg_lb.astype(F32), axis=0), axis=0)
    router_wt = router_w.T
    xr = jnp.concatenate([x.reshape(nl, d), ctx.reshape(nc, d)])
    transposed = False
    for l in range(depth):
        sh1, sc1, g1, sh2, sc2, g2 = (mods[l][:, k] for k in range(6))
        keep_ctx = l < depth - 1
        n_out = nl + nc if keep_ctx else nl
        j = l // 2
        if (l % 2 == 1) != transposed:
            xt = _regroup(xr, b, GRID_W if transposed else rows, rows if transposed else GRID_W, nl)
            xr = jnp.concatenate([xt, xr[nl:]])
            transposed = not transposed
        if l % 2 == 0:
            xr = _ab_layer(xr, dims, seg_fn, tm, n_out, norm1_g[l], sc1, sh1, g1, ab_w_in[j], ab_conv_w[j],
                           ab_conv_b[j], ml_ig_b[j], ml_fg_b[j], lb_all[l], ml_norm_g[j], hg_norm_g[j], ab_w_out[j])
        else:
            xr = _ssd_layer(xr, dims, seg_fn, tm, n_out, norm1_g[l], sc1, sh1, g1, ssd_w_in[j], ssd_conv_w[j],
                            ssd_conv_b[j], ssd_dt_b[j], ssd_a_log[j], ssd_d[j], ssd_norm_g[j], ssd_w_out[j])
        xr = _moe(xr, n_out, seg_fn, norm2_g[l], sc2, sh2, g2, router_wt, router_b, moe_w1[l], moe_w3[l],
                  moe_w2[l], final_g if l == depth - 1 else None)
    if transposed:
        xr = _regroup(xr, b, GRID_W, rows, nl)
    return xr[:nl].reshape(b, seq, d)
```

```python
import functools
import math

import jax
import jax.numpy as jnp
from jax import lax
from jax.experimental import pallas as pl
from jax.experimental.pallas import tpu as pltpu

F32 = jnp.float32
BF16 = jnp.bfloat16
HI = lax.Precision.HIGHEST

D_MODEL = 2048
GRID_W = 64
EPS = 1e-6
CHUNK = 64
CONV_K = 5
ML_H, ML_DK, ML_DV = 4, 128, 256
HG_H, HG_DK, HG_DV = 8, 128, 128
ML_QK, ML_V = ML_H * ML_DK, ML_H * ML_DV
HG_K, HG_V = HG_H * HG_DK, HG_H * HG_DV
AB_OUT = ML_V + HG_V
D_INNER = 2 * D_MODEL
SSD_P, SSD_G, SSD_N = 64, 8, 128
SSD_H = D_INNER // SSD_P
SSD_R = SSD_H // SSD_G
N_EXPERTS, N_GROUPS, TOP_K, D_EXPERT = 16, 4, 2, 1024
EXPERTS_PER_GROUP = N_EXPERTS // N_GROUPS

LANES = 128
SUBLANES = 8
VMEM_LIMIT = 56 * 1024 * 1024

AB_S_Q, AB_S_K, AB_S_V, AB_S_OG, AB_S_HQ, AB_S_HF, AB_S_HI, AB_S_HG, AB_S_GATE = 0, 4, 8, 16, 24, 32, 48, 56, 64
AB_SLABS = 66
SSD_S_Z, SSD_S_X, SSD_S_B, SSD_S_C, SSD_S_DT = 0, 32, 64, 72, 80
SSD_SLABS = 81


def _cparams(sem):
    return pltpu.CompilerParams(dimension_semantics=sem, vmem_limit_bytes=VMEM_LIMIT)


def _silu(x):
    return x * jax.nn.sigmoid(x)


def _dot(a, b):
    return jnp.dot(a.astype(BF16), b.astype(BF16), preferred_element_type=F32)


def _dot_nt(a, b):
    return lax.dot_general(a.astype(BF16), b.astype(BF16), (((1,), (1,)), ((), ())),
                           preferred_element_type=F32)


def _dot_tn(a, b):
    return lax.dot_general(a.astype(BF16), b.astype(BF16), (((0,), (0,)), ((), ())),
                           preferred_element_type=F32)


def _dot_hi(a, b):
    return jnp.dot(a, b, precision=HI, preferred_element_type=F32)


def _tri(rev):
    t = lax.broadcasted_iota(jnp.int32, (CHUNK, CHUNK), 0)
    s = lax.broadcasted_iota(jnp.int32, (CHUNK, CHUNK), 1)
    return (s >= t) if rev else (s <= t)


def _mod_kernel(c_ref, w_ref, b_ref, o_ref):
    c = c_ref[...]
    o_ref[...] = _dot(_silu(c), w_ref[...]) + b_ref[...]


def _modulation(c8, ada_w, ada_b):
    depth, d, d6 = ada_w.shape
    tn = 1024
    return pl.pallas_call(
        _mod_kernel,
        grid=(depth, d6 // tn),
        in_specs=[pl.BlockSpec((SUBLANES, d), lambda l, j: (0, 0)),
                  pl.BlockSpec((None, d, tn), lambda l, j: (l, 0, j)),
                  pl.BlockSpec((None, 1, tn), lambda l, j: (l, 0, j))],
        out_specs=pl.BlockSpec((None, SUBLANES, tn), lambda l, j: (l, 0, j)),
        out_shape=jax.ShapeDtypeStruct((depth, SUBLANES, d6), F32),
        compiler_params=_cparams(("parallel", "parallel")),
        name="modulation",
    )(c8, ada_w, ada_b.reshape(depth, 1, d6))


def _inproj_kernel(x_ref, g_ref, sc_ref, sh_ref, w_ref, o_ref, h_ref):
    @pl.when(pl.program_id(1) == 0)
    def _():
        x = x_ref[...]
        y = x * lax.rsqrt(jnp.mean(x * x, axis=-1, keepdims=True) + EPS) * g_ref[...]
        h_ref[...] = (y * (1.0 + sc_ref[...]) + sh_ref[...]).astype(BF16)

    acc = jnp.dot(h_ref[...], w_ref[...], preferred_element_type=F32)
    for s in range(o_ref.shape[0]):
        o_ref[s] = acc[:, s * LANES:(s + 1) * LANES]


def _inproj(x, g, sc, sh, w, seg_fn, tm, tn):
    n, d = x.shape
    ncols = w.shape[1]
    nsl = tn // LANES
    return pl.pallas_call(
        _inproj_kernel,
        grid=(n // tm, ncols // tn),
        in_specs=[pl.BlockSpec((tm, d), lambda i, j: (i, 0)),
                  pl.BlockSpec((1, d), lambda i, j: (0, 0)),
                  pl.BlockSpec((None, 1, d), lambda i, j: (seg_fn(i * tm), 0, 0)),
                  pl.BlockSpec((None, 1, d), lambda i, j: (seg_fn(i * tm), 0, 0)),
                  pl.BlockSpec((d, tn), lambda i, j: (0, j))],
        out_specs=pl.BlockSpec((nsl, tm, LANES), lambda i, j: (j, i, 0)),
        out_shape=jax.ShapeDtypeStruct((ncols // LANES, n, LANES), F32),
        scratch_shapes=[pltpu.VMEM((tm, d), BF16)],
        compiler_params=_cparams(("parallel", "arbitrary")),
        name="inproj",
    )(x, g.reshape(1, d), sc, sh, w)


def _conv_kernel(prev_ref, cur_ref, next_ref, w_ref, o_ref, *, tt, tiles_lat, n_lat_tiles, tiles_ctx):
    i = pl.program_id(1)
    in_lat = i < n_lat_tiles
    pos = jnp.where(in_lat, i % tiles_lat, (i - n_lat_tiles) % tiles_ctx)
    last = jnp.where(in_lat, tiles_lat - 1, tiles_ctx - 1)
    keep_prev = (pos != 0).astype(F32)
    keep_next = (pos != last).astype(F32)
    ext = jnp.concatenate([prev_ref[...] * keep_prev, cur_ref[...], next_ref[...] * keep_next], axis=1)
    w = w_ref[...]
    acc = jnp.zeros(cur_ref.shape, F32) + w[:, CONV_K:CONV_K + 1, :]
    for k in range(CONV_K):
        shift = (CONV_K // 2 - k) % (tt + 2 * SUBLANES)
        r = ext if shift == 0 else pltpu.roll(ext, shift, 1)
        acc = acc + r[:, SUBLANES:SUBLANES + tt, :] * w[:, k:k + 1, :]
    o_ref[...] = _silu(acc) * w[:, CONV_K + 1:CONV_K + 2, :]


def _conv_slabs(p3, slab0, nslab, wpack, dims, sb):
    b, seq, ctxl = dims
    n = p3.shape[1]
    tt = min(256, ctxl)
    t8 = tt // SUBLANES
    nblk8 = n // SUBLANES
    s0 = slab0 // sb
    kern = functools.partial(_conv_kernel, tt=tt, tiles_lat=seq // tt, n_lat_tiles=b * seq // tt,
                             tiles_ctx=ctxl // tt)
    return pl.pallas_call(
        kern,
        grid=(nslab // sb, n // tt),
        in_specs=[pl.BlockSpec((sb, SUBLANES, LANES), lambda s, i: (s0 + s, jnp.maximum(i * t8 - 1, 0), 0)),
                  pl.BlockSpec((sb, tt, LANES), lambda s, i: (s0 + s, i, 0)),
                  pl.BlockSpec((sb, SUBLANES, LANES),
                               lambda s, i: (s0 + s, jnp.minimum((i + 1) * t8, nblk8 - 1), 0)),
                  pl.BlockSpec((sb, SUBLANES, LANES), lambda s, i: (s, 0, 0))],
        out_specs=pl.BlockSpec((sb, tt, LANES), lambda s, i: (s, i, 0)),
        out_shape=jax.ShapeDtypeStruct((nslab, n, LANES), F32),
        compiler_params=_cparams(("parallel", "parallel")),
        name="conv",
    )(p3, p3, p3, wpack)


def _conv_pack(conv_w, conv_b, scale):
    c = conv_w.shape[1]
    rows = jnp.concatenate([conv_w, conv_b[None], scale[None], jnp.zeros((1, c), F32)], axis=0)
    return rows.reshape(SUBLANES, c // LANES, LANES).transpose(1, 0, 2)


def _chunk_maps(dims):
    b, seq, ctxl = dims
    ncc, nlc = ctxl // CHUNK, seq // CHUNK

    def fwd(bi, i):
        return jnp.where(i < ncc, b * nlc + bi * ncc + i, bi * nlc + (i - ncc))

    def bwd(bi, i):
        return jnp.where(i < ncc, b * nlc + bi * ncc + (ncc - 1 - i), bi * nlc + (nlc - 1 - (i - ncc)))

    return fwd, bwd, ncc + nlc


def _mlstm_kernel(qkf_ref, vf_ref, gcf_ref, grf_ref, qkb_ref, vb_ref, gcb_ref, grb_ref, bc_ref, br_ref,
                  of_ref, ob_ref, c_ref, n_ref, m_ref):
    @pl.when(pl.program_id(1) == 0)
    def _():
        c_ref[...] = jnp.zeros_like(c_ref)
        n_ref[...] = jnp.zeros_like(n_ref)
        m_ref[...] = jnp.zeros_like(m_ref)

    ng = 2 * ML_H
    for d, (qk_ref, v_ref, gc_ref, gr_ref, o_ref) in enumerate(
            ((qkf_ref, vf_ref, gcf_ref, grf_ref, of_ref), (qkb_ref, vb_ref, gcb_ref, grb_ref, ob_ref))):
        rev = d == 1
        mask = _tri(rev)
        gcol = gc_ref[0][:, :2 * ng] + bc_ref[...]
        grow = gr_ref[...] + br_ref[...]
        ic_all = gcol[:, d * ML_H:(d + 1) * ML_H]
        lfc_all = jax.nn.log_sigmoid(gcol[:, ng + d * ML_H:ng + (d + 1) * ML_H])
        ir_all = grow[d * ML_H:(d + 1) * ML_H, :]
        lfr_all = jax.nn.log_sigmoid(grow[ng + d * ML_H:ng + (d + 1) * ML_H, :])
        incl = mask.astype(F32)
        bcol_all = _dot_hi(incl, lfc_all)
        brow_all = _dot_hi(lfr_all, _tri(not rev).astype(F32))
        last = 0 if rev else CHUNK - 1
        for h in range(ML_H):
            q = qk_ref[h]
            k = qk_ref[ML_H + h]
            v = jnp.concatenate([v_ref[2 * h], v_ref[2 * h + 1]], axis=1)
            bcol, icol = bcol_all[:, h:h + 1], ic_all[:, h:h + 1]
            brow, irow = brow_all[h:h + 1, :], ir_all[h:h + 1, :]
            cst = c_ref[d, h]
            nst = n_ref[d, h]
            mprev = m_ref[d, h][:, :1]
            logd = jnp.where(mask, bcol - brow + irow, -jnp.inf)
            inter = bcol + mprev
            m_t = jnp.maximum(inter, jnp.max(logd, axis=1, keepdims=True))
            s = _dot_nt(q, k) * jnp.exp(logd - m_t)
            sc = jnp.exp(inter - m_t)
            num = _dot(s, v) + sc * _dot_nt(q, cst)
            den = jnp.sum(s, axis=1, keepdims=True) + sc * jnp.sum(q * nst, axis=1, keepdims=True)
            hout = num / jnp.maximum(jnp.abs(den), jnp.exp(-m_t))
            o_ref[2 * h] = hout[:, :LANES]
            o_ref[2 * h + 1] = hout[:, LANES:]
            b_last = bcol[last:last + 1, :]
            wlog = b_last - bcol + icol
            m_new = jnp.maximum(b_last + mprev, jnp.max(wlog, axis=0, keepdims=True))
            w = jnp.exp(wlog - m_new)
            dec = jnp.exp(b_last + mprev - m_new)
            c_ref[d, h] = dec * cst + _dot_tn(w * v, k)
            n_ref[d, h] = dec * nst + jnp.sum(w * k, axis=0, keepdims=True)
            m_ref[d, h] = jnp.broadcast_to(m_new, (1, LANES))


def _mlstm(qk, p3, grow, bcol, brow, dims):
    b = dims[0]
    n = p3.shape[1]
    fwd, bwd, nch = _chunk_maps(dims)

    def specs(cm):
        return [pl.BlockSpec((2 * ML_H, CHUNK, LANES), lambda bi, i: (0, cm(bi, i), 0)),
                pl.BlockSpec((2 * ML_H, CHUNK, LANES), lambda bi, i: (AB_S_V // (2 * ML_H), cm(bi, i), 0)),
                pl.BlockSpec((1, CHUNK, LANES), lambda bi, i: (AB_S_GATE, cm(bi, i), 0)),
                pl.BlockSpec((None, 4 * ML_H, CHUNK), lambda bi, i: (cm(bi, i), 0, 0))]

    out_sd = jax.ShapeDtypeStruct((2 * ML_H, n, LANES), F32)
    return pl.pallas_call(
        _mlstm_kernel,
        grid=(b, nch),
        in_specs=specs(fwd) + specs(bwd) + [pl.BlockSpec((1, 4 * ML_H), lambda bi, i: (0, 0)),
                                            pl.BlockSpec((4 * ML_H, 1), lambda bi, i: (0, 0))],
        out_specs=[pl.BlockSpec((2 * ML_H, CHUNK, LANES), lambda bi, i: (0, fwd(bi, i), 0)),
                   pl.BlockSpec((2 * ML_H, CHUNK, LANES), lambda bi, i: (0, bwd(bi, i), 0))],
        out_shape=[out_sd, out_sd],
        scratch_shapes=[pltpu.VMEM((2, ML_H, ML_DV, ML_DK), F32),
                        pltpu.VMEM((2, ML_H, 1, ML_DK), F32),
                        pltpu.VMEM((2, ML_H, 1, LANES), F32)],
        compiler_params=_cparams(("parallel", "arbitrary")),
        name="mlstm",
    )(qk, p3, p3, grow, qk, p3, p3, grow, bcol, brow)


def _bcast_rows(a, rows, span):
    parts = [jnp.broadcast_to(a[r:r + 1, :], (span, a.shape[1])) for r in rows]
    return parts[0] if len(parts) == 1 else jnp.concatenate(parts, axis=0)


def _hgrn2_kernel(qf_ref, ff_ref, vf_ref, qb_ref, fb_ref, vb_ref, lb_ref, of_ref, ob_ref, s_ref):
    @pl.when(pl.program_id(1) == 0)
    def _():
        s_ref[...] = jnp.zeros_like(s_ref)

    t = lax.broadcasted_iota(jnp.int32, (CHUNK, CHUNK), 0)
    s = lax.broadcasted_iota(jnp.int32, (CHUNK, CHUNK), 1)
    for d, (q_ref, f_ref, v_ref, o_ref) in enumerate(((qf_ref, ff_ref, vf_ref, of_ref),
                                                      (qb_ref, fb_ref, vb_ref, ob_ref))):
        rev = d == 1
        mask = _tri(rev)
        incl = mask.astype(F32)
        last = 0 if rev else CHUNK - 1
        for h in range(HG_H):
            q = q_ref[h]
            lb = lb_ref[h]
            f = lb + (1.0 - lb) * jax.nn.sigmoid(f_ref[h])
            k = 1.0 - f
            lg = jnp.log(f)
            v = v_ref[h]
            a = _dot_hi(incl, lg)
            scores = jnp.zeros((CHUNK, CHUNK), F32)
            for m in (32, 16, 8):
                nb = CHUNK // (2 * m)
                ref_rows = [bi * 2 * m + (m if rev else m - 1) for bi in range(nb)]
                aref = _bcast_rows(a, ref_rows, 2 * m)
                eq = jnp.exp(jnp.minimum(a - aref, 0.0))
                ek = jnp.exp(jnp.minimum(aref - a, 0.0))
                same = (t // (2 * m)) == (s // (2 * m))
                t_late = ((t // m) % 2 == 0) if rev else ((t // m) % 2 == 1)
                s_early = ((s // m) % 2 == 1) if rev else ((s // m) % 2 == 0)
                lvl = same & t_late & s_early
                scores = scores + jnp.where(lvl, _dot_nt(q * eq, k * ek), 0.0)
            aex = a - lg
            ref_rows = [bi * SUBLANES + (SUBLANES - 1 if rev else 0) for bi in range(CHUNK // SUBLANES)]
            aref = _bcast_rows(aex, ref_rows, SUBLANES)
            diag = ((t // SUBLANES) == (s // SUBLANES)) & mask
            scores = scores + jnp.where(diag, _dot_nt(q * jnp.exp(a - aref), k * jnp.exp(aref - a)), 0.0)
            st = s_ref[d, h]
            o_ref[h] = _dot(scores, v) + _dot_nt(q * jnp.exp(a), st)
            a_last = a[last:last + 1, :]
            s_ref[d, h] = st * jnp.exp(a_last) + _dot_tn(v, k * jnp.exp(a_last - a))


def _hgrn2(p3, lb, dims):
    b = dims[0]
    n = p3.shape[1]
    fwd, bwd, nch = _chunk_maps(dims)

    def specs(cm, d):
        return [pl.BlockSpec((HG_H, CHUNK, LANES), lambda bi, i: (AB_S_HQ // HG_H, cm(bi, i), 0)),
                pl.BlockSpec((HG_H, CHUNK, LANES), lambda bi, i: (AB_S_HF // HG_H + d, cm(bi, i), 0)),
                pl.BlockSpec((HG_H, CHUNK, LANES), lambda bi, i: (AB_S_HI // HG_H, cm(bi, i), 0))]

    out_sd = jax.ShapeDtypeStruct((HG_H, n, LANES), F32)
    return pl.pallas_call(
        _hgrn2_kernel,
        grid=(b, nch),
        in_specs=specs(fwd, 0) + specs(bwd, 1) + [pl.BlockSpec((HG_H, 1, LANES), lambda bi, i: (0, 0, 0))],
        out_specs=[pl.BlockSpec((HG_H, CHUNK, LANES), lambda bi, i: (0, fwd(bi, i), 0)),
                   pl.BlockSpec((HG_H, CHUNK, LANES), lambda bi, i: (0, bwd(bi, i), 0))],
        out_shape=[out_sd, out_sd],
        scratch_shapes=[pltpu.VMEM((2, HG_H, HG_DV, HG_DK), F32)],
        compiler_params=_cparams(("parallel", "arbitrary")),
        name="hgrn2",
    )(p3, p3, p3, p3, p3, p3, lb)


def _ab_out_kernel(mf_ref, mb_ref, hf_ref, hb_ref, og_ref, hg_ref, mlg_ref, hgg_ref, w_ref, x_ref, g1_ref,
                   o_ref, lhs_ref):
    for h in range(ML_H):
        hs = jnp.concatenate([mf_ref[2 * h] + mb_ref[2 * h], mf_ref[2 * h + 1] + mb_ref[2 * h + 1]], axis=1)
        r = hs * lax.rsqrt(jnp.mean(hs * hs, axis=-1, keepdims=True) + EPS)
        og = jnp.concatenate([og_ref[2 * h], og_ref[2 * h + 1]], axis=1)
        y = jax.nn.sigmoid(og) * (r * mlg_ref[:, h * ML_DV:(h + 1) * ML_DV])
        lhs_ref[:, h * ML_DV:(h + 1) * ML_DV] = y.astype(BF16)
    for h in range(HG_H):
        hs = hf_ref[h] + hb_ref[h]
        r = hs * lax.rsqrt(jnp.mean(hs * hs, axis=-1, keepdims=True) + EPS)
        y = _silu(hg_ref[h]) * (r * hgg_ref[:, h * HG_DV:(h + 1) * HG_DV])
        lhs_ref[:, ML_V + h * HG_DV:ML_V + (h + 1) * HG_DV] = y.astype(BF16)
    acc = jnp.dot(lhs_ref[...], w_ref[...], preferred_element_type=F32)
    o_ref[...] = x_ref[...] + g1_ref[...] * acc


def _ab_out(hm, ho, p3, ml_g, hg_g, w_out, x, g1, seg_fn, n, tm):
    d = x.shape[1]
    slab8 = lambda idx: pl.BlockSpec((SUBLANES, tm, LANES), lambda i: (idx, i, 0))
    return pl.pallas_call(
        _ab_out_kernel,
        grid=(n // tm,),
        in_specs=[slab8(0), slab8(0), slab8(0), slab8(0), slab8(AB_S_OG // SUBLANES), slab8(AB_S_HG // SUBLANES),
                  pl.BlockSpec((1, ML_V), lambda i: (0, 0)),
                  pl.BlockSpec((1, HG_V), lambda i: (0, 0)),
                  pl.BlockSpec((AB_OUT, d), lambda i: (0, 0)),
                  pl.BlockSpec((tm, d), lambda i: (i, 0)),
                  pl.BlockSpec((None, 1, d), lambda i: (seg_fn(i * tm), 0, 0))],
        out_specs=pl.BlockSpec((tm, d), lambda i: (i, 0)),
        out_shape=jax.ShapeDtypeStruct((n, d), F32),
        scratch_shapes=[pltpu.VMEM((tm, AB_OUT), BF16)],
        compiler_params=_cparams(("parallel",)),
        name="ab_out",
    )(hm[0], hm[1], ho[0], ho[1], p3, p3, ml_g.reshape(1, ML_V), hg_g.reshape(1, HG_V), w_out, x, g1)


def _ab_weight(w_in):
    d = w_in.shape[0]
    o = [0, 2 * ML_QK, 2 * ML_QK + ML_V, 2 * ML_QK + 2 * ML_V]
    g0 = o[3]
    h0 = g0 + 4 * ML_H
    pad = AB_SLABS * LANES - (w_in.shape[1])
    return jnp.concatenate([w_in[:, :g0], w_in[:, h0:], w_in[:, g0:h0], jnp.zeros((d, pad), w_in.dtype)],
                           axis=1).astype(BF16)


def _ab_layer(x, dims, seg_fn, tm, n_out, g_norm, sc, sh, g1, w_in, conv_w, conv_b, ig_b, fg_b, lb, ml_g, hg_g,
              w_out):
    p3 = _inproj(x, g_norm, sc, sh, _ab_weight(w_in), seg_fn, tm, 6 * LANES)
    kscale = jnp.concatenate([jnp.ones((ML_QK,), F32), jnp.full((ML_QK,), ML_DK ** -0.5, F32)])
    qk = _conv_slabs(p3, AB_S_Q, 2 * ML_H, _conv_pack(conv_w, conv_b, kscale), dims, 2 * ML_H)
    n = x.shape[0]
    gates = p3[AB_S_GATE, :, :4 * ML_H]
    grow = gates.reshape(n // CHUNK, CHUNK, 4 * ML_H).transpose(0, 2, 1)
    gbias = jnp.concatenate([ig_b.reshape(-1), fg_b.reshape(-1)])
    hm = _mlstm(qk, p3, grow, gbias.reshape(1, -1), gbias.reshape(-1, 1), dims)
    ho = _hgrn2(p3, lb.reshape(HG_H, 1, HG_DK), dims)
    return _ab_out(hm, ho, p3, ml_g, hg_g, w_out.astype(BF16), x, g1, seg_fn, n_out, min(tm, 256))


def _ssd_kernel(xf_ref, dcf_ref, drf_ref, xb_ref, dcb_ref, drb_ref, bc_ref, br_ref, ac_ref, ar_ref,
                of_ref, ob_ref, h_ref):
    @pl.when(pl.program_id(1) == 0)
    def _():
        h_ref[...] = jnp.zeros_like(h_ref)

    gw = SSD_R * SSD_P
    t_idx = lax.broadcasted_iota(jnp.int32, (CHUNK, gw), 0)
    s_idx = lax.broadcasted_iota(jnp.int32, (CHUNK, gw), 1) % SSD_P
    e_r = lax.broadcasted_iota(jnp.int32, (3 * SSD_R, gw), 0) % SSD_R
    e_c = lax.broadcasted_iota(jnp.int32, (3 * SSD_R, gw), 1) // SSD_P
    expand3 = (e_r == e_c).astype(BF16)
    low_half = lax.broadcasted_iota(jnp.int32, (CHUNK, LANES), 1) < SSD_P

    def expand(v):
        hi = v.astype(BF16).astype(F32)
        mid = (v - hi).astype(BF16).astype(F32)
        lo = v - hi - mid
        return jnp.dot(jnp.concatenate([hi, mid, lo], axis=1).astype(BF16), expand3, preferred_element_type=F32)

    for d, (x_ref, dc_ref, dr_ref, o_ref) in enumerate(((xf_ref, dcf_ref, drf_ref, of_ref),
                                                        (xb_ref, dcb_ref, drb_ref, ob_ref))):
        rev = d == 1
        mask = (s_idx >= t_idx) if rev else (s_idx <= t_idx)
        last = 0 if rev else CHUNK - 1
        hs = slice(d * SSD_H, (d + 1) * SSD_H)
        dtc = jax.nn.softplus(dc_ref[0][:, hs] + bc_ref[:, hs])
        lac = dtc * ac_ref[:, hs]
        dtr = jax.nn.softplus(dr_ref[hs, :] + br_ref[hs, :])
        lar = dtr * ar_ref[hs, :]
        cum_c = _dot_hi(_tri(rev).astype(F32), lac)
        cum_r = _dot_hi(lar, _tri(not rev).astype(F32))
        wgt = jnp.exp(cum_c[last:last + 1, :] - cum_c) * dtc
        for g in range(SSD_G):
            heads = slice(g * SSD_R, (g + 1) * SSD_R)
            cum_x = expand(cum_c[:, heads])
            wgt_x = expand(wgt[:, heads])
            ecum_x = jnp.exp(cum_x)
            rows = lambda a: jnp.concatenate([a[g * SSD_R + r:g * SSD_R + r + 1, :] for r in range(SSD_R)], axis=1)
            seg = jnp.exp(jnp.where(mask, cum_x - rows(cum_r), -jnp.inf)) * rows(dtr)
            x = jnp.concatenate([x_ref[4 * g + j] for j in range(4)], axis=1)
            bm = x_ref[4 * SSD_G + g]
            cm = x_ref[5 * SSD_G + g]
            cb = _dot_nt(cm, bm)
            cb2 = jnp.concatenate([cb, cb], axis=1)
            hst = h_ref[d, g]
            yoff = _dot(cm, hst)
            for j in range(4):
                ls = slice(j * LANES, (j + 1) * LANES)
                xp = x[:, ls]
                bd = jnp.concatenate([jnp.where(low_half, xp, 0.0), jnp.where(low_half, 0.0, xp)], axis=0)
                o_ref[4 * g + j] = _dot(cb2 * seg[:, ls], bd) + ecum_x[:, ls] * yoff[:, ls]
            h_ref[d, g] = hst * ecum_x[last:last + 1, :] + _dot_tn(bm, x * wgt_x)


def _ssd_scan(xbc, p3, dtrow, dt_b, neg_a, dims):
    b = dims[0]
    n = p3.shape[1]
    fwd, bwd, nch = _chunk_maps(dims)
    nxs = xbc.shape[0]

    def specs(cm):
        return [pl.BlockSpec((nxs, CHUNK, LANES), lambda bi, i: (0, cm(bi, i), 0)),
                pl.BlockSpec((1, CHUNK, LANES), lambda bi, i: (SSD_S_DT, cm(bi, i), 0)),
                pl.BlockSpec((None, 2 * SSD_H, CHUNK), lambda bi, i: (cm(bi, i), 0, 0))]

    vec = lambda shape: pl.BlockSpec(shape, lambda bi, i: (0, 0))
    out_sd = jax.ShapeDtypeStruct((D_INNER // LANES, n, LANES), F32)
    return pl.pallas_call(
        _ssd_kernel,
        grid=(b, nch),
        in_specs=specs(fwd) + specs(bwd) + [vec((1, 2 * SSD_H)), vec((2 * SSD_H, 1)),
                                            vec((1, 2 * SSD_H)), vec((2 * SSD_H, 1))],
        out_specs=[pl.BlockSpec((D_INNER // LANES, CHUNK, LANES), lambda bi, i: (0, fwd(bi, i), 0)),
                   pl.BlockSpec((D_INNER // LANES, CHUNK, LANES), lambda bi, i: (0, bwd(bi, i), 0))],
        out_shape=[out_sd, out_sd],
        scratch_shapes=[pltpu.VMEM((2, SSD_G, SSD_N, SSD_R * SSD_P), F32)],
        compiler_params=_cparams(("parallel", "arbitrary")),
        name="ssd_scan",
    )(xbc, p3, dtrow, xbc, p3, dtrow, dt_b.reshape(1, -1), dt_b.reshape(-1, 1),
      neg_a.reshape(1, -1), neg_a.reshape(-1, 1))


def _ssd_out_kernel(yf_ref, yb_ref, xs_ref, z_ref, dsk_ref, ng_ref, w_ref, x_ref, g1_ref, o_ref, lhs_ref, acc_ref):
    k = pl.program_id(1)

    @pl.when(k == 0)
    def _():
        acc_ref[...] = jnp.zeros_like(acc_ref)

    gw = D_INNER // SSD_G
    for gg in range(lhs_ref.shape[1] // gw):
        cat = lambda ref: jnp.concatenate([ref[4 * gg + j] for j in range(4)], axis=1)
        cs = slice(gg * gw, (gg + 1) * gw)
        y = cat(yf_ref) + cat(yb_ref) + dsk_ref[:, cs] * cat(xs_ref)
        u = y * _silu(cat(z_ref))
        u = u * lax.rsqrt(jnp.mean(u * u, axis=-1, keepdims=True) + EPS) * ng_ref[:, cs]
        lhs_ref[:, cs] = u.astype(BF16)
    acc_ref[...] += jnp.dot(lhs_ref[...], w_ref[...], preferred_element_type=F32)

    @pl.when(k == pl.num_programs(1) - 1)
    def _():
        o_ref[...] = x_ref[...] + g1_ref[...] * acc_ref[...]


def _ssd_out(yf, yb, xbc, p3, dskip, norm_g, w_out, x, g1, seg_fn, n_rows, tm):
    d = x.shape[1]
    tk = 1024
    nsl = tk // LANES
    slab = lambda: pl.BlockSpec((nsl, tm, LANES), lambda i, k: (k, i, 0))
    return pl.pallas_call(
        _ssd_out_kernel,
        grid=(n_rows // tm, D_INNER // tk),
        in_specs=[slab(), slab(), slab(), slab(),
                  pl.BlockSpec((1, tk), lambda i, k: (0, k)),
                  pl.BlockSpec((1, tk), lambda i, k: (0, k)),
                  pl.BlockSpec((tk, d), lambda i, k: (k, 0)),
                  pl.BlockSpec((tm, d), lambda i, k: (i, 0)),
                  pl.BlockSpec((None, 1, d), lambda i, k: (seg_fn(i * tm), 0, 0))],
        out_specs=pl.BlockSpec((tm, d), lambda i, k: (i, 0)),
        out_shape=jax.ShapeDtypeStruct((n_rows, d), F32),
        scratch_shapes=[pltpu.VMEM((tm, tk), BF16), pltpu.VMEM((tm, d), F32)],
        compiler_params=_cparams(("parallel", "arbitrary")),
        name="ssd_out",
    )(yf, yb, xbc, p3, dskip, norm_g.reshape(1, -1), w_out, x, g1)


def _ssd_layer(x, dims, seg_fn, tm, n_out, g_norm, sc, sh, g1, w_in, conv_w, conv_b, dt_b, a_log, d_skip,
               norm_g, w_out):
    d = x.shape[1]
    n = x.shape[0]
    p3 = _inproj(x, g_norm, sc, sh, w_in.astype(BF16), seg_fn, tm, 9 * LANES)
    nconv = conv_w.shape[1]
    xbc = _conv_slabs(p3, SSD_S_X, nconv // LANES, _conv_pack(conv_w, conv_b, jnp.ones((nconv,), F32)), dims, 16)
    dt = p3[SSD_S_DT]
    dtrow = dt.reshape(n // CHUNK, CHUNK, 2 * SSD_H).transpose(0, 2, 1)
    yf, yb = _ssd_scan(xbc, p3, dtrow, dt_b.reshape(-1), -jnp.exp(a_log.astype(F32)).reshape(-1), dims)
    dskip = jnp.repeat(d_skip.astype(F32), SSD_P).reshape(1, D_INNER)
    return _ssd_out(yf, yb, xbc, p3, dskip, norm_g, w_out.astype(BF16), x, g1, seg_fn, n_out, min(tm, 256))


def _regroup_kernel(x_ref, o_ref):
    a = x_ref.shape[0]
    for cc in range(SUBLANES):
        o_ref[cc * a:(cc + 1) * a, :] = x_ref[:, cc, :]


def _regroup(x, b, a, c, n_out):
    d = x.shape[1]
    cblk = c // SUBLANES
    return pl.pallas_call(
        _regroup_kernel,
        grid=(b, cblk),
        in_specs=[pl.BlockSpec((a, SUBLANES, d), lambda bi, j: (bi, j, 0))],
        out_specs=pl.BlockSpec((a * SUBLANES, d), lambda bi, j: (bi * cblk + j, 0)),
        out_shape=jax.ShapeDtypeStruct((n_out, d), F32),
        compiler_params=_cparams(("parallel", "parallel")),
        name="regroup",
    )(x.reshape(x.shape[0] // c, c, d))


MOE_TILE = 512


def _router_kernel(x_ref, g_ref, sc_ref, sh_ref, wt_ref, rb_ref, hn_ref, ii_ref, iw_ref, cnt_ref, carry_ref):
    @pl.when(pl.program_id(0) == 0)
    def _():
        carry_ref[...] = jnp.zeros_like(carry_ref)

    x = x_ref[...]
    tm = x.shape[0]
    y = x * lax.rsqrt(jnp.mean(x * x, axis=-1, keepdims=True) + EPS) * g_ref[...]
    hn = y * (1.0 + sc_ref[...]) + sh_ref[...]
    _store_token_rows(hn_ref, hn)
    logits = lax.dot_general(wt_ref[...], hn, (((1,), (1,)), ((), ())), precision=HI,
                             preferred_element_type=F32)
    score = jax.nn.sigmoid(logits)
    biased = score + rb_ref[...]
    rb = [biased[e:e + 1, :] for e in range(N_EXPERTS)]
    rs = [score[e:e + 1, :] for e in range(N_EXPERTS)]
    gsc = []
    for g in range(N_GROUPS):
        a, b, c, d = rb[EXPERTS_PER_GROUP * g:EXPERTS_PER_GROUP * (g + 1)]
        hi1, lo1, hi2, lo2 = jnp.maximum(a, b), jnp.minimum(a, b), jnp.maximum(c, d), jnp.minimum(c, d)
        gsc.append(jnp.maximum(hi1, hi2) + jnp.maximum(jnp.minimum(hi1, hi2), jnp.maximum(lo1, lo2)))
    best = jnp.zeros((1, tm), jnp.int32)
    bsc = gsc[0]
    for g in range(1, N_GROUPS):
        upd = gsc[g] > bsc
        best = jnp.where(upd, g, best)
        bsc = jnp.where(upd, gsc[g], bsc)

    def pick(rows, p):
        out = rows[p]
        for g in range(1, N_GROUPS):
            out = jnp.where(best == g, rows[EXPERTS_PER_GROUP * g + p], out)
        return out

    vals = [pick(rb, p) for p in range(EXPERTS_PER_GROUP)]
    scs = [pick(rs, p) for p in range(EXPERTS_PER_GROUP)]
    p1, v1, s1 = jnp.zeros((1, tm), jnp.int32), vals[0], scs[0]
    for p in range(1, EXPERTS_PER_GROUP):
        upd = vals[p] > v1
        p1, v1, s1 = jnp.where(upd, p, p1), jnp.where(upd, vals[p], v1), jnp.where(upd, scs[p], s1)
    p2 = jnp.zeros((1, tm), jnp.int32)
    v2 = jnp.full((1, tm), -jnp.inf, F32)
    s2 = jnp.zeros((1, tm), F32)
    for p in range(EXPERTS_PER_GROUP):
        upd = (p1 != p) & (vals[p] > v2)
        p2, v2, s2 = jnp.where(upd, p, p2), jnp.where(upd, vals[p], v2), jnp.where(upd, scs[p], s2)
    e1 = best * EXPERTS_PER_GROUP + p1
    e2 = best * EXPERTS_PER_GROUP + p2
    tot = s1 + s2
    eiota = lax.broadcasted_iota(jnp.int32, (N_EXPERTS, tm), 0)
    oh1 = (eiota == e1).astype(F32)
    oh2 = (eiota == e2).astype(F32)
    oh = oh1 + oh2
    before = (lax.broadcasted_iota(jnp.int32, (tm, tm), 0) < lax.broadcasted_iota(jnp.int32, (tm, tm), 1))
    cnt = _dot(oh, before.astype(BF16)) + carry_ref[:, :1]
    r1 = jnp.sum(oh1 * cnt, axis=0, keepdims=True).astype(jnp.int32)
    r2 = jnp.sum(oh2 * cnt, axis=0, keepdims=True).astype(jnp.int32)
    zi = jnp.zeros((SUBLANES - 4, tm), jnp.int32)
    ii_ref[...] = jnp.concatenate([e1, e2, r1, r2, zi], axis=0)
    iw_ref[...] = jnp.concatenate([s1 / tot, s2 / tot, jnp.zeros((SUBLANES - 2, tm), F32)], axis=0)
    carry = carry_ref[...] + jnp.sum(oh, axis=1, keepdims=True)
    carry_ref[...] = carry
    cnt_ref[...] = carry


def _router(x, n_rows, g, sc, sh, seg_fn, router_wt, router_b):
    d = x.shape[1]
    tm = MOE_TILE
    return pl.pallas_call(
        _router_kernel,
        grid=(n_rows // tm,),
        in_specs=[pl.BlockSpec((tm, d), lambda i: (i, 0)),
                  pl.BlockSpec((1, d), lambda i: (0, 0)),
                  pl.BlockSpec((None, 1, d), lambda i: (seg_fn(i * tm), 0, 0)),
                  pl.BlockSpec((None, 1, d), lambda i: (seg_fn(i * tm), 0, 0)),
                  pl.BlockSpec((N_EXPERTS, d), lambda i: (0, 0)),
                  pl.BlockSpec((N_EXPERTS, 1), lambda i: (0, 0))],
        out_specs=[pl.BlockSpec((tm, d // LANES, LANES), lambda i: (i, 0, 0)),
                   pl.BlockSpec((SUBLANES, tm), lambda i: (0, i)),
                   pl.BlockSpec((SUBLANES, tm), lambda i: (0, i)),
                   pl.BlockSpec((N_EXPERTS, LANES), lambda i: (0, 0))],
        out_shape=[jax.ShapeDtypeStruct((n_rows, d // LANES, LANES), F32),
                   jax.ShapeDtypeStruct((SUBLANES, n_rows), jnp.int32),
                   jax.ShapeDtypeStruct((SUBLANES, n_rows), F32),
                   jax.ShapeDtypeStruct((N_EXPERTS, LANES), F32)],
        scratch_shapes=[pltpu.VMEM((N_EXPERTS, LANES), F32)],
        compiler_params=_cparams(("arbitrary",)),
        name="router",
    )(x, g.reshape(1, d), sc, sh, router_wt, router_b.reshape(N_EXPERTS, 1))


TOK_SUB = D_MODEL // LANES


def _store_token_rows(ref, val):
    for s in range(TOK_SUB):
        ref[:, s, :] = val[:, s * LANES:(s + 1) * LANES]


def _row_copy(src_ref, o_ref, sem, src_row, dst_row):
    return pltpu.make_async_copy(src_ref.at[pl.ds(src_row, 1)], o_ref.at[pl.ds(dst_row, 1)], sem)


def _gather_kernel(idx_ref, src_ref, o_ref, sem):
    rows = o_ref.shape[0]

    def issue(r8, carry):
        for u in range(SUBLANES):
            r = r8 * SUBLANES + u
            _row_copy(src_ref, o_ref, sem, idx_ref[r], r).start()
        return carry

    lax.fori_loop(0, rows // SUBLANES, issue, 0)
    pltpu.make_async_copy(src_ref.at[pl.ds(0, rows)], o_ref, sem).wait()


def _gather_rows(idx, src, n_out):
    rows = MOE_TILE
    return pl.pallas_call(
        _gather_kernel,
        grid=(n_out // rows,),
        in_specs=[pl.BlockSpec((rows,), lambda i: (i,), memory_space=pltpu.SMEM),
                  pl.BlockSpec(memory_space=pl.ANY)],
        out_specs=pl.BlockSpec((rows,) + src.shape[1:], lambda i: (i, 0, 0)),
        out_shape=jax.ShapeDtypeStruct((n_out,) + src.shape[1:], src.dtype),
        scratch_shapes=[pltpu.SemaphoreType.DMA],
        compiler_params=_cparams(("arbitrary",)),
        name="gather_rows",
    )(idx, src)


def _expert_kernel(te_ref, nu_ref, x_ref, w1_ref, w3_ref, w2_ref, o_ref, xb_ref, acc_ref):
    i, f = pl.program_id(0), pl.program_id(1)
    nf = pl.num_programs(1)
    used = i < nu_ref[0]

    @pl.when(used & (f == 0))
    def _():
        for s in range(TOK_SUB):
            xb_ref[:, s * LANES:(s + 1) * LANES] = x_ref[:, s, :].astype(BF16)
        acc_ref[...] = jnp.zeros_like(acc_ref)

    @pl.when(used)
    def _():
        xb = xb_ref[...]
        a = _silu(_dot(xb, w1_ref[...])) * _dot(xb, w3_ref[...])
        acc_ref[...] += _dot(a, w2_ref[...])

    @pl.when(f == nf - 1)
    def _():
        _store_token_rows(o_ref, jnp.where(used, acc_ref[...], 0.0))


def _experts(tile_e, n_used, xs, w1, w3, w2):
    p = xs.shape[0]
    d = D_MODEL
    te, tf = MOE_TILE, 512
    nf = D_EXPERT // tf
    fidx = lambda i, f, nu: jnp.where(i < nu[0], f, nf - 1)
    grid_spec = pltpu.PrefetchScalarGridSpec(
        num_scalar_prefetch=2,
        grid=(p // te, nf),
        in_specs=[pl.BlockSpec((te, TOK_SUB, LANES), lambda i, f, te_r, nu: (i, 0, 0)),
                  pl.BlockSpec((None, d, tf), lambda i, f, te_r, nu: (te_r[i], 0, fidx(i, f, nu))),
                  pl.BlockSpec((None, d, tf), lambda i, f, te_r, nu: (te_r[i], 0, fidx(i, f, nu))),
                  pl.BlockSpec((None, tf, d), lambda i, f, te_r, nu: (te_r[i], fidx(i, f, nu), 0))],
        out_specs=pl.BlockSpec((te, TOK_SUB, LANES), lambda i, f, te_r, nu: (i, 0, 0)),
        scratch_shapes=[pltpu.VMEM((te, d), BF16), pltpu.VMEM((te, d), F32)])
    return pl.pallas_call(
        _expert_kernel,
        grid_spec=grid_spec,
        out_shape=jax.ShapeDtypeStruct((p, TOK_SUB, LANES), F32),
        compiler_params=_cparams(("arbitrary", "arbitrary")),
        name="experts",
    )(tile_e, n_used, xs, w1, w3, w2)


def _combine_kernel(x_ref, y1_ref, y2_ref, w_ref, g2_ref, fg_ref, o_ref, *, final):
    w = w_ref[...]
    for s in range(TOK_SUB):
        cs = slice(s * LANES, (s + 1) * LANES)
        o_ref[:, cs] = x_ref[:, cs] + g2_ref[:, cs] * (w[:, 0:1] * y1_ref[:, s, :] + w[:, 1:2] * y2_ref[:, s, :])
    if final:
        out = o_ref[...]
        o_ref[...] = out * lax.rsqrt(jnp.mean(out * out, axis=-1, keepdims=True) + EPS) * fg_ref[...]


def _combine(x, n_rows, yg, wts, g2, seg_fn, final_g):
    d = x.shape[1]
    tm = 256
    nt = n_rows // tm
    fg = jnp.ones((1, d), F32) if final_g is None else final_g.reshape(1, d)
    return pl.pallas_call(
        functools.partial(_combine_kernel, final=final_g is not None),
        grid=(nt,),
        in_specs=[pl.BlockSpec((tm, d), lambda i: (i, 0)),
                  pl.BlockSpec((tm, TOK_SUB, LANES), lambda i: (i, 0, 0)),
                  pl.BlockSpec((tm, TOK_SUB, LANES), lambda i: (nt + i, 0, 0)),
                  pl.BlockSpec((tm, 2), lambda i: (i, 0)),
                  pl.BlockSpec((None, 1, d), lambda i: (seg_fn(i * tm), 0, 0)),
                  pl.BlockSpec((1, d), lambda i: (0, 0))],
        out_specs=pl.BlockSpec((tm, d), lambda i: (i, 0)),
        out_shape=jax.ShapeDtypeStruct((n_rows, d), F32),
        compiler_params=_cparams(("parallel",)),
        name="combine",
    )(x, yg, yg, wts, g2, fg)


def _moe(x, n_rows, seg_fn, g, sc, sh, g2, router_wt, router_b, w1, w3, w2, final_g=None):
    te = MOE_TILE
    hn, ii, iw, cnt = _router(x, n_rows, g, sc, sh, seg_fn, router_wt, router_b)
    counts = cnt[:, 0].astype(jnp.int32)
    padded = (counts + te - 1) // te * te
    ends = jnp.cumsum(padded)
    base = ends - padded
    pos1 = base[ii[0]] + ii[2]
    pos2 = base[ii[1]] + ii[3]
    p = 2 * n_rows + N_EXPERTS * te
    ntile = p // te
    n_used = ends[-1] // te
    tiles = jnp.arange(ntile, dtype=jnp.int32)
    tile_e = jnp.minimum(jnp.searchsorted(ends, tiles * te, side='right'), N_EXPERTS - 1).astype(jnp.int32)
    tile_e = jnp.where(tiles < n_used, tile_e, tile_e[jnp.maximum(n_used - 1, 0)])
    tok = jnp.arange(n_rows, dtype=jnp.int32)
    src = jnp.zeros((p,), jnp.int32).at[pos1].set(tok).at[pos2].set(tok)
    xs = _gather_rows(src, hn, p)
    ys = _experts(tile_e, n_used.reshape(1).astype(jnp.int32), xs, w1, w3, w2)
    yg = _gather_rows(jnp.concatenate([pos1, pos2]), ys, 2 * n_rows)
    return _combine(x, n_rows, yg, iw[:2].T, g2, seg_fn, final_g)


def kernel(x, c, ctx, c_ctx, ada_w, ada_b, norm1_g, norm2_g, ab_w_in, ab_conv_w, ab_conv_b, ml_ig_b, ml_fg_b,
           hg_lb, ml_norm_g, hg_norm_g, ab_w_out, ssd_w_in, ssd_conv_w, ssd_conv_b, ssd_dt_b, ssd_a_log, ssd_d,
           ssd_norm_g, ssd_w_out, router_w, router_b, moe_w1, moe_w3, moe_w2, final_g):
    b, seq, d = x.shape
    ctxl = ctx.shape[1]
    depth = ada_w.shape[0]
    dims = (b, seq, ctxl)
    nl, nc = b * seq, b * ctxl
    rows = seq // GRID_W
    tm = min(512, nc)
    seg_fn = lambda row: jnp.where(row < nl, row // seq, b)
    c8 = jnp.concatenate([c, c_ctx[None], jnp.zeros((SUBLANES - b - 1, d), F32)])
    mods = _modulation(c8, ada_w, ada_b)[:, :b + 1].reshape(depth, b + 1, 6, 1, d)
    lb_all = jnp.cumsum(jax.nn.softmax(hg_lb.astype(F32), axis=0), axis=0)
    router_wt = router_w.T
    xr = jnp.concatenate([x.reshape(nl, d), ctx.reshape(nc, d)])
    transposed = False
    for l in range(depth):
        sh1, sc1, g1, sh2, sc2, g2 = (mods[l][:, k] for k in range(6))
        keep_ctx = l < depth - 1
        n_out = nl + nc if keep_ctx else nl
        j = l // 2
        if (l % 2 == 1) != transposed:
            xt = _regroup(xr, b, GRID_W if transposed else rows, rows if transposed else GRID_W, nl)
            xr = jnp.concatenate([xt, xr[nl:]])
            transposed = not transposed
        if l % 2 == 0:
            xr = _ab_layer(xr, dims, seg_fn, tm, n_out, norm1_g[l], sc1, sh1, g1, ab_w_in[j], ab_conv_w[j],
                           ab_conv_b[j], ml_ig_b[j], ml_fg_b[j], lb_all[l], ml_norm_g[j], hg_norm_g[j], ab_w_out[j])
        else:
            xr = _ssd_layer(xr, dims, seg_fn, tm, n_out, norm1_g[l], sc1, sh1, g1, ssd_w_in[j], ssd_conv_w[j],
                            ssd_conv_b[j], ssd_dt_b[j], ssd_a_log[j], ssd_d[j], ssd_norm_g[j], ssd_w_out[j])
        xr = _moe(xr, n_out, seg_fn, norm2_g[l], sc2, sh2, g2, router_wt, router_b, moe_w1[l], moe_w3[l],
                  moe_w2[l], final_g if l == depth - 1 else None)
    if transposed:
        xr = _regroup(xr, b, GRID_W, rows, nl)
    return xr[:nl].reshape(b, seq, d)
```

```python
import functools
import math

import jax
import jax.numpy as jnp
from jax import lax
from jax.experimental import pallas as pl
from jax.experimental.pallas import tpu as pltpu

F32 = jnp.float32
BF16 = jnp.bfloat16
HI = lax.Precision.HIGHEST

D_MODEL = 2048
GRID_W = 64
EPS = 1e-6
CHUNK = 64
CONV_K = 5
ML_H, ML_DK, ML_DV = 4, 128, 256
HG_H, HG_DK, HG_DV = 8, 128, 128
ML_QK, ML_V = ML_H * ML_DK, ML_H * ML_DV
HG_K, HG_V = HG_H * HG_DK, HG_H * HG_DV
AB_OUT = ML_V + HG_V
D_INNER = 2 * D_MODEL
SSD_P, SSD_G, SSD_N = 64, 8, 128
SSD_H = D_INNER // SSD_P
SSD_R = SSD_H // SSD_G
N_EXPERTS, N_GROUPS, TOP_K, D_EXPERT = 16, 4, 2, 1024
EXPERTS_PER_GROUP = N_EXPERTS // N_GROUPS

LANES = 128
SUBLANES = 8
VMEM_LIMIT = 56 * 1024 * 1024

AB_S_Q, AB_S_K, AB_S_V, AB_S_OG, AB_S_HQ, AB_S_HF, AB_S_HI, AB_S_HG, AB_S_GATE = 0, 4, 8, 16, 24, 32, 48, 56, 64
AB_SLABS = 66
SSD_S_Z, SSD_S_X, SSD_S_B, SSD_S_C, SSD_S_DT = 0, 32, 64, 72, 80
SSD_SLABS = 81


def _cparams(sem):
    return pltpu.CompilerParams(dimension_semantics=sem, vmem_limit_bytes=VMEM_LIMIT)


def _silu(x):
    return x * jax.nn.sigmoid(x)


def _dot(a, b):
    return jnp.dot(a.astype(BF16), b.astype(BF16), preferred_element_type=F32)


def _dot_nt(a, b):
    return lax.dot_general(a.astype(BF16), b.astype(BF16), (((1,), (1,)), ((), ())),
                           preferred_element_type=F32)


def _dot_tn(a, b):
    return lax.dot_general(a.astype(BF16), b.astype(BF16), (((0,), (0,)), ((), ())),
                           preferred_element_type=F32)


def _dot_hi(a, b):
    return jnp.dot(a, b, precision=HI, preferred_element_type=F32)


def _tri(rev):
    t = lax.broadcasted_iota(jnp.int32, (CHUNK, CHUNK), 0)
    s = lax.broadcasted_iota(jnp.int32, (CHUNK, CHUNK), 1)
    return (s >= t) if rev else (s <= t)


def _mod_kernel(c_ref, w_ref, b_ref, o_ref):
    c = c_ref[...]
    o_ref[...] = _dot(_silu(c), w_ref[...]) + b_ref[...]


def _modulation(c8, ada_w, ada_b):
    depth, d, d6 = ada_w.shape
    tn = 1024
    return pl.pallas_call(
        _mod_kernel,
        grid=(depth, d6 // tn),
        in_specs=[pl.BlockSpec((SUBLANES, d), lambda l, j: (0, 0)),
                  pl.BlockSpec((None, d, tn), lambda l, j: (l, 0, j)),
                  pl.BlockSpec((None, 1, tn), lambda l, j: (l, 0, j))],
        out_specs=pl.BlockSpec((None, SUBLANES, tn), lambda l, j: (l, 0, j)),
        out_shape=jax.ShapeDtypeStruct((depth, SUBLANES, d6), F32),
        compiler_params=_cparams(("parallel", "parallel")),
        name="modulation",
    )(c8, ada_w, ada_b.reshape(depth, 1, d6))


def _inproj_kernel(x_ref, g_ref, sc_ref, sh_ref, w_ref, o_ref, h_ref):
    @pl.when(pl.program_id(1) == 0)
    def _():
        x = x_ref[...]
        y = x * lax.rsqrt(jnp.mean(x * x, axis=-1, keepdims=True) + EPS) * g_ref[...]
        h_ref[...] = (y * (1.0 + sc_ref[...]) + sh_ref[...]).astype(BF16)

    acc = jnp.dot(h_ref[...], w_ref[...], preferred_element_type=F32)
    for s in range(o_ref.shape[0]):
        o_ref[s] = acc[:, s * LANES:(s + 1) * LANES]


def _inproj(x, g, sc, sh, w, seg_fn, tm, tn):
    n, d = x.shape
    ncols = w.shape[1]
    nsl = tn // LANES
    return pl.pallas_call(
        _inproj_kernel,
        grid=(n // tm, ncols // tn),
        in_specs=[pl.BlockSpec((tm, d), lambda i, j: (i, 0)),
                  pl.BlockSpec((1, d), lambda i, j: (0, 0)),
                  pl.BlockSpec((None, 1, d), lambda i, j: (seg_fn(i * tm), 0, 0)),
                  pl.BlockSpec((None, 1, d), lambda i, j: (seg_fn(i * tm), 0, 0)),
                  pl.BlockSpec((d, tn), lambda i, j: (0, j))],
        out_specs=pl.BlockSpec((nsl, tm, LANES), lambda i, j: (j, i, 0)),
        out_shape=jax.ShapeDtypeStruct((ncols // LANES, n, LANES), F32),
        scratch_shapes=[pltpu.VMEM((tm, d), BF16)],
        compiler_params=_cparams(("parallel", "arbitrary")),
        name="inproj",
    )(x, g.reshape(1, d), sc, sh, w)


def _conv_kernel(prev_ref, cur_ref, next_ref, w_ref, o_ref, *, tt, tiles_lat, n_lat_tiles, tiles_ctx):
    i = pl.program_id(1)
    in_lat = i < n_lat_tiles
    pos = jnp.where(in_lat, i % tiles_lat, (i - n_lat_tiles) % tiles_ctx)
    last = jnp.where(in_lat, tiles_lat - 1, tiles_ctx - 1)
    keep_prev = (pos != 0).astype(F32)
    keep_next = (pos != last).astype(F32)
    ext = jnp.concatenate([prev_ref[...] * keep_prev, cur_ref[...], next_ref[...] * keep_next], axis=1)
    w = w_ref[...]
    acc = jnp.zeros(cur_ref.shape, F32) + w[:, CONV_K:CONV_K + 1, :]
    for k in range(CONV_K):
        shift = (CONV_K // 2 - k) % (tt + 2 * SUBLANES)
        r = ext if shift == 0 else pltpu.roll(ext, shift, 1)
        acc = acc + r[:, SUBLANES:SUBLANES + tt, :] * w[:, k:k + 1, :]
    o_ref[...] = _silu(acc) * w[:, CONV_K + 1:CONV_K + 2, :]


def _conv_slabs(p3, slab0, nslab, wpack, dims, sb):
    b, seq, ctxl = dims
    n = p3.shape[1]
    tt = min(256, ctxl)
    t8 = tt // SUBLANES
    nblk8 = n // SUBLANES
    s0 = slab0 // sb
    kern = functools.partial(_conv_kernel, tt=tt, tiles_lat=seq // tt, n_lat_tiles=b * seq // tt,
                             tiles_ctx=ctxl // tt)
    return pl.pallas_call(
        kern,
        grid=(nslab // sb, n // tt),
        in_specs=[pl.BlockSpec((sb, SUBLANES, LANES), lambda s, i: (s0 + s, jnp.maximum(i * t8 - 1, 0), 0)),
                  pl.BlockSpec((sb, tt, LANES), lambda s, i: (s0 + s, i, 0)),
                  pl.BlockSpec((sb, SUBLANES, LANES),
                               lambda s, i: (s0 + s, jnp.minimum((i + 1) * t8, nblk8 - 1), 0)),
                  pl.BlockSpec((sb, SUBLANES, LANES), lambda s, i: (s, 0, 0))],
        out_specs=pl.BlockSpec((sb, tt, LANES), lambda s, i: (s, i, 0)),
        out_shape=jax.ShapeDtypeStruct((nslab, n, LANES), F32),
        compiler_params=_cparams(("parallel", "parallel")),
        name="conv",
    )(p3, p3, p3, wpack)


def _conv_pack(conv_w, conv_b, scale):
    c = conv_w.shape[1]
    rows = jnp.concatenate([conv_w, conv_b[None], scale[None], jnp.zeros((1, c), F32)], axis=0)
    return rows.reshape(SUBLANES, c // LANES, LANES).transpose(1, 0, 2)


def _chunk_maps(dims):
    b, seq, ctxl = dims
    ncc, nlc = ctxl // CHUNK, seq // CHUNK

    def fwd(bi, i):
        return jnp.where(i < ncc, b * nlc + bi * ncc + i, bi * nlc + (i - ncc))

    def bwd(bi, i):
        return jnp.where(i < ncc, b * nlc + bi * ncc + (ncc - 1 - i), bi * nlc + (nlc - 1 - (i - ncc)))

    return fwd, bwd, ncc + nlc


def _mlstm_kernel(qkf_ref, vf_ref, gcf_ref, grf_ref, qkb_ref, vb_ref, gcb_ref, grb_ref, bc_ref, br_ref,
                  of_ref, ob_ref, c_ref, n_ref, m_ref):
    @pl.when(pl.program_id(1) == 0)
    def _():
        c_ref[...] = jnp.zeros_like(c_ref)
        n_ref[...] = jnp.zeros_like(n_ref)
        m_ref[...] = jnp.zeros_like(m_ref)

    ng = 2 * ML_H
    for d, (qk_ref, v_ref, gc_ref, gr_ref, o_ref) in enumerate(
            ((qkf_ref, vf_ref, gcf_ref, grf_ref, of_ref), (qkb_ref, vb_ref, gcb_ref, grb_ref, ob_ref))):
        rev = d == 1
        mask = _tri(rev)
        gcol = gc_ref[0][:, :2 * ng] + bc_ref[...]
        grow = gr_ref[...] + br_ref[...]
        ic_all = gcol[:, d * ML_H:(d + 1) * ML_H]
        lfc_all = jax.nn.log_sigmoid(gcol[:, ng + d * ML_H:ng + (d + 1) * ML_H])
        ir_all = grow[d * ML_H:(d + 1) * ML_H, :]
        lfr_all = jax.nn.log_sigmoid(grow[ng + d * ML_H:ng + (d + 1) * ML_H, :])
        incl = mask.astype(F32)
        bcol_all = _dot_hi(incl, lfc_all)
        brow_all = _dot_hi(lfr_all, _tri(not rev).astype(F32))
        last = 0 if rev else CHUNK - 1
        for h in range(ML_H):
            q = qk_ref[h]
            k = qk_ref[ML_H + h]
            v = jnp.concatenate([v_ref[2 * h], v_ref[2 * h + 1]], axis=1)
            bcol, icol = bcol_all[:, h:h + 1], ic_all[:, h:h + 1]
            brow, irow = brow_all[h:h + 1, :], ir_all[h:h + 1, :]
            cst = c_ref[d, h]
            nst = n_ref[d, h]
            mprev = m_ref[d, h][:, :1]
            logd = jnp.where(mask, bcol - brow + irow, -jnp.inf)
            inter = bcol + mprev
            m_t = jnp.maximum(inter, jnp.max(logd, axis=1, keepdims=True))
            s = _dot_nt(q, k) * jnp.exp(logd - m_t)
            sc = jnp.exp(inter - m_t)
            num = _dot(s, v) + sc * _dot_nt(q, cst)
            den = jnp.sum(s, axis=1, keepdims=True) + sc * jnp.sum(q * nst, axis=1, keepdims=True)
            hout = num / jnp.maximum(jnp.abs(den), jnp.exp(-m_t))
            o_ref[2 * h] = hout[:, :LANES]
            o_ref[2 * h + 1] = hout[:, LANES:]
            b_last = bcol[last:last + 1, :]
            wlog = b_last - bcol + icol
            m_new = jnp.maximum(b_last + mprev, jnp.max(wlog, axis=0, keepdims=True))
            w = jnp.exp(wlog - m_new)
            dec = jnp.exp(b_last + mprev - m_new)
            c_ref[d, h] = dec * cst + _dot_tn(w * v, k)
            n_ref[d, h] = dec * nst + jnp.sum(w * k, axis=0, keepdims=True)
            m_ref[d, h] = jnp.broadcast_to(m_new, (1, LANES))


def _mlstm(qk, p3, grow, bcol, brow, dims):
    b = dims[0]
    n = p3.shape[1]
    fwd, bwd, nch = _chunk_maps(dims)

    def specs(cm):
        return [pl.BlockSpec((2 * ML_H, CHUNK, LANES), lambda bi, i: (0, cm(bi, i), 0)),
                pl.BlockSpec((2 * ML_H, CHUNK, LANES), lambda bi, i: (AB_S_V // (2 * ML_H), cm(bi, i), 0)),
                pl.BlockSpec((1, CHUNK, LANES), lambda bi, i: (AB_S_GATE, cm(bi, i), 0)),
                pl.BlockSpec((None, 4 * ML_H, CHUNK), lambda bi, i: (cm(bi, i), 0, 0))]

    out_sd = jax.ShapeDtypeStruct((2 * ML_H, n, LANES), F32)
    return pl.pallas_call(
        _mlstm_kernel,
        grid=(b, nch),
        in_specs=specs(fwd) + specs(bwd) + [pl.BlockSpec((1, 4 * ML_H), lambda bi, i: (0, 0)),
                                            pl.BlockSpec((4 * ML_H, 1), lambda bi, i: (0, 0))],
        out_specs=[pl.BlockSpec((2 * ML_H, CHUNK, LANES), lambda bi, i: (0, fwd(bi, i), 0)),
                   pl.BlockSpec((2 * ML_H, CHUNK, LANES), lambda bi, i: (0, bwd(bi, i), 0))],
        out_shape=[out_sd, out_sd],
        scratch_shapes=[pltpu.VMEM((2, ML_H, ML_DV, ML_DK), F32),
                        pltpu.VMEM((2, ML_H, 1, ML_DK), F32),
                        pltpu.VMEM((2, ML_H, 1, LANES), F32)],
        compiler_params=_cparams(("parallel", "arbitrary")),
        name="mlstm",
    )(qk, p3, p3, grow, qk, p3, p3, grow, bcol, brow)


def _bcast_rows(a, rows, span):
    parts = [jnp.broadcast_to(a[r:r + 1, :], (span, a.shape[1])) for r in rows]
    return parts[0] if len(parts) == 1 else jnp.concatenate(parts, axis=0)


def _hgrn2_kernel(qf_ref, ff_ref, vf_ref, qb_ref, fb_ref, vb_ref, lb_ref, of_ref, ob_ref, s_ref):
    @pl.when(pl.program_id(1) == 0)
    def _():
        s_ref[...] = jnp.zeros_like(s_ref)

    t = lax.broadcasted_iota(jnp.int32, (CHUNK, CHUNK), 0)
    s = lax.broadcasted_iota(jnp.int32, (CHUNK, CHUNK), 1)
    for d, (q_ref, f_ref, v_ref, o_ref) in enumerate(((qf_ref, ff_ref, vf_ref, of_ref),
                                                      (qb_ref, fb_ref, vb_ref, ob_ref))):
        rev = d == 1
        mask = _tri(rev)
        incl = mask.astype(F32)
        last = 0 if rev else CHUNK - 1
        for h in range(HG_H):
            q = q_ref[h]
            lb = lb_ref[h]
            f = lb + (1.0 - lb) * jax.nn.sigmoid(f_ref[h])
            k = 1.0 - f
            lg = jnp.log(f)
            v = v_ref[h]
            a = _dot_hi(incl, lg)
            scores = jnp.zeros((CHUNK, CHUNK), F32)
            for m in (32, 16, 8):
                nb = CHUNK // (2 * m)
                ref_rows = [bi * 2 * m + (m if rev else m - 1) for bi in range(nb)]
                aref = _bcast_rows(a, ref_rows, 2 * m)
                eq = jnp.exp(jnp.minimum(a - aref, 0.0))
                ek = jnp.exp(jnp.minimum(aref - a, 0.0))
                same = (t // (2 * m)) == (s // (2 * m))
                t_late = ((t // m) % 2 == 0) if rev else ((t // m) % 2 == 1)
                s_early = ((s // m) % 2 == 1) if rev else ((s // m) % 2 == 0)
                lvl = same & t_late & s_early
                scores = scores + jnp.where(lvl, _dot_nt(q * eq, k * ek), 0.0)
            aex = a - lg
            ref_rows = [bi * SUBLANES + (SUBLANES - 1 if rev else 0) for bi in range(CHUNK // SUBLANES)]
            aref = _bcast_rows(aex, ref_rows, SUBLANES)
            diag = ((t // SUBLANES) == (s // SUBLANES)) & mask
            scores = scores + jnp.where(diag, _dot_nt(q * jnp.exp(a - aref), k * jnp.exp(aref - a)), 0.0)
            st = s_ref[d, h]
            o_ref[h] = _dot(scores, v) + _dot_nt(q * jnp.exp(a), st)
            a_last = a[last:last + 1, :]
            s_ref[d, h] = st * jnp.exp(a_last) + _dot_tn(v, k * jnp.exp(a_last - a))


def _hgrn2(p3, lb, dims):
    b = dims[0]
    n = p3.shape[1]
    fwd, bwd, nch = _chunk_maps(dims)

    def specs(cm, d):
        return [pl.BlockSpec((HG_H, CHUNK, LANES), lambda bi, i: (AB_S_HQ // HG_H, cm(bi, i), 0)),
                pl.BlockSpec((HG_H, CHUNK, LANES), lambda bi, i: (AB_S_HF // HG_H + d, cm(bi, i), 0)),
                pl.BlockSpec((HG_H, CHUNK, LANES), lambda bi, i: (AB_S_HI // HG_H, cm(bi, i), 0))]

    out_sd = jax.ShapeDtypeStruct((HG_H, n, LANES), F32)
    return pl.pallas_call(
        _hgrn2_kernel,
        grid=(b, nch),
        in_specs=specs(fwd, 0) + specs(bwd, 1) + [pl.BlockSpec((HG_H, 1, LANES), lambda bi, i: (0, 0, 0))],
        out_specs=[pl.BlockSpec((HG_H, CHUNK, LANES), lambda bi, i: (0, fwd(bi, i), 0)),
                   pl.BlockSpec((HG_H, CHUNK, LANES), lambda bi, i: (0, bwd(bi, i), 0))],
        out_shape=[out_sd, out_sd],
        scratch_shapes=[pltpu.VMEM((2, HG_H, HG_DV, HG_DK), F32)],
        compiler_params=_cparams(("parallel", "arbitrary")),
        name="hgrn2",
    )(p3, p3, p3, p3, p3, p3, lb)


def _ab_out_kernel(mf_ref, mb_ref, hf_ref, hb_ref, og_ref, hg_ref, mlg_ref, hgg_ref, w_ref, x_ref, g1_ref,
                   o_ref, lhs_ref):
    for h in range(ML_H):
        hs = jnp.concatenate([mf_ref[2 * h] + mb_ref[2 * h], mf_ref[2 * h + 1] + mb_ref[2 * h + 1]], axis=1)
        r = hs * lax.rsqrt(jnp.mean(hs * hs, axis=-1, keepdims=True) + EPS)
        og = jnp.concatenate([og_ref[2 * h], og_ref[2 * h + 1]], axis=1)
        y = jax.nn.sigmoid(og) * (r * mlg_ref[:, h * ML_DV:(h + 1) * ML_DV])
        lhs_ref[:, h * ML_DV:(h + 1) * ML_DV] = y.astype(BF16)
    for h in range(HG_H):
        hs = hf_ref[h] + hb_ref[h]
        r = hs * lax.rsqrt(jnp.mean(hs * hs, axis=-1, keepdims=True) + EPS)
        y = _silu(hg_ref[h]) * (r * hgg_ref[:, h * HG_DV:(h + 1) * HG_DV])
        lhs_ref[:, ML_V + h * HG_DV:ML_V + (h + 1) * HG_DV] = y.astype(BF16)
    acc = jnp.dot(lhs_ref[...], w_ref[...], preferred_element_type=F32)
    o_ref[...] = x_ref[...] + g1_ref[...] * acc


def _ab_out(hm, ho, p3, ml_g, hg_g, w_out, x, g1, seg_fn, n, tm):
    d = x.shape[1]
    slab8 = lambda idx: pl.BlockSpec((SUBLANES, tm, LANES), lambda i: (idx, i, 0))
    return pl.pallas_call(
        _ab_out_kernel,
        grid=(n // tm,),
        in_specs=[slab8(0), slab8(0), slab8(0), slab8(0), slab8(AB_S_OG // SUBLANES), slab8(AB_S_HG // SUBLANES),
                  pl.BlockSpec((1, ML_V), lambda i: (0, 0)),
                  pl.BlockSpec((1, HG_V), lambda i: (0, 0)),
                  pl.BlockSpec((AB_OUT, d), lambda i: (0, 0)),
                  pl.BlockSpec((tm, d), lambda i: (i, 0)),
                  pl.BlockSpec((None, 1, d), lambda i: (seg_fn(i * tm), 0, 0))],
        out_specs=pl.BlockSpec((tm, d), lambda i: (i, 0)),
        out_shape=jax.ShapeDtypeStruct((n, d), F32),
        scratch_shapes=[pltpu.VMEM((tm, AB_OUT), BF16)],
        compiler_params=_cparams(("parallel",)),
        name="ab_out",
    )(hm[0], hm[1], ho[0], ho[1], p3, p3, ml_g.reshape(1, ML_V), hg_g.reshape(1, HG_V), w_out, x, g1)


def _ab_weight(w_in):
    d = w_in.shape[0]
    o = [0, 2 * ML_QK, 2 * ML_QK + ML_V, 2 * ML_QK + 2 * ML_V]
    g0 = o[3]
    h0 = g0 + 4 * ML_H
    pad = AB_SLABS * LANES - (w_in.shape[1])
    return jnp.concatenate([w_in[:, :g0], w_in[:, h0:], w_in[:, g0:h0], jnp.zeros((d, pad), w_in.dtype)],
                           axis=1).astype(BF16)


def _ab_layer(x, dims, seg_fn, tm, n_out, g_norm, sc, sh, g1, w_in, conv_w, conv_b, ig_b, fg_b, lb, ml_g, hg_g,
              w_out):
    p3 = _inproj(x, g_norm, sc, sh, _ab_weight(w_in), seg_fn, tm, 22 * LANES)
    kscale = jnp.concatenate([jnp.ones((ML_QK,), F32), jnp.full((ML_QK,), ML_DK ** -0.5, F32)])
    qk = _conv_slabs(p3, AB_S_Q, 2 * ML_H, _conv_pack(conv_w, conv_b, kscale), dims, 2 * ML_H)
    n = x.shape[0]
    gates = p3[AB_S_GATE, :, :4 * ML_H]
    grow = gates.reshape(n // CHUNK, CHUNK, 4 * ML_H).transpose(0, 2, 1)
    gbias = jnp.concatenate([ig_b.reshape(-1), fg_b.reshape(-1)])
    hm = _mlstm(qk, p3, grow, gbias.reshape(1, -1), gbias.reshape(-1, 1), dims)
    ho = _hgrn2(p3, lb.reshape(HG_H, 1, HG_DK), dims)
    return _ab_out(hm, ho, p3, ml_g, hg_g, w_out.astype(BF16), x, g1, seg_fn, n_out, min(tm, 256))


def _ssd_kernel(xf_ref, dcf_ref, drf_ref, xb_ref, dcb_ref, drb_ref, bc_ref, br_ref, ac_ref, ar_ref,
                of_ref, ob_ref, h_ref):
    @pl.when(pl.program_id(1) == 0)
    def _():
        h_ref[...] = jnp.zeros_like(h_ref)

    gw = SSD_R * SSD_P
    t_idx = lax.broadcasted_iota(jnp.int32, (CHUNK, gw), 0)
    s_idx = lax.broadcasted_iota(jnp.int32, (CHUNK, gw), 1) % SSD_P
    e_r = lax.broadcasted_iota(jnp.int32, (3 * SSD_R, gw), 0) % SSD_R
    e_c = lax.broadcasted_iota(jnp.int32, (3 * SSD_R, gw), 1) // SSD_P
    expand3 = (e_r == e_c).astype(BF16)
    low_half = lax.broadcasted_iota(jnp.int32, (CHUNK, LANES), 1) < SSD_P

    def expand(v):
        hi = v.astype(BF16).astype(F32)
        mid = (v - hi).astype(BF16).astype(F32)
        lo = v - hi - mid
        return jnp.dot(jnp.concatenate([hi, mid, lo], axis=1).astype(BF16), expand3, preferred_element_type=F32)

    for d, (x_ref, dc_ref, dr_ref, o_ref) in enumerate(((xf_ref, dcf_ref, drf_ref, of_ref),
                                                        (xb_ref, dcb_ref, drb_ref, ob_ref))):
        rev = d == 1
        mask = (s_idx >= t_idx) if rev else (s_idx <= t_idx)
        last = 0 if rev else CHUNK - 1
        hs = slice(d * SSD_H, (d + 1) * SSD_H)
        dtc = jax.nn.softplus(dc_ref[0][:, hs] + bc_ref[:, hs])
        lac = dtc * ac_ref[:, hs]
        dtr = jax.nn.softplus(dr_ref[hs, :] + br_ref[hs, :])
        lar = dtr * ar_ref[hs, :]
        cum_c = _dot_hi(_tri(rev).astype(F32), lac)
        cum_r = _dot_hi(lar, _tri(not rev).astype(F32))
        wgt = jnp.exp(cum_c[last:last + 1, :] - cum_c) * dtc
        for g in range(SSD_G):
            heads = slice(g * SSD_R, (g + 1) * SSD_R)
            cum_x = expand(cum_c[:, heads])
            wgt_x = expand(wgt[:, heads])
            ecum_x = jnp.exp(cum_x)
            rows = lambda a: jnp.concatenate([a[g * SSD_R + r:g * SSD_R + r + 1, :] for r in range(SSD_R)], axis=1)
            seg = jnp.exp(jnp.where(mask, cum_x - rows(cum_r), -jnp.inf)) * rows(dtr)
            x = jnp.concatenate([x_ref[4 * g + j] for j in range(4)], axis=1)
            bm = x_ref[4 * SSD_G + g]
            cm = x_ref[5 * SSD_G + g]
            cb = _dot_nt(cm, bm)
            cb2 = jnp.concatenate([cb, cb], axis=1)
            hst = h_ref[d, g]
            yoff = _dot(cm, hst)
            for j in range(4):
                ls = slice(j * LANES, (j + 1) * LANES)
                xp = x[:, ls]
                bd = jnp.concatenate([jnp.where(low_half, xp, 0.0), jnp.where(low_half, 0.0, xp)], axis=0)
                o_ref[4 * g + j] = _dot(cb2 * seg[:, ls], bd) + ecum_x[:, ls] * yoff[:, ls]
            h_ref[d, g] = hst * ecum_x[last:last + 1, :] + _dot_tn(bm, x * wgt_x)


def _ssd_scan(xbc, p3, dtrow, dt_b, neg_a, dims):
    b = dims[0]
    n = p3.shape[1]
    fwd, bwd, nch = _chunk_maps(dims)
    nxs = xbc.shape[0]

    def specs(cm):
        return [pl.BlockSpec((nxs, CHUNK, LANES), lambda bi, i: (0, cm(bi, i), 0)),
                pl.BlockSpec((1, CHUNK, LANES), lambda bi, i: (SSD_S_DT, cm(bi, i), 0)),
                pl.BlockSpec((None, 2 * SSD_H, CHUNK), lambda bi, i: (cm(bi, i), 0, 0))]

    vec = lambda shape: pl.BlockSpec(shape, lambda bi, i: (0, 0))
    out_sd = jax.ShapeDtypeStruct((D_INNER // LANES, n, LANES), F32)
    return pl.pallas_call(
        _ssd_kernel,
        grid=(b, nch),
        in_specs=specs(fwd) + specs(bwd) + [vec((1, 2 * SSD_H)), vec((2 * SSD_H, 1)),
                                            vec((1, 2 * SSD_H)), vec((2 * SSD_H, 1))],
        out_specs=[pl.BlockSpec((D_INNER // LANES, CHUNK, LANES), lambda bi, i: (0, fwd(bi, i), 0)),
                   pl.BlockSpec((D_INNER // LANES, CHUNK, LANES), lambda bi, i: (0, bwd(bi, i), 0))],
        out_shape=[out_sd, out_sd],
        scratch_shapes=[pltpu.VMEM((2, SSD_G, SSD_N, SSD_R * SSD_P), F32)],
        compiler_params=_cparams(("parallel", "arbitrary")),
        name="ssd_scan",
    )(xbc, p3, dtrow, xbc, p3, dtrow, dt_b.reshape(1, -1), dt_b.reshape(-1, 1),
      neg_a.reshape(1, -1), neg_a.reshape(-1, 1))


def _ssd_out_kernel(yf_ref, yb_ref, xs_ref, z_ref, dsk_ref, ng_ref, w_ref, x_ref, g1_ref, o_ref, lhs_ref, acc_ref):
    k = pl.program_id(1)

    @pl.when(k == 0)
    def _():
        acc_ref[...] = jnp.zeros_like(acc_ref)

    gw = D_INNER // SSD_G
    for gg in range(lhs_ref.shape[1] // gw):
        cat = lambda ref: jnp.concatenate([ref[4 * gg + j] for j in range(4)], axis=1)
        cs = slice(gg * gw, (gg + 1) * gw)
        y = cat(yf_ref) + cat(yb_ref) + dsk_ref[:, cs] * cat(xs_ref)
        u = y * _silu(cat(z_ref))
        u = u * lax.rsqrt(jnp.mean(u * u, axis=-1, keepdims=True) + EPS) * ng_ref[:, cs]
        lhs_ref[:, cs] = u.astype(BF16)
    acc_ref[...] += jnp.dot(lhs_ref[...], w_ref[...], preferred_element_type=F32)

    @pl.when(k == pl.num_programs(1) - 1)
    def _():
        o_ref[...] = x_ref[...] + g1_ref[...] * acc_ref[...]


def _ssd_out(yf, yb, xbc, p3, dskip, norm_g, w_out, x, g1, seg_fn, n_rows, tm):
    d = x.shape[1]
    tk = 1024
    nsl = tk // LANES
    slab = lambda: pl.BlockSpec((nsl, tm, LANES), lambda i, k: (k, i, 0))
    return pl.pallas_call(
        _ssd_out_kernel,
        grid=(n_rows // tm, D_INNER // tk),
        in_specs=[slab(), slab(), slab(), slab(),
                  pl.BlockSpec((1, tk), lambda i, k: (0, k)),
                  pl.BlockSpec((1, tk), lambda i, k: (0, k)),
                  pl.BlockSpec((tk, d), lambda i, k: (k, 0)),
                  pl.BlockSpec((tm, d), lambda i, k: (i, 0)),
                  pl.BlockSpec((None, 1, d), lambda i, k: (seg_fn(i * tm), 0, 0))],
        out_specs=pl.BlockSpec((tm, d), lambda i, k: (i, 0)),
        out_shape=jax.ShapeDtypeStruct((n_rows, d), F32),
        scratch_shapes=[pltpu.VMEM((tm, tk), BF16), pltpu.VMEM((tm, d), F32)],
        compiler_params=_cparams(("parallel", "arbitrary")),
        name="ssd_out",
    )(yf, yb, xbc, p3, dskip, norm_g.reshape(1, -1), w_out, x, g1)


def _ssd_layer(x, dims, seg_fn, tm, n_out, g_norm, sc, sh, g1, w_in, conv_w, conv_b, dt_b, a_log, d_skip,
               norm_g, w_out):
    d = x.shape[1]
    n = x.shape[0]
    p3 = _inproj(x, g_norm, sc, sh, w_in.astype(BF16), seg_fn, tm, 9 * LANES)
    nconv = conv_w.shape[1]
    xbc = _conv_slabs(p3, SSD_S_X, nconv // LANES, _conv_pack(conv_w, conv_b, jnp.ones((nconv,), F32)), dims, 16)
    dt = p3[SSD_S_DT]
    dtrow = dt.reshape(n // CHUNK, CHUNK, 2 * SSD_H).transpose(0, 2, 1)
    yf, yb = _ssd_scan(xbc, p3, dtrow, dt_b.reshape(-1), -jnp.exp(a_log.astype(F32)).reshape(-1), dims)
    dskip = jnp.repeat(d_skip.astype(F32), SSD_P).reshape(1, D_INNER)
    return _ssd_out(yf, yb, xbc, p3, dskip, norm_g, w_out.astype(BF16), x, g1, seg_fn, n_out, min(tm, 256))


def _regroup_kernel(x_ref, o_ref):
    a = x_ref.shape[0]
    for cc in range(SUBLANES):
        o_ref[cc * a:(cc + 1) * a, :] = x_ref[:, cc, :]


def _regroup(x, b, a, c, n_out):
    d = x.shape[1]
    cblk = c // SUBLANES
    return pl.pallas_call(
        _regroup_kernel,
        grid=(b, cblk),
        in_specs=[pl.BlockSpec((a, SUBLANES, d), lambda bi, j: (bi, j, 0))],
        out_specs=pl.BlockSpec((a * SUBLANES, d), lambda bi, j: (bi * cblk + j, 0)),
        out_shape=jax.ShapeDtypeStruct((n_out, d), F32),
        compiler_params=_cparams(("parallel", "parallel")),
        name="regroup",
    )(x.reshape(x.shape[0] // c, c, d))


MOE_TILE = 512


def _router_kernel(x_ref, g_ref, sc_ref, sh_ref, wt_ref, rb_ref, hn_ref, ii_ref, iw_ref, cnt_ref, carry_ref):
    @pl.when(pl.program_id(0) == 0)
    def _():
        carry_ref[...] = jnp.zeros_like(carry_ref)

    x = x_ref[...]
    tm = x.shape[0]
    y = x * lax.rsqrt(jnp.mean(x * x, axis=-1, keepdims=True) + EPS) * g_ref[...]
    hn = y * (1.0 + sc_ref[...]) + sh_ref[...]
    _store_token_rows(hn_ref, hn)
    logits = lax.dot_general(wt_ref[...], hn, (((1,), (1,)), ((), ())), precision=HI,
                             preferred_element_type=F32)
    score = jax.nn.sigmoid(logits)
    biased = score + rb_ref[...]
    rb = [biased[e:e + 1, :] for e in range(N_EXPERTS)]
    rs = [score[e:e + 1, :] for e in range(N_EXPERTS)]
    gsc = []
    for g in range(N_GROUPS):
        a, b, c, d = rb[EXPERTS_PER_GROUP * g:EXPERTS_PER_GROUP * (g + 1)]
        hi1, lo1, hi2, lo2 = jnp.maximum(a, b), jnp.minimum(a, b), jnp.maximum(c, d), jnp.minimum(c, d)
        gsc.append(jnp.maximum(hi1, hi2) + jnp.maximum(jnp.minimum(hi1, hi2), jnp.maximum(lo1, lo2)))
    best = jnp.zeros((1, tm), jnp.int32)
    bsc = gsc[0]
    for g in range(1, N_GROUPS):
        upd = gsc[g] > bsc
        best = jnp.where(upd, g, best)
        bsc = jnp.where(upd, gsc[g], bsc)

    def pick(rows, p):
        out = rows[p]
        for g in range(1, N_GROUPS):
            out = jnp.where(best == g, rows[EXPERTS_PER_GROUP * g + p], out)
        return out

    vals = [pick(rb, p) for p in range(EXPERTS_PER_GROUP)]
    scs = [pick(rs, p) for p in range(EXPERTS_PER_GROUP)]
    p1, v1, s1 = jnp.zeros((1, tm), jnp.int32), vals[0], scs[0]
    for p in range(1, EXPERTS_PER_GROUP):
        upd = vals[p] > v1
        p1, v1, s1 = jnp.where(upd, p, p1), jnp.where(upd, vals[p], v1), jnp.where(upd, scs[p], s1)
    p2 = jnp.zeros((1, tm), jnp.int32)
    v2 = jnp.full((1, tm), -jnp.inf, F32)
    s2 = jnp.zeros((1, tm), F32)
    for p in range(EXPERTS_PER_GROUP):
        upd = (p1 != p) & (vals[p] > v2)
        p2, v2, s2 = jnp.where(upd, p, p2), jnp.where(upd, vals[p], v2), jnp.where(upd, scs[p], s2)
    e1 = best * EXPERTS_PER_GROUP + p1
    e2 = best * EXPERTS_PER_GROUP + p2
    tot = s1 + s2
    eiota = lax.broadcasted_iota(jnp.int32, (N_EXPERTS, tm), 0)
    oh1 = (eiota == e1).astype(F32)
    oh2 = (eiota == e2).astype(F32)
    oh = oh1 + oh2
    before = (lax.broadcasted_iota(jnp.int32, (tm, tm), 0) < lax.broadcasted_iota(jnp.int32, (tm, tm), 1))
    cnt = _dot(oh, before.astype(BF16)) + carry_ref[:, :1]
    r1 = jnp.sum(oh1 * cnt, axis=0, keepdims=True).astype(jnp.int32)
    r2 = jnp.sum(oh2 * cnt, axis=0, keepdims=True).astype(jnp.int32)
    zi = jnp.zeros((SUBLANES - 4, tm), jnp.int32)
    ii_ref[...] = jnp.concatenate([e1, e2, r1, r2, zi], axis=0)
    iw_ref[...] = jnp.concatenate([s1 / tot, s2 / tot, jnp.zeros((SUBLANES - 2, tm), F32)], axis=0)
    carry = carry_ref[...] + jnp.sum(oh, axis=1, keepdims=True)
    carry_ref[...] = carry
    cnt_ref[...] = carry


def _router(x, n_rows, g, sc, sh, seg_fn, router_wt, router_b):
    d = x.shape[1]
    tm = MOE_TILE
    return pl.pallas_call(
        _router_kernel,
        grid=(n_rows // tm,),
        in_specs=[pl.BlockSpec((tm, d), lambda i: (i, 0)),
                  pl.BlockSpec((1, d), lambda i: (0, 0)),
                  pl.BlockSpec((None, 1, d), lambda i: (seg_fn(i * tm), 0, 0)),
                  pl.BlockSpec((None, 1, d), lambda i: (seg_fn(i * tm), 0, 0)),
                  pl.BlockSpec((N_EXPERTS, d), lambda i: (0, 0)),
                  pl.BlockSpec((N_EXPERTS, 1), lambda i: (0, 0))],
        out_specs=[pl.BlockSpec((tm, d // LANES, LANES), lambda i: (i, 0, 0)),
                   pl.BlockSpec((SUBLANES, tm), lambda i: (0, i)),
                   pl.BlockSpec((SUBLANES, tm), lambda i: (0, i)),
                   pl.BlockSpec((N_EXPERTS, LANES), lambda i: (0, 0))],
        out_shape=[jax.ShapeDtypeStruct((n_rows, d // LANES, LANES), F32),
                   jax.ShapeDtypeStruct((SUBLANES, n_rows), jnp.int32),
                   jax.ShapeDtypeStruct((SUBLANES, n_rows), F32),
                   jax.ShapeDtypeStruct((N_EXPERTS, LANES), F32)],
        scratch_shapes=[pltpu.VMEM((N_EXPERTS, LANES), F32)],
        compiler_params=_cparams(("arbitrary",)),
        name="router",
    )(x, g.reshape(1, d), sc, sh, router_wt, router_b.reshape(N_EXPERTS, 1))


TOK_SUB = D_MODEL // LANES


def _store_token_rows(ref, val):
    for s in range(TOK_SUB):
        ref[:, s, :] = val[:, s * LANES:(s + 1) * LANES]


def _row_copy(src_ref, o_ref, sem, src_row, dst_row):
    return pltpu.make_async_copy(src_ref.at[pl.ds(src_row, 1)], o_ref.at[pl.ds(dst_row, 1)], sem)


def _gather_kernel(idx_ref, src_ref, o_ref, sem):
    rows = o_ref.shape[0]

    def issue(r8, carry):
        for u in range(SUBLANES):
            r = r8 * SUBLANES + u
            _row_copy(src_ref, o_ref, sem, idx_ref[r], r).start()
        return carry

    lax.fori_loop(0, rows // SUBLANES, issue, 0)
    pltpu.make_async_copy(src_ref.at[pl.ds(0, rows)], o_ref, sem).wait()


def _gather_rows(idx, src, n_out):
    rows = MOE_TILE
    return pl.pallas_call(
        _gather_kernel,
        grid=(n_out // rows,),
        in_specs=[pl.BlockSpec((rows,), lambda i: (i,), memory_space=pltpu.SMEM),
                  pl.BlockSpec(memory_space=pl.ANY)],
        out_specs=pl.BlockSpec((rows,) + src.shape[1:], lambda i: (i, 0, 0)),
        out_shape=jax.ShapeDtypeStruct((n_out,) + src.shape[1:], src.dtype),
        scratch_shapes=[pltpu.SemaphoreType.DMA],
        compiler_params=_cparams(("arbitrary",)),
        name="gather_rows",
    )(idx, src)


def _expert_kernel(te_ref, nu_ref, xn_ref, w1_ref, w3_ref, w2_ref, o_ref, xb_ref, acc_ref, *, nf):
    i, f = pl.program_id(0), pl.program_id(1)
    compute = (i >= 1) & (i - 1 < nu_ref[0])
    part = xn_ref.shape[0] // nf

    def convert(fv):
        rows = slice(fv * part, (fv + 1) * part)
        for s in range(TOK_SUB):
            xb_ref[i % 2, rows, s * LANES:(s + 1) * LANES] = xn_ref[rows, s, :].astype(BF16)

    for fv in range(nf):
        @pl.when((f == fv) & (i == 0))
        def _(fv=fv):
            convert(fv)

        @pl.when((f == fv) & compute)
        def _(fv=fv):
            xb = xb_ref[(i - 1) % 2]
            a = _silu(_dot(xb, w1_ref[...])) * _dot(xb, w3_ref[...])
            y = _dot(a, w2_ref[...])
            convert(fv)
            if fv > 0:
                y = acc_ref[...] + y
            if fv < nf - 1:
                acc_ref[...] = y
            else:
                _store_token_rows(o_ref, y)

    @pl.when((f == nf - 1) & (i >= 1) & jnp.logical_not(compute))
    def _():
        o_ref[...] = jnp.zeros_like(o_ref)


def _experts(tile_e, n_used, xs, w1, w3, w2, layer):
    p = xs.shape[0]
    d = D_MODEL
    te, tf = MOE_TILE, 512
    nf = D_EXPERT // tf
    ntile = p // te
    tile = lambda i: jnp.maximum(i - 1, 0)
    fidx = lambda i, f, nu: jnp.where((i >= 1) & (i - 1 < nu[0]), f, nf - 1)
    grid_spec = pltpu.PrefetchScalarGridSpec(
        num_scalar_prefetch=2,
        grid=(ntile + 1, nf),
        in_specs=[pl.BlockSpec((te, TOK_SUB, LANES), lambda i, f, te_r, nu: (jnp.minimum(i, ntile - 1), 0, 0)),
                  pl.BlockSpec((None, None, d, tf),
                               lambda i, f, te_r, nu: (layer, te_r[tile(i)], 0, fidx(i, f, nu))),
                  pl.BlockSpec((None, None, d, tf),
                               lambda i, f, te_r, nu: (layer, te_r[tile(i)], 0, fidx(i, f, nu))),
                  pl.BlockSpec((None, None, tf, d),
                               lambda i, f, te_r, nu: (layer, te_r[tile(i)], fidx(i, f, nu), 0))],
        out_specs=pl.BlockSpec((te, TOK_SUB, LANES), lambda i, f, te_r, nu: (tile(i), 0, 0)),
        scratch_shapes=[pltpu.VMEM((2, te, d), BF16), pltpu.VMEM((te, d), F32)])
    return pl.pallas_call(
        functools.partial(_expert_kernel, nf=nf),
        grid_spec=grid_spec,
        out_shape=jax.ShapeDtypeStruct((p, TOK_SUB, LANES), F32),
        compiler_params=_cparams(("arbitrary", "arbitrary")),
        name="experts",
    )(tile_e, n_used, xs, w1, w3, w2)


def _combine_kernel(x_ref, y1_ref, y2_ref, w_ref, g2_ref, fg_ref, o_ref, *, final):
    w = w_ref[...]
    for s in range(TOK_SUB):
        cs = slice(s * LANES, (s + 1) * LANES)
        o_ref[:, cs] = x_ref[:, cs] + g2_ref[:, cs] * (w[:, 0:1] * y1_ref[:, s, :] + w[:, 1:2] * y2_ref[:, s, :])
    if final:
        out = o_ref[...]
        o_ref[...] = out * lax.rsqrt(jnp.mean(out * out, axis=-1, keepdims=True) + EPS) * fg_ref[...]


def _combine(x, n_rows, yg, wts, g2, seg_fn, final_g):
    d = x.shape[1]
    tm = 256
    nt = n_rows // tm
    fg = jnp.ones((1, d), F32) if final_g is None else final_g.reshape(1, d)
    return pl.pallas_call(
        functools.partial(_combine_kernel, final=final_g is not None),
        grid=(nt,),
        in_specs=[pl.BlockSpec((tm, d), lambda i: (i, 0)),
                  pl.BlockSpec((tm, TOK_SUB, LANES), lambda i: (i, 0, 0)),
                  pl.BlockSpec((tm, TOK_SUB, LANES), lambda i: (nt + i, 0, 0)),
                  pl.BlockSpec((tm, 2), lambda i: (i, 0)),
                  pl.BlockSpec((None, 1, d), lambda i: (seg_fn(i * tm), 0, 0)),
                  pl.BlockSpec((1, d), lambda i: (0, 0))],
        out_specs=pl.BlockSpec((tm, d), lambda i: (i, 0)),
        out_shape=jax.ShapeDtypeStruct((n_rows, d), F32),
        compiler_params=_cparams(("parallel",)),
        name="combine",
    )(x, yg, yg, wts, g2, fg)


def _moe(x, n_rows, seg_fn, g, sc, sh, g2, router_wt, router_b, w1, w3, w2, layer, final_g=None):
    te = MOE_TILE
    hn, ii, iw, cnt = _router(x, n_rows, g, sc, sh, seg_fn, router_wt, router_b)
    counts = cnt[:, 0].astype(jnp.int32)
    padded = (counts + te - 1) // te * te
    ends = jnp.cumsum(padded)
    base = ends - padded
    pos = jnp.concatenate([base[ii[0]] + ii[2], base[ii[1]] + ii[3]])
    p = 2 * n_rows + N_EXPERTS * te
    ntile = p // te
    n_used = ends[-1] // te
    tiles = jnp.arange(ntile, dtype=jnp.int32)
    tile_e = jnp.sum((jnp.minimum(tiles, n_used - 1)[:, None] * te >= ends[None, :]).astype(jnp.int32), axis=1)
    tok = jnp.arange(n_rows, dtype=jnp.int32)
    src = (jnp.arange(p, dtype=jnp.int32) % n_rows).at[pos].set(jnp.concatenate([tok, tok]))
    xs = _gather_rows(src, hn, p)
    ys = _experts(tile_e, n_used.reshape(1).astype(jnp.int32), xs, w1, w3, w2, layer)
    yg = _gather_rows(pos, ys, 2 * n_rows)
    return _combine(x, n_rows, yg, iw[:2].T, g2, seg_fn, final_g)


def kernel(x, c, ctx, c_ctx, ada_w, ada_b, norm1_g, norm2_g, ab_w_in, ab_conv_w, ab_conv_b, ml_ig_b, ml_fg_b,
           hg_lb, ml_norm_g, hg_norm_g, ab_w_out, ssd_w_in, ssd_conv_w, ssd_conv_b, ssd_dt_b, ssd_a_log, ssd_d,
           ssd_norm_g, ssd_w_out, router_w, router_b, moe_w1, moe_w3, moe_w2, final_g):
    b, seq, d = x.shape
    ctxl = ctx.shape[1]
    depth = ada_w.shape[0]
    dims = (b, seq, ctxl)
    nl, nc = b * seq, b * ctxl
    rows = seq // GRID_W
    tm = min(512, nc)
    seg_fn = lambda row: jnp.where(row < nl, row // seq, b)
    c8 = jnp.concatenate([c, c_ctx[None], jnp.zeros((SUBLANES - b - 1, d), F32)])
    mods = _modulation(c8, ada_w, ada_b)[:, :b + 1].reshape(depth, b + 1, 6, 1, d)
    lb_all = jnp.cumsum(jax.nn.softmax(hg_lb.astype(F32), axis=0), axis=0)
    router_wt = router_w.T
    xr = jnp.concatenate([x.reshape(nl, d), ctx.reshape(nc, d)])
    transposed = False
    for l in range(depth):
        sh1, sc1, g1, sh2, sc2, g2 = (mods[l][:, k] for k in range(6))
        keep_ctx = l < depth - 1
        n_out = nl + nc if keep_ctx else nl
        j = l // 2
        if (l % 2 == 1) != transposed:
            xt = _regroup(xr, b, GRID_W if transposed else rows, rows if transposed else GRID_W, nl)
            xr = jnp.concatenate([xt, xr[nl:]])
            transposed = not transposed
        if l % 2 == 0:
            xr = _ab_layer(xr, dims, seg_fn, tm, n_out, norm1_g[l], sc1, sh1, g1, ab_w_in[j], ab_conv_w[j],
                           ab_conv_b[j], ml_ig_b[j], ml_fg_b[j], lb_all[l], ml_norm_g[j], hg_norm_g[j], ab_w_out[j])
        else:
            xr = _ssd_layer(xr, dims, seg_fn, tm, n_out, norm1_g[l], sc1, sh1, g1, ssd_w_in[j], ssd_conv_w[j],
                            ssd_conv_b[j], ssd_dt_b[j], ssd_a_log[j], ssd_d[j], ssd_norm_g[j], ssd_w_out[j])
        xr = _moe(xr, n_out, seg_fn, norm2_g[l], sc2, sh2, g2, router_wt, router_b, moe_w1, moe_w3, moe_w2, l,
                  final_g if l == depth - 1 else None)
    if transposed:
        xr = _regroup(xr, b, GRID_W, rows, nl)
    return xr[:nl].reshape(b, seq, d)
```

```python
import functools
import math

import jax
import jax.numpy as jnp
from jax import lax
from jax.experimental import pallas as pl
from jax.experimental.pallas import tpu as pltpu

F32 = jnp.float32
BF16 = jnp.bfloat16
HI = lax.Precision.HIGHEST

D_MODEL = 2048
GRID_W = 64
EPS = 1e-6
CHUNK = 64
CONV_K = 5
ML_H, ML_DK, ML_DV = 4, 128, 256
HG_H, HG_DK, HG_DV = 8, 128, 128
ML_QK, ML_V = ML_H * ML_DK, ML_H * ML_DV
HG_K, HG_V = HG_H * HG_DK, HG_H * HG_DV
AB_OUT = ML_V + HG_V
D_INNER = 2 * D_MODEL
SSD_P, SSD_G, SSD_N = 64, 8, 128
SSD_H = D_INNER // SSD_P
SSD_R = SSD_H // SSD_G
N_EXPERTS, N_GROUPS, TOP_K, D_EXPERT = 16, 4, 2, 1024
EXPERTS_PER_GROUP = N_EXPERTS // N_GROUPS

LANES = 128
SUBLANES = 8
VMEM_LIMIT = 56 * 1024 * 1024

AB_S_Q, AB_S_K, AB_S_V, AB_S_OG, AB_S_HQ, AB_S_HF, AB_S_HI, AB_S_HG, AB_S_GATE = 0, 4, 8, 16, 24, 32, 48, 56, 64
AB_SLABS = 66
SSD_S_Z, SSD_S_X, SSD_S_B, SSD_S_C, SSD_S_DT = 0, 32, 64, 72, 80
SSD_SLABS = 81


def _cparams(sem):
    return pltpu.CompilerParams(dimension_semantics=sem, vmem_limit_bytes=VMEM_LIMIT)


def _silu(x):
    return x * jax.nn.sigmoid(x)


def _dot(a, b):
    return jnp.dot(a.astype(BF16), b.astype(BF16), preferred_element_type=F32)


def _dot_nt(a, b):
    return lax.dot_general(a.astype(BF16), b.astype(BF16), (((1,), (1,)), ((), ())),
                           preferred_element_type=F32)


def _dot_tn(a, b):
    return lax.dot_general(a.astype(BF16), b.astype(BF16), (((0,), (0,)), ((), ())),
                           preferred_element_type=F32)


def _dot_hi(a, b):
    return jnp.dot(a, b, precision=HI, preferred_element_type=F32)


def _tri(rev):
    t = lax.broadcasted_iota(jnp.int32, (CHUNK, CHUNK), 0)
    s = lax.broadcasted_iota(jnp.int32, (CHUNK, CHUNK), 1)
    return (s >= t) if rev else (s <= t)


def _mod_kernel(c_ref, w_ref, b_ref, o_ref):
    c = c_ref[...]
    o_ref[...] = _dot(_silu(c), w_ref[...]) + b_ref[...]


def _modulation(c8, ada_w, ada_b):
    depth, d, d6 = ada_w.shape
    tn = 1024
    return pl.pallas_call(
        _mod_kernel,
        grid=(depth, d6 // tn),
        in_specs=[pl.BlockSpec((SUBLANES, d), lambda l, j: (0, 0)),
                  pl.BlockSpec((None, d, tn), lambda l, j: (l, 0, j)),
                  pl.BlockSpec((None, 1, tn), lambda l, j: (l, 0, j))],
        out_specs=pl.BlockSpec((None, SUBLANES, tn), lambda l, j: (l, 0, j)),
        out_shape=jax.ShapeDtypeStruct((depth, SUBLANES, d6), F32),
        compiler_params=_cparams(("parallel", "parallel")),
        name="modulation",
    )(c8, ada_w, ada_b.reshape(depth, 1, d6))


def _inproj_kernel(x_ref, g_ref, sc_ref, sh_ref, w_ref, o_ref, h_ref):
    @pl.when(pl.program_id(1) == 0)
    def _():
        x = x_ref[...]
        y = x * lax.rsqrt(jnp.mean(x * x, axis=-1, keepdims=True) + EPS) * g_ref[...]
        h_ref[...] = (y * (1.0 + sc_ref[...]) + sh_ref[...]).astype(BF16)

    acc = jnp.dot(h_ref[...], w_ref[...], preferred_element_type=F32)
    for s in range(o_ref.shape[0]):
        o_ref[s] = acc[:, s * LANES:(s + 1) * LANES]


def _inproj(x, g, sc, sh, w, seg_fn, tm, tn):
    n, d = x.shape
    ncols = w.shape[1]
    nsl = tn // LANES
    return pl.pallas_call(
        _inproj_kernel,
        grid=(n // tm, ncols // tn),
        in_specs=[pl.BlockSpec((tm, d), lambda i, j: (i, 0)),
                  pl.BlockSpec((1, d), lambda i, j: (0, 0)),
                  pl.BlockSpec((None, 1, d), lambda i, j: (seg_fn(i * tm), 0, 0)),
                  pl.BlockSpec((None, 1, d), lambda i, j: (seg_fn(i * tm), 0, 0)),
                  pl.BlockSpec((d, tn), lambda i, j: (0, j))],
        out_specs=pl.BlockSpec((nsl, tm, LANES), lambda i, j: (j, i, 0)),
        out_shape=jax.ShapeDtypeStruct((ncols // LANES, n, LANES), F32),
        scratch_shapes=[pltpu.VMEM((tm, d), BF16)],
        compiler_params=_cparams(("parallel", "arbitrary")),
        name="inproj",
    )(x, g.reshape(1, d), sc, sh, w)


def _conv_kernel(prev_ref, cur_ref, next_ref, w_ref, o_ref, *, tt, tiles_lat, n_lat_tiles, tiles_ctx):
    i = pl.program_id(1)
    in_lat = i < n_lat_tiles
    pos = jnp.where(in_lat, i % tiles_lat, (i - n_lat_tiles) % tiles_ctx)
    last = jnp.where(in_lat, tiles_lat - 1, tiles_ctx - 1)
    keep_prev = (pos != 0).astype(F32)
    keep_next = (pos != last).astype(F32)
    ext = jnp.concatenate([prev_ref[...] * keep_prev, cur_ref[...], next_ref[...] * keep_next], axis=1)
    w = w_ref[...]
    acc = jnp.zeros(cur_ref.shape, F32) + w[:, CONV_K:CONV_K + 1, :]
    for k in range(CONV_K):
        shift = (CONV_K // 2 - k) % (tt + 2 * SUBLANES)
        r = ext if shift == 0 else pltpu.roll(ext, shift, 1)
        acc = acc + r[:, SUBLANES:SUBLANES + tt, :] * w[:, k:k + 1, :]
    o_ref[...] = _silu(acc) * w[:, CONV_K + 1:CONV_K + 2, :]


def _conv_slabs(p3, slab0, nslab, wpack, dims, sb):
    b, seq, ctxl = dims
    n = p3.shape[1]
    tt = min(256, ctxl)
    t8 = tt // SUBLANES
    nblk8 = n // SUBLANES
    s0 = slab0 // sb
    kern = functools.partial(_conv_kernel, tt=tt, tiles_lat=seq // tt, n_lat_tiles=b * seq // tt,
                             tiles_ctx=ctxl // tt)
    return pl.pallas_call(
        kern,
        grid=(nslab // sb, n // tt),
        in_specs=[pl.BlockSpec((sb, SUBLANES, LANES), lambda s, i: (s0 + s, jnp.maximum(i * t8 - 1, 0), 0)),
                  pl.BlockSpec((sb, tt, LANES), lambda s, i: (s0 + s, i, 0)),
                  pl.BlockSpec((sb, SUBLANES, LANES),
                               lambda s, i: (s0 + s, jnp.minimum((i + 1) * t8, nblk8 - 1), 0)),
                  pl.BlockSpec((sb, SUBLANES, LANES), lambda s, i: (s, 0, 0))],
        out_specs=pl.BlockSpec((sb, tt, LANES), lambda s, i: (s, i, 0)),
        out_shape=jax.ShapeDtypeStruct((nslab, n, LANES), F32),
        compiler_params=_cparams(("parallel", "parallel")),
        name="conv",
    )(p3, p3, p3, wpack)


def _conv_pack(conv_w, conv_b, scale):
    c = conv_w.shape[1]
    rows = jnp.concatenate([conv_w, conv_b[None], scale[None], jnp.zeros((1, c), F32)], axis=0)
    return rows.reshape(SUBLANES, c // LANES, LANES).transpose(1, 0, 2)


def _chunk_maps(dims):
    b, seq, ctxl = dims
    ncc, nlc = ctxl // CHUNK, seq // CHUNK

    def fwd(bi, i):
        return jnp.where(i < ncc, b * nlc + bi * ncc + i, bi * nlc + (i - ncc))

    def bwd(bi, i):
        return jnp.where(i < ncc, b * nlc + bi * ncc + (ncc - 1 - i), bi * nlc + (nlc - 1 - (i - ncc)))

    return fwd, bwd, ncc + nlc


def _mlstm_kernel(qkf_ref, vf_ref, gcf_ref, grf_ref, qkb_ref, vb_ref, gcb_ref, grb_ref, bc_ref, br_ref,
                  of_ref, ob_ref, c_ref, n_ref, m_ref):
    @pl.when(pl.program_id(1) == 0)
    def _():
        c_ref[...] = jnp.zeros_like(c_ref)
        n_ref[...] = jnp.zeros_like(n_ref)
        m_ref[...] = jnp.zeros_like(m_ref)

    ng = 2 * ML_H
    for d, (qk_ref, v_ref, gc_ref, gr_ref, o_ref) in enumerate(
            ((qkf_ref, vf_ref, gcf_ref, grf_ref, of_ref), (qkb_ref, vb_ref, gcb_ref, grb_ref, ob_ref))):
        rev = d == 1
        mask = _tri(rev)
        gcol = gc_ref[0][:, :2 * ng] + bc_ref[...]
        grow = gr_ref[...] + br_ref[...]
        ic_all = gcol[:, d * ML_H:(d + 1) * ML_H]
        lfc_all = jax.nn.log_sigmoid(gcol[:, ng + d * ML_H:ng + (d + 1) * ML_H])
        ir_all = grow[d * ML_H:(d + 1) * ML_H, :]
        lfr_all = jax.nn.log_sigmoid(grow[ng + d * ML_H:ng + (d + 1) * ML_H, :])
        incl = mask.astype(F32)
        bcol_all = _dot_hi(incl, lfc_all)
        brow_all = _dot_hi(lfr_all, _tri(not rev).astype(F32))
        last = 0 if rev else CHUNK - 1
        for h in range(ML_H):
            q = qk_ref[h]
            k = qk_ref[ML_H + h]
            v = jnp.concatenate([v_ref[2 * h], v_ref[2 * h + 1]], axis=1)
            bcol, icol = bcol_all[:, h:h + 1], ic_all[:, h:h + 1]
            brow, irow = brow_all[h:h + 1, :], ir_all[h:h + 1, :]
            cst = c_ref[d, h]
            nst = n_ref[d, h]
            mprev = m_ref[d, h][:, :1]
            logd = jnp.where(mask, bcol - brow + irow, -jnp.inf)
            inter = bcol + mprev
            m_t = jnp.maximum(inter, jnp.max(logd, axis=1, keepdims=True))
            s = _dot_nt(q, k) * jnp.exp(logd - m_t)
            sc = jnp.exp(inter - m_t)
            num = _dot(s, v) + sc * _dot_nt(q, cst)
            den = jnp.sum(s, axis=1, keepdims=True) + sc * jnp.sum(q * nst, axis=1, keepdims=True)
            hout = num / jnp.maximum(jnp.abs(den), jnp.exp(-m_t))
            o_ref[2 * h] = hout[:, :LANES]
            o_ref[2 * h + 1] = hout[:, LANES:]
            b_last = bcol[last:last + 1, :]
            wlog = b_last - bcol + icol
            m_new = jnp.maximum(b_last + mprev, jnp.max(wlog, axis=0, keepdims=True))
            w = jnp.exp(wlog - m_new)
            dec = jnp.exp(b_last + mprev - m_new)
            c_ref[d, h] = dec * cst + _dot_tn(w * v, k)
            n_ref[d, h] = dec * nst + jnp.sum(w * k, axis=0, keepdims=True)
            m_ref[d, h] = jnp.broadcast_to(m_new, (1, LANES))


def _mlstm(qk, p3, grow, bcol, brow, dims):
    b = dims[0]
    n = p3.shape[1]
    fwd, bwd, nch = _chunk_maps(dims)

    def specs(cm):
        return [pl.BlockSpec((2 * ML_H, CHUNK, LANES), lambda bi, i: (0, cm(bi, i), 0)),
                pl.BlockSpec((2 * ML_H, CHUNK, LANES), lambda bi, i: (AB_S_V // (2 * ML_H), cm(bi, i), 0)),
                pl.BlockSpec((1, CHUNK, LANES), lambda bi, i: (AB_S_GATE, cm(bi, i), 0)),
                pl.BlockSpec((None, 4 * ML_H, CHUNK), lambda bi, i: (cm(bi, i), 0, 0))]

    out_sd = jax.ShapeDtypeStruct((2 * ML_H, n, LANES), F32)
    return pl.pallas_call(
        _mlstm_kernel,
        grid=(b, nch),
        in_specs=specs(fwd) + specs(bwd) + [pl.BlockSpec((1, 4 * ML_H), lambda bi, i: (0, 0)),
                                            pl.BlockSpec((4 * ML_H, 1), lambda bi, i: (0, 0))],
        out_specs=[pl.BlockSpec((2 * ML_H, CHUNK, LANES), lambda bi, i: (0, fwd(bi, i), 0)),
                   pl.BlockSpec((2 * ML_H, CHUNK, LANES), lambda bi, i: (0, bwd(bi, i), 0))],
        out_shape=[out_sd, out_sd],
        scratch_shapes=[pltpu.VMEM((2, ML_H, ML_DV, ML_DK), F32),
                        pltpu.VMEM((2, ML_H, 1, ML_DK), F32),
                        pltpu.VMEM((2, ML_H, 1, LANES), F32)],
        compiler_params=_cparams(("parallel", "arbitrary")),
        name="mlstm",
    )(qk, p3, p3, grow, qk, p3, p3, grow, bcol, brow)


def _bcast_rows(a, rows, span):
    parts = [jnp.broadcast_to(a[r:r + 1, :], (span, a.shape[1])) for r in rows]
    return parts[0] if len(parts) == 1 else jnp.concatenate(parts, axis=0)


def _hgrn2_kernel(qf_ref, ff_ref, vf_ref, qb_ref, fb_ref, vb_ref, lb_ref, of_ref, ob_ref, s_ref):
    @pl.when(pl.program_id(1) == 0)
    def _():
        s_ref[...] = jnp.zeros_like(s_ref)

    t = lax.broadcasted_iota(jnp.int32, (CHUNK, CHUNK), 0)
    s = lax.broadcasted_iota(jnp.int32, (CHUNK, CHUNK), 1)
    refs = ((qf_ref, ff_ref, vf_ref, of_ref), (qb_ref, fb_ref, vb_ref, ob_ref))
    units = [(d, h) for d in range(2) for h in range(HG_H)]
    levels = (32, 16, 8)

    def level_mask(m, rev):
        same = (t // (2 * m)) == (s // (2 * m))
        t_late = ((t // m) % 2 == 0) if rev else ((t // m) % 2 == 1)
        s_early = ((s // m) % 2 == 1) if rev else ((s // m) % 2 == 0)
        return same & t_late & s_early

    p1 = {}
    for d, h in units:
        q_ref, f_ref, v_ref, _ = refs[d]
        lb = lb_ref[h]
        f = lb + (1.0 - lb) * jax.nn.sigmoid(f_ref[h])
        lg = jnp.log(f)
        a = _dot_hi(_tri(d == 1).astype(F32), lg)
        p1[d, h] = (q_ref[h], 1.0 - f, lg, v_ref[h], a, s_ref[d, h])
    p2 = {}
    for d, h in units:
        rev = d == 1
        q, k, lg, v, a, st = p1[d, h]
        last = 0 if rev else CHUNK - 1
        qs, ks = [], []
        for m in levels:
            nb = CHUNK // (2 * m)
            aref = _bcast_rows(a, [bi * 2 * m + (m if rev else m - 1) for bi in range(nb)], 2 * m)
            qs.append((q * jnp.exp(jnp.minimum(a - aref, 0.0))).astype(BF16))
            ks.append((k * jnp.exp(jnp.minimum(aref - a, 0.0))).astype(BF16))
        aref = _bcast_rows(a - lg, [bi * SUBLANES + (SUBLANES - 1 if rev else 0) for bi in range(CHUNK // SUBLANES)],
                           SUBLANES)
        qs.append((q * jnp.exp(a - aref)).astype(BF16))
        ks.append((k * jnp.exp(aref - a)).astype(BF16))
        a_last = a[last:last + 1, :]
        p2[d, h] = (qs, ks, (q * jnp.exp(a)).astype(BF16), (k * jnp.exp(a_last - a)).astype(BF16), jnp.exp(a_last))
    p3 = {}
    for d, h in units:
        qs, ks, qdec, kdec, edec = p2[d, h]
        st = p1[d, h][5]
        p3[d, h] = ([_dot_nt(qq, kk) for qq, kk in zip(qs, ks)], _dot_nt(qdec, st))
    for d, h in units:
        rev = d == 1
        prods, qst = p3[d, h]
        scores = jnp.zeros((CHUNK, CHUNK), F32)
        for m, pr in zip(levels, prods[:-1]):
            scores = scores + jnp.where(level_mask(m, rev), pr, 0.0)
        diag = ((t // SUBLANES) == (s // SUBLANES)) & _tri(rev)
        scores = scores + jnp.where(diag, prods[-1], 0.0)
        refs[d][3][h] = _dot(scores, p1[d, h][3]) + qst
    for d, h in units:
        v, st = p1[d, h][3], p1[d, h][5]
        kdec, edec = p2[d, h][3], p2[d, h][4]
        s_ref[d, h] = st * edec + _dot_tn(v, kdec)


def _hgrn2(p3, lb, dims):
    b = dims[0]
    n = p3.shape[1]
    fwd, bwd, nch = _chunk_maps(dims)

    def specs(cm, d):
        return [pl.BlockSpec((HG_H, CHUNK, LANES), lambda bi, i: (AB_S_HQ // HG_H, cm(bi, i), 0)),
                pl.BlockSpec((HG_H, CHUNK, LANES), lambda bi, i: (AB_S_HF // HG_H + d, cm(bi, i), 0)),
                pl.BlockSpec((HG_H, CHUNK, LANES), lambda bi, i: (AB_S_HI // HG_H, cm(bi, i), 0))]

    out_sd = jax.ShapeDtypeStruct((HG_H, n, LANES), F32)
    return pl.pallas_call(
        _hgrn2_kernel,
        grid=(b, nch),
        in_specs=specs(fwd, 0) + specs(bwd, 1) + [pl.BlockSpec((HG_H, 1, LANES), lambda bi, i: (0, 0, 0))],
        out_specs=[pl.BlockSpec((HG_H, CHUNK, LANES), lambda bi, i: (0, fwd(bi, i), 0)),
                   pl.BlockSpec((HG_H, CHUNK, LANES), lambda bi, i: (0, bwd(bi, i), 0))],
        out_shape=[out_sd, out_sd],
        scratch_shapes=[pltpu.VMEM((2, HG_H, HG_DV, HG_DK), F32)],
        compiler_params=_cparams(("parallel", "arbitrary")),
        name="hgrn2",
    )(p3, p3, p3, p3, p3, p3, lb)


def _ab_out_kernel(mf_ref, mb_ref, hf_ref, hb_ref, og_ref, hg_ref, mlg_ref, hgg_ref, w_ref, x_ref, g1_ref,
                   o_ref, lhs_ref):
    for h in range(ML_H):
        hs = jnp.concatenate([mf_ref[2 * h] + mb_ref[2 * h], mf_ref[2 * h + 1] + mb_ref[2 * h + 1]], axis=1)
        r = hs * lax.rsqrt(jnp.mean(hs * hs, axis=-1, keepdims=True) + EPS)
        og = jnp.concatenate([og_ref[2 * h], og_ref[2 * h + 1]], axis=1)
        y = jax.nn.sigmoid(og) * (r * mlg_ref[:, h * ML_DV:(h + 1) * ML_DV])
        lhs_ref[:, h * ML_DV:(h + 1) * ML_DV] = y.astype(BF16)
    for h in range(HG_H):
        hs = hf_ref[h] + hb_ref[h]
        r = hs * lax.rsqrt(jnp.mean(hs * hs, axis=-1, keepdims=True) + EPS)
        y = _silu(hg_ref[h]) * (r * hgg_ref[:, h * HG_DV:(h + 1) * HG_DV])
        lhs_ref[:, ML_V + h * HG_DV:ML_V + (h + 1) * HG_DV] = y.astype(BF16)
    acc = jnp.dot(lhs_ref[...], w_ref[...], preferred_element_type=F32)
    o_ref[...] = x_ref[...] + g1_ref[...] * acc


def _ab_out(hm, ho, p3, ml_g, hg_g, w_out, x, g1, seg_fn, n, tm):
    d = x.shape[1]
    slab8 = lambda idx: pl.BlockSpec((SUBLANES, tm, LANES), lambda i: (idx, i, 0))
    return pl.pallas_call(
        _ab_out_kernel,
        grid=(n // tm,),
        in_specs=[slab8(0), slab8(0), slab8(0), slab8(0), slab8(AB_S_OG // SUBLANES), slab8(AB_S_HG // SUBLANES),
                  pl.BlockSpec((1, ML_V), lambda i: (0, 0)),
                  pl.BlockSpec((1, HG_V), lambda i: (0, 0)),
                  pl.BlockSpec((AB_OUT, d), lambda i: (0, 0)),
                  pl.BlockSpec((tm, d), lambda i: (i, 0)),
                  pl.BlockSpec((None, 1, d), lambda i: (seg_fn(i * tm), 0, 0))],
        out_specs=pl.BlockSpec((tm, d), lambda i: (i, 0)),
        out_shape=jax.ShapeDtypeStruct((n, d), F32),
        scratch_shapes=[pltpu.VMEM((tm, AB_OUT), BF16)],
        compiler_params=_cparams(("parallel",)),
        name="ab_out",
    )(hm[0], hm[1], ho[0], ho[1], p3, p3, ml_g.reshape(1, ML_V), hg_g.reshape(1, HG_V), w_out, x, g1)


def _ab_weight(w_in):
    d = w_in.shape[0]
    o = [0, 2 * ML_QK, 2 * ML_QK + ML_V, 2 * ML_QK + 2 * ML_V]
    g0 = o[3]
    h0 = g0 + 4 * ML_H
    pad = AB_SLABS * LANES - (w_in.shape[1])
    return jnp.concatenate([w_in[:, :g0], w_in[:, h0:], w_in[:, g0:h0], jnp.zeros((d, pad), w_in.dtype)],
                           axis=1).astype(BF16)


def _ab_layer(x, dims, seg_fn, tm, n_out, g_norm, sc, sh, g1, w_in, conv_w, conv_b, ig_b, fg_b, lb, ml_g, hg_g,
              w_out):
    p3 = _inproj(x, g_norm, sc, sh, _ab_weight(w_in), seg_fn, tm, 22 * LANES)
    kscale = jnp.concatenate([jnp.ones((ML_QK,), F32), jnp.full((ML_QK,), ML_DK ** -0.5, F32)])
    qk = _conv_slabs(p3, AB_S_Q, 2 * ML_H, _conv_pack(conv_w, conv_b, kscale), dims, 2 * ML_H)
    n = x.shape[0]
    gates = p3[AB_S_GATE, :, :4 * ML_H]
    grow = gates.reshape(n // CHUNK, CHUNK, 4 * ML_H).transpose(0, 2, 1)
    gbias = jnp.concatenate([ig_b.reshape(-1), fg_b.reshape(-1)])
    hm = _mlstm(qk, p3, grow, gbias.reshape(1, -1), gbias.reshape(-1, 1), dims)
    ho = _hgrn2(p3, lb.reshape(HG_H, 1, HG_DK), dims)
    return _ab_out(hm, ho, p3, ml_g, hg_g, w_out.astype(BF16), x, g1, seg_fn, n_out, min(tm, 256))


def _ssd_kernel(xf_ref, dcf_ref, drf_ref, xb_ref, dcb_ref, drb_ref, bc_ref, br_ref, ac_ref, ar_ref,
                of_ref, ob_ref, h_ref):
    @pl.when(pl.program_id(1) == 0)
    def _():
        h_ref[...] = jnp.zeros_like(h_ref)

    gw = SSD_R * SSD_P
    t_idx = lax.broadcasted_iota(jnp.int32, (CHUNK, gw), 0)
    s_idx = lax.broadcasted_iota(jnp.int32, (CHUNK, gw), 1) % SSD_P
    e_r = lax.broadcasted_iota(jnp.int32, (3 * SSD_R, gw), 0) % SSD_R
    e_c = lax.broadcasted_iota(jnp.int32, (3 * SSD_R, gw), 1) // SSD_P
    expand3 = (e_r == e_c).astype(BF16)
    low_half = lax.broadcasted_iota(jnp.int32, (CHUNK, LANES), 1) < SSD_P

    def expand(v):
        hi = v.astype(BF16).astype(F32)
        mid = (v - hi).astype(BF16).astype(F32)
        lo = v - hi - mid
        return jnp.dot(jnp.concatenate([hi, mid, lo], axis=1).astype(BF16), expand3, preferred_element_type=F32)

    refs = ((xf_ref, dcf_ref, drf_ref, of_ref), (xb_ref, dcb_ref, drb_ref, ob_ref))
    units = [(d, g) for d in range(2) for g in range(SSD_G)]
    gate = []
    for d, (x_ref, dc_ref, dr_ref, o_ref) in enumerate(refs):
        rev = d == 1
        mask = (s_idx >= t_idx) if rev else (s_idx <= t_idx)
        last = 0 if rev else CHUNK - 1
        hs = slice(d * SSD_H, (d + 1) * SSD_H)
        dtc = jax.nn.softplus(dc_ref[0][:, hs] + bc_ref[:, hs])
        lac = dtc * ac_ref[:, hs]
        dtr = jax.nn.softplus(dr_ref[hs, :] + br_ref[hs, :])
        lar = dtr * ar_ref[hs, :]
        cum_c = _dot_hi(_tri(rev).astype(F32), lac)
        cum_r = _dot_hi(lar, _tri(not rev).astype(F32))
        wgt = jnp.exp(cum_c[last:last + 1, :] - cum_c) * dtc
        gate.append((mask, last, cum_c, cum_r, dtr, wgt))
    p1 = {}
    for d, g in units:
        x_ref = refs[d][0]
        mask, last, cum_c, cum_r, dtr, wgt = gate[d]
        heads = slice(g * SSD_R, (g + 1) * SSD_R)
        x = jnp.concatenate([x_ref[4 * g + j] for j in range(4)], axis=1)
        bm = x_ref[4 * SSD_G + g]
        cm = x_ref[5 * SSD_G + g]
        hst = h_ref[d, g]
        p1[d, g] = (x, bm, hst, expand(cum_c[:, heads]), expand(wgt[:, heads]), _dot_nt(cm, bm), _dot(cm, hst))
    p2 = {}
    for d, g in units:
        mask, last, cum_c, cum_r, dtr, wgt = gate[d]
        x, bm, hst, cum_x, wgt_x, cb, yoff = p1[d, g]
        rows = lambda a: jnp.concatenate([a[g * SSD_R + r:g * SSD_R + r + 1, :] for r in range(SSD_R)], axis=1)
        seg = jnp.exp(jnp.where(mask, cum_x - rows(cum_r), -jnp.inf)) * rows(dtr)
        cb2 = jnp.concatenate([cb, cb], axis=1)
        ms = [(cb2 * seg[:, j * LANES:(j + 1) * LANES]).astype(BF16) for j in range(4)]
        p2[d, g] = (ms, jnp.exp(cum_x), (x * wgt_x).astype(BF16))
    for d, g in units:
        o_ref = refs[d][3]
        x, yoff = p1[d, g][0], p1[d, g][6]
        ms, ecum_x, xw = p2[d, g]
        for j in range(4):
            ls = slice(j * LANES, (j + 1) * LANES)
            xp = x[:, ls]
            bd = jnp.concatenate([jnp.where(low_half, xp, 0.0), jnp.where(low_half, 0.0, xp)], axis=0)
            o_ref[4 * g + j] = _dot(ms[j], bd) + ecum_x[:, ls] * yoff[:, ls]
    for d, g in units:
        last = gate[d][1]
        bm, hst = p1[d, g][1], p1[d, g][2]
        ms, ecum_x, xw = p2[d, g]
        h_ref[d, g] = hst * ecum_x[last:last + 1, :] + _dot_tn(bm, xw)


def _ssd_scan(xbc, p3, dtrow, dt_b, neg_a, dims):
    b = dims[0]
    n = p3.shape[1]
    fwd, bwd, nch = _chunk_maps(dims)
    nxs = xbc.shape[0]

    def specs(cm):
        return [pl.BlockSpec((nxs, CHUNK, LANES), lambda bi, i: (0, cm(bi, i), 0)),
                pl.BlockSpec((1, CHUNK, LANES), lambda bi, i: (SSD_S_DT, cm(bi, i), 0)),
                pl.BlockSpec((None, 2 * SSD_H, CHUNK), lambda bi, i: (cm(bi, i), 0, 0))]

    vec = lambda shape: pl.BlockSpec(shape, lambda bi, i: (0, 0))
    out_sd = jax.ShapeDtypeStruct((D_INNER // LANES, n, LANES), F32)
    return pl.pallas_call(
        _ssd_kernel,
        grid=(b, nch),
        in_specs=specs(fwd) + specs(bwd) + [vec((1, 2 * SSD_H)), vec((2 * SSD_H, 1)),
                                            vec((1, 2 * SSD_H)), vec((2 * SSD_H, 1))],
        out_specs=[pl.BlockSpec((D_INNER // LANES, CHUNK, LANES), lambda bi, i: (0, fwd(bi, i), 0)),
                   pl.BlockSpec((D_INNER // LANES, CHUNK, LANES), lambda bi, i: (0, bwd(bi, i), 0))],
        out_shape=[out_sd, out_sd],
        scratch_shapes=[pltpu.VMEM((2, SSD_G, SSD_N, SSD_R * SSD_P), F32)],
        compiler_params=_cparams(("parallel", "arbitrary")),
        name="ssd_scan",
    )(xbc, p3, dtrow, xbc, p3, dtrow, dt_b.reshape(1, -1), dt_b.reshape(-1, 1),
      neg_a.reshape(1, -1), neg_a.reshape(-1, 1))


def _ssd_out_kernel(yf_ref, yb_ref, xs_ref, z_ref, dsk_ref, ng_ref, w_ref, x_ref, g1_ref, o_ref, lhs_ref, acc_ref):
    k = pl.program_id(1)

    @pl.when(k == 0)
    def _():
        acc_ref[...] = jnp.zeros_like(acc_ref)

    gw = D_INNER // SSD_G
    for gg in range(lhs_ref.shape[1] // gw):
        cat = lambda ref: jnp.concatenate([ref[4 * gg + j] for j in range(4)], axis=1)
        cs = slice(gg * gw, (gg + 1) * gw)
        y = cat(yf_ref) + cat(yb_ref) + dsk_ref[:, cs] * cat(xs_ref)
        u = y * _silu(cat(z_ref))
        u = u * lax.rsqrt(jnp.mean(u * u, axis=-1, keepdims=True) + EPS) * ng_ref[:, cs]
        lhs_ref[:, cs] = u.astype(BF16)
    acc_ref[...] += jnp.dot(lhs_ref[...], w_ref[...], preferred_element_type=F32)

    @pl.when(k == pl.num_programs(1) - 1)
    def _():
        o_ref[...] = x_ref[...] + g1_ref[...] * acc_ref[...]


def _ssd_out(yf, yb, xbc, p3, dskip, norm_g, w_out, x, g1, seg_fn, n_rows, tm):
    d = x.shape[1]
    tk = 1024
    nsl = tk // LANES
    slab = lambda: pl.BlockSpec((nsl, tm, LANES), lambda i, k: (k, i, 0))
    return pl.pallas_call(
        _ssd_out_kernel,
        grid=(n_rows // tm, D_INNER // tk),
        in_specs=[slab(), slab(), slab(), slab(),
                  pl.BlockSpec((1, tk), lambda i, k: (0, k)),
                  pl.BlockSpec((1, tk), lambda i, k: (0, k)),
                  pl.BlockSpec((tk, d), lambda i, k: (k, 0)),
                  pl.BlockSpec((tm, d), lambda i, k: (i, 0)),
                  pl.BlockSpec((None, 1, d), lambda i, k: (seg_fn(i * tm), 0, 0))],
        out_specs=pl.BlockSpec((tm, d), lambda i, k: (i, 0)),
        out_shape=jax.ShapeDtypeStruct((n_rows, d), F32),
        scratch_shapes=[pltpu.VMEM((tm, tk), BF16), pltpu.VMEM((tm, d), F32)],
        compiler_params=_cparams(("parallel", "arbitrary")),
        name="ssd_out",
    )(yf, yb, xbc, p3, dskip, norm_g.reshape(1, -1), w_out, x, g1)


def _ssd_layer(x, dims, seg_fn, tm, n_out, g_norm, sc, sh, g1, w_in, conv_w, conv_b, dt_b, a_log, d_skip,
               norm_g, w_out):
    d = x.shape[1]
    n = x.shape[0]
    p3 = _inproj(x, g_norm, sc, sh, w_in.astype(BF16), seg_fn, tm, 9 * LANES)
    nconv = conv_w.shape[1]
    xbc = _conv_slabs(p3, SSD_S_X, nconv // LANES, _conv_pack(conv_w, conv_b, jnp.ones((nconv,), F32)), dims, 16)
    dt = p3[SSD_S_DT]
    dtrow = dt.reshape(n // CHUNK, CHUNK, 2 * SSD_H).transpose(0, 2, 1)
    yf, yb = _ssd_scan(xbc, p3, dtrow, dt_b.reshape(-1), -jnp.exp(a_log.astype(F32)).reshape(-1), dims)
    dskip = jnp.repeat(d_skip.astype(F32), SSD_P).reshape(1, D_INNER)
    return _ssd_out(yf, yb, xbc, p3, dskip, norm_g, w_out.astype(BF16), x, g1, seg_fn, n_out, tm)


def _regroup_kernel(x_ref, o_ref):
    a = x_ref.shape[0]
    for cc in range(SUBLANES):
        o_ref[cc * a:(cc + 1) * a, :] = x_ref[:, cc, :]


def _regroup(x, b, a, c, n_out):
    d = x.shape[1]
    cblk = c // SUBLANES
    return pl.pallas_call(
        _regroup_kernel,
        grid=(b, cblk),
        in_specs=[pl.BlockSpec((a, SUBLANES, d), lambda bi, j: (bi, j, 0))],
        out_specs=pl.BlockSpec((a * SUBLANES, d), lambda bi, j: (bi * cblk + j, 0)),
        out_shape=jax.ShapeDtypeStruct((n_out, d), F32),
        compiler_params=_cparams(("parallel", "parallel")),
        name="regroup",
    )(x.reshape(x.shape[0] // c, c, d))


MOE_TILE = 512


def _router_kernel(x_ref, g_ref, sc_ref, sh_ref, wt_ref, rb_ref, hn_ref, ii_ref, iw_ref, cnt_ref, carry_ref):
    @pl.when(pl.program_id(0) == 0)
    def _():
        carry_ref[...] = jnp.zeros_like(carry_ref)

    x = x_ref[...]
    tm = x.shape[0]
    y = x * lax.rsqrt(jnp.mean(x * x, axis=-1, keepdims=True) + EPS) * g_ref[...]
    hn = y * (1.0 + sc_ref[...]) + sh_ref[...]
    _store_token_rows(hn_ref, hn)
    logits = lax.dot_general(wt_ref[...], hn, (((1,), (1,)), ((), ())), precision=HI,
                             preferred_element_type=F32)
    score = jax.nn.sigmoid(logits)
    biased = score + rb_ref[...]
    rb = [biased[e:e + 1, :] for e in range(N_EXPERTS)]
    rs = [score[e:e + 1, :] for e in range(N_EXPERTS)]
    gsc = []
    for g in range(N_GROUPS):
        a, b, c, d = rb[EXPERTS_PER_GROUP * g:EXPERTS_PER_GROUP * (g + 1)]
        hi1, lo1, hi2, lo2 = jnp.maximum(a, b), jnp.minimum(a, b), jnp.maximum(c, d), jnp.minimum(c, d)
        gsc.append(jnp.maximum(hi1, hi2) + jnp.maximum(jnp.minimum(hi1, hi2), jnp.maximum(lo1, lo2)))
    best = jnp.zeros((1, tm), jnp.int32)
    bsc = gsc[0]
    for g in range(1, N_GROUPS):
        upd = gsc[g] > bsc
        best = jnp.where(upd, g, best)
        bsc = jnp.where(upd, gsc[g], bsc)

    def pick(rows, p):
        out = rows[p]
        for g in range(1, N_GROUPS):
            out = jnp.where(best == g, rows[EXPERTS_PER_GROUP * g + p], out)
        return out

    vals = [pick(rb, p) for p in range(EXPERTS_PER_GROUP)]
    scs = [pick(rs, p) for p in range(EXPERTS_PER_GROUP)]
    p1, v1, s1 = jnp.zeros((1, tm), jnp.int32), vals[0], scs[0]
    for p in range(1, EXPERTS_PER_GROUP):
        upd = vals[p] > v1
        p1, v1, s1 = jnp.where(upd, p, p1), jnp.where(upd, vals[p], v1), jnp.where(upd, scs[p], s1)
    p2 = jnp.zeros((1, tm), jnp.int32)
    v2 = jnp.full((1, tm), -jnp.inf, F32)
    s2 = jnp.zeros((1, tm), F32)
    for p in range(EXPERTS_PER_GROUP):
        upd = (p1 != p) & (vals[p] > v2)
        p2, v2, s2 = jnp.where(upd, p, p2), jnp.where(upd, vals[p], v2), jnp.where(upd, scs[p], s2)
    e1 = best * EXPERTS_PER_GROUP + p1
    e2 = best * EXPERTS_PER_GROUP + p2
    tot = s1 + s2
    eiota = lax.broadcasted_iota(jnp.int32, (N_EXPERTS, tm), 0)
    oh1 = (eiota == e1).astype(F32)
    oh2 = (eiota == e2).astype(F32)
    oh = oh1 + oh2
    before = (lax.broadcasted_iota(jnp.int32, (tm, tm), 0) < lax.broadcasted_iota(jnp.int32, (tm, tm), 1))
    cnt = _dot(oh, before.astype(BF16)) + carry_ref[:, :1]
    r1 = jnp.sum(oh1 * cnt, axis=0, keepdims=True).astype(jnp.int32)
    r2 = jnp.sum(oh2 * cnt, axis=0, keepdims=True).astype(jnp.int32)
    zi = jnp.zeros((SUBLANES - 4, tm), jnp.int32)
    ii_ref[...] = jnp.concatenate([e1, e2, r1, r2, zi], axis=0)
    iw_ref[...] = jnp.concatenate([s1 / tot, s2 / tot, jnp.zeros((SUBLANES - 2, tm), F32)], axis=0)
    carry = carry_ref[...] + jnp.sum(oh, axis=1, keepdims=True)
    carry_ref[...] = carry
    cnt_ref[...] = carry


def _router(x, n_rows, g, sc, sh, seg_fn, router_wt, router_b):
    d = x.shape[1]
    tm = MOE_TILE
    return pl.pallas_call(
        _router_kernel,
        grid=(n_rows // tm,),
        in_specs=[pl.BlockSpec((tm, d), lambda i: (i, 0)),
                  pl.BlockSpec((1, d), lambda i: (0, 0)),
                  pl.BlockSpec((None, 1, d), lambda i: (seg_fn(i * tm), 0, 0)),
                  pl.BlockSpec((None, 1, d), lambda i: (seg_fn(i * tm), 0, 0)),
                  pl.BlockSpec((N_EXPERTS, d), lambda i: (0, 0)),
                  pl.BlockSpec((N_EXPERTS, 1), lambda i: (0, 0))],
        out_specs=[pl.BlockSpec((tm, d // LANES, LANES), lambda i: (i, 0, 0)),
                   pl.BlockSpec((SUBLANES, tm), lambda i: (0, i)),
                   pl.BlockSpec((SUBLANES, tm), lambda i: (0, i)),
                   pl.BlockSpec((N_EXPERTS, LANES), lambda i: (0, 0))],
        out_shape=[jax.ShapeDtypeStruct((n_rows, d // LANES, LANES), F32),
                   jax.ShapeDtypeStruct((SUBLANES, n_rows), jnp.int32),
                   jax.ShapeDtypeStruct((SUBLANES, n_rows), F32),
                   jax.ShapeDtypeStruct((N_EXPERTS, LANES), F32)],
        scratch_shapes=[pltpu.VMEM((N_EXPERTS, LANES), F32)],
        compiler_params=_cparams(("arbitrary",)),
        name="router",
    )(x, g.reshape(1, d), sc, sh, router_wt, router_b.reshape(N_EXPERTS, 1))


TOK_SUB = D_MODEL // LANES


def _store_token_rows(ref, val):
    for s in range(TOK_SUB):
        ref[:, s, :] = val[:, s * LANES:(s + 1) * LANES]


def _row_copy(src_ref, o_ref, sem, src_row, dst_row):
    return pltpu.make_async_copy(src_ref.at[pl.ds(src_row, 1)], o_ref.at[pl.ds(dst_row, 1)], sem)


def _gather_kernel(idx_ref, src_ref, o_ref, sem):
    rows = o_ref.shape[0]

    def issue(r8, carry):
        for u in range(SUBLANES):
            r = r8 * SUBLANES + u
            _row_copy(src_ref, o_ref, sem, idx_ref[r], r).start()
        return carry

    lax.fori_loop(0, rows // SUBLANES, issue, 0)
    pltpu.make_async_copy(src_ref.at[pl.ds(0, rows)], o_ref, sem).wait()


def _gather_rows(idx, src, n_out):
    rows = MOE_TILE
    return pl.pallas_call(
        _gather_kernel,
        grid=(n_out // rows,),
        in_specs=[pl.BlockSpec((rows,), lambda i: (i,), memory_space=pltpu.SMEM),
                  pl.BlockSpec(memory_space=pl.ANY)],
        out_specs=pl.BlockSpec((rows,) + src.shape[1:], lambda i: (i, 0, 0)),
        out_shape=jax.ShapeDtypeStruct((n_out,) + src.shape[1:], src.dtype),
        scratch_shapes=[pltpu.SemaphoreType.DMA],
        compiler_params=_cparams(("arbitrary",)),
        name="gather_rows",
    )(idx, src)


def _expert_kernel(te_ref, nu_ref, xn_ref, w1_ref, w3_ref, w2_ref, o_ref, xb_ref, acc_ref, *, nf):
    i, f = pl.program_id(0), pl.program_id(1)
    compute = (i >= 1) & (i - 1 < nu_ref[0])
    part = xn_ref.shape[0] // nf

    def convert(fv):
        rows = slice(fv * part, (fv + 1) * part)
        for s in range(TOK_SUB):
            xb_ref[i % 2, rows, s * LANES:(s + 1) * LANES] = xn_ref[rows, s, :].astype(BF16)

    for fv in range(nf):
        @pl.when((f == fv) & (i == 0))
        def _(fv=fv):
            convert(fv)

        @pl.when((f == fv) & compute)
        def _(fv=fv):
            xb = xb_ref[(i - 1) % 2]
            a = _silu(_dot(xb, w1_ref[...])) * _dot(xb, w3_ref[...])
            y = _dot(a, w2_ref[...])
            convert(fv)
            if fv > 0:
                y = acc_ref[...] + y
            if fv < nf - 1:
                acc_ref[...] = y
            else:
                _store_token_rows(o_ref, y)

    @pl.when((f == nf - 1) & (i >= 1) & jnp.logical_not(compute))
    def _():
        o_ref[...] = jnp.zeros_like(o_ref)


def _experts(tile_e, n_used, xs, w1, w3, w2, layer):
    p = xs.shape[0]
    d = D_MODEL
    te, tf = MOE_TILE, 512
    nf = D_EXPERT // tf
    ntile = p // te
    tile = lambda i: jnp.maximum(i - 1, 0)
    fidx = lambda i, f, nu: jnp.where((i >= 1) & (i - 1 < nu[0]), f, nf - 1)
    grid_spec = pltpu.PrefetchScalarGridSpec(
        num_scalar_prefetch=2,
        grid=(ntile + 1, nf),
        in_specs=[pl.BlockSpec((te, TOK_SUB, LANES), lambda i, f, te_r, nu: (jnp.minimum(i, ntile - 1), 0, 0)),
                  pl.BlockSpec((None, None, d, tf),
                               lambda i, f, te_r, nu: (layer, te_r[tile(i)], 0, fidx(i, f, nu))),
                  pl.BlockSpec((None, None, d, tf),
                               lambda i, f, te_r, nu: (layer, te_r[tile(i)], 0, fidx(i, f, nu))),
                  pl.BlockSpec((None, None, tf, d),
                               lambda i, f, te_r, nu: (layer, te_r[tile(i)], fidx(i, f, nu), 0))],
        out_specs=pl.BlockSpec((te, TOK_SUB, LANES), lambda i, f, te_r, nu: (tile(i), 0, 0)),
        scratch_shapes=[pltpu.VMEM((2, te, d), BF16), pltpu.VMEM((te, d), F32)])
    return pl.pallas_call(
        functools.partial(_expert_kernel, nf=nf),
        grid_spec=grid_spec,
        out_shape=jax.ShapeDtypeStruct((p, TOK_SUB, LANES), F32),
        compiler_params=_cparams(("arbitrary", "arbitrary")),
        name="experts",
    )(tile_e, n_used, xs, w1, w3, w2)


def _combine_kernel(x_ref, y1_ref, y2_ref, w_ref, g2_ref, fg_ref, o_ref, *, final):
    w = w_ref[...]
    for s in range(TOK_SUB):
        cs = slice(s * LANES, (s + 1) * LANES)
        o_ref[:, cs] = x_ref[:, cs] + g2_ref[:, cs] * (w[:, 0:1] * y1_ref[:, s, :] + w[:, 1:2] * y2_ref[:, s, :])
    if final:
        out = o_ref[...]
        o_ref[...] = out * lax.rsqrt(jnp.mean(out * out, axis=-1, keepdims=True) + EPS) * fg_ref[...]


def _combine(x, n_rows, yg, wts, g2, seg_fn, final_g):
    d = x.shape[1]
    tm = 256
    nt = n_rows // tm
    fg = jnp.ones((1, d), F32) if final_g is None else final_g.reshape(1, d)
    return pl.pallas_call(
        functools.partial(_combine_kernel, final=final_g is not None),
        grid=(nt,),
        in_specs=[pl.BlockSpec((tm, d), lambda i: (i, 0)),
                  pl.BlockSpec((tm, TOK_SUB, LANES), lambda i: (i, 0, 0)),
                  pl.BlockSpec((tm, TOK_SUB, LANES), lambda i: (nt + i, 0, 0)),
                  pl.BlockSpec((tm, 2), lambda i: (i, 0)),
                  pl.BlockSpec((None, 1, d), lambda i: (seg_fn(i * tm), 0, 0)),
                  pl.BlockSpec((1, d), lambda i: (0, 0))],
        out_specs=pl.BlockSpec((tm, d), lambda i: (i, 0)),
        out_shape=jax.ShapeDtypeStruct((n_rows, d), F32),
        compiler_params=_cparams(("parallel",)),
        name="combine",
    )(x, yg, yg, wts, g2, fg)


def _moe(x, n_rows, seg_fn, g, sc, sh, g2, router_wt, router_b, w1, w3, w2, layer, final_g=None):
    te = MOE_TILE
    hn, ii, iw, cnt = _router(x, n_rows, g, sc, sh, seg_fn, router_wt, router_b)
    counts = cnt[:, 0].astype(jnp.int32)
    padded = (counts + te - 1) // te * te
    ends = jnp.cumsum(padded)
    base = ends - padded
    pos = jnp.concatenate([base[ii[0]] + ii[2], base[ii[1]] + ii[3]])
    p = 2 * n_rows + N_EXPERTS * te
    ntile = p // te
    n_used = ends[-1] // te
    tiles = jnp.arange(ntile, dtype=jnp.int32)
    tile_e = jnp.sum((jnp.minimum(tiles, n_used - 1)[:, None] * te >= ends[None, :]).astype(jnp.int32), axis=1)
    tok = jnp.arange(n_rows, dtype=jnp.int32)
    src = (jnp.arange(p, dtype=jnp.int32) % n_rows).at[pos].set(jnp.concatenate([tok, tok]))
    xs = _gather_rows(src, hn, p)
    ys = _experts(tile_e, n_used.reshape(1).astype(jnp.int32), xs, w1, w3, w2, layer)
    yg = _gather_rows(pos, ys, 2 * n_rows)
    return _combine(x, n_rows, yg, iw[:2].T, g2, seg_fn, final_g)


def kernel(x, c, ctx, c_ctx, ada_w, ada_b, norm1_g, norm2_g, ab_w_in, ab_conv_w, ab_conv_b, ml_ig_b, ml_fg_b,
           hg_lb, ml_norm_g, hg_norm_g, ab_w_out, ssd_w_in, ssd_conv_w, ssd_conv_b, ssd_dt_b, ssd_a_log, ssd_d,
           ssd_norm_g, ssd_w_out, router_w, router_b, moe_w1, moe_w3, moe_w2, final_g):
    b, seq, d = x.shape
    ctxl = ctx.shape[1]
    depth = ada_w.shape[0]
    dims = (b, seq, ctxl)
    nl, nc = b * seq, b * ctxl
    rows = seq // GRID_W
    tm = min(512, nc)
    seg_fn = lambda row: jnp.where(row < nl, row // seq, b)
    c8 = jnp.concatenate([c, c_ctx[None], jnp.zeros((SUBLANES - b - 1, d), F32)])
    mods = _modulation(c8, ada_w, ada_b)[:, :b + 1].reshape(depth, b + 1, 6, 1, d)
    lb_all = jnp.cumsum(jax.nn.softmax(hg_lb.astype(F32), axis=0), axis=0)
    router_wt = router_w.T
    xr = jnp.concatenate([x.reshape(nl, d), ctx.reshape(nc, d)])
    transposed = False
    for l in range(depth):
        sh1, sc1, g1, sh2, sc2, g2 = (mods[l][:, k] for k in range(6))
        keep_ctx = l < depth - 1
        n_out = nl + nc if keep_ctx else nl
        j = l // 2
        if (l % 2 == 1) != transposed:
            xt = _regroup(xr, b, GRID_W if transposed else rows, rows if transposed else GRID_W, nl)
            xr = jnp.concatenate([xt, xr[nl:]])
            transposed = not transposed
        if l % 2 == 0:
            xr = _ab_layer(xr, dims, seg_fn, tm, n_out, norm1_g[l], sc1, sh1, g1, ab_w_in[j], ab_conv_w[j],
                           ab_conv_b[j], ml_ig_b[j], ml_fg_b[j], lb_all[l], ml_norm_g[j], hg_norm_g[j], ab_w_out[j])
        else:
            xr = _ssd_layer(xr, dims, seg_fn, tm, n_out, norm1_g[l], sc1, sh1, g1, ssd_w_in[j], ssd_conv_w[j],
                            ssd_conv_b[j], ssd_dt_b[j], ssd_a_log[j], ssd_d[j], ssd_norm_g[j], ssd_w_out[j])
        xr = _moe(xr, n_out, seg_fn, norm2_g[l], sc2, sh2, g2, router_wt, router_b, moe_w1, moe_w3, moe_w2, l,
                  final_g if l == depth - 1 else None)
    if transposed:
        xr = _regroup(xr, b, GRID_W, rows, nl)
    return xr[:nl].reshape(b, seq, d)
```

```python
import functools
import math

import jax
import jax.numpy as jnp
from jax import lax
from jax.experimental import pallas as pl
from jax.experimental.pallas import tpu as pltpu

F32 = jnp.float32
BF16 = jnp.bfloat16
HI = lax.Precision.HIGHEST

D_MODEL = 2048
GRID_W = 64
EPS = 1e-6
CHUNK = 64
CONV_K = 5
ML_H, ML_DK, ML_DV = 4, 128, 256
HG_H, HG_DK, HG_DV = 8, 128, 128
ML_QK, ML_V = ML_H * ML_DK, ML_H * ML_DV
HG_K, HG_V = HG_H * HG_DK, HG_H * HG_DV
AB_OUT = ML_V + HG_V
D_INNER = 2 * D_MODEL
SSD_P, SSD_G, SSD_N = 64, 8, 128
SSD_H = D_INNER // SSD_P
SSD_R = SSD_H // SSD_G
N_EXPERTS, N_GROUPS, TOP_K, D_EXPERT = 16, 4, 2, 1024
EXPERTS_PER_GROUP = N_EXPERTS // N_GROUPS

LANES = 128
SUBLANES = 8
VMEM_LIMIT = 56 * 1024 * 1024

AB_S_Q, AB_S_K, AB_S_V, AB_S_OG, AB_S_HQ, AB_S_HF, AB_S_HI, AB_S_HG, AB_S_GATE = 0, 4, 8, 16, 24, 32, 48, 56, 64
AB_SLABS = 66
SSD_S_Z, SSD_S_X, SSD_S_B, SSD_S_C, SSD_S_DT = 0, 32, 64, 72, 80
SSD_SLABS = 81


def _cparams(sem):
    return pltpu.CompilerParams(dimension_semantics=sem, vmem_limit_bytes=VMEM_LIMIT)


def _silu(x):
    return x * jax.nn.sigmoid(x)


def _dot(a, b):
    return jnp.dot(a.astype(BF16), b.astype(BF16), preferred_element_type=F32)


def _dot_nt(a, b):
    return lax.dot_general(a.astype(BF16), b.astype(BF16), (((1,), (1,)), ((), ())),
                           preferred_element_type=F32)


def _dot_tn(a, b):
    return lax.dot_general(a.astype(BF16), b.astype(BF16), (((0,), (0,)), ((), ())),
                           preferred_element_type=F32)


def _dot_hi(a, b):
    return jnp.dot(a, b, precision=HI, preferred_element_type=F32)


def _tri(rev):
    t = lax.broadcasted_iota(jnp.int32, (CHUNK, CHUNK), 0)
    s = lax.broadcasted_iota(jnp.int32, (CHUNK, CHUNK), 1)
    return (s >= t) if rev else (s <= t)


def _mod_kernel(c_ref, w_ref, b_ref, o_ref):
    c = c_ref[...]
    o_ref[...] = _dot(_silu(c), w_ref[...]) + b_ref[...]


def _modulation(c8, ada_w, ada_b):
    depth, d, d6 = ada_w.shape
    tn = 1024
    return pl.pallas_call(
        _mod_kernel,
        grid=(depth, d6 // tn),
        in_specs=[pl.BlockSpec((SUBLANES, d), lambda l, j: (0, 0)),
                  pl.BlockSpec((None, d, tn), lambda l, j: (l, 0, j)),
                  pl.BlockSpec((None, 1, tn), lambda l, j: (l, 0, j))],
        out_specs=pl.BlockSpec((None, SUBLANES, tn), lambda l, j: (l, 0, j)),
        out_shape=jax.ShapeDtypeStruct((depth, SUBLANES, d6), F32),
        compiler_params=_cparams(("parallel", "parallel")),
        name="modulation",
    )(c8, ada_w, ada_b.reshape(depth, 1, d6))


def _inproj_kernel(x_ref, g_ref, sc_ref, sh_ref, w_ref, o_ref, h_ref):
    @pl.when(pl.program_id(1) == 0)
    def _():
        x = x_ref[...]
        y = x * lax.rsqrt(jnp.mean(x * x, axis=-1, keepdims=True) + EPS) * g_ref[...]
        h_ref[...] = (y * (1.0 + sc_ref[...]) + sh_ref[...]).astype(BF16)

    acc = jnp.dot(h_ref[...], w_ref[...], preferred_element_type=F32)
    for s in range(o_ref.shape[0]):
        o_ref[s] = acc[:, s * LANES:(s + 1) * LANES]


def _inproj(x, g, sc, sh, w, seg_fn, tm, tn):
    n, d = x.shape
    ncols = w.shape[1]
    nsl = tn // LANES
    return pl.pallas_call(
        _inproj_kernel,
        grid=(n // tm, ncols // tn),
        in_specs=[pl.BlockSpec((tm, d), lambda i, j: (i, 0)),
                  pl.BlockSpec((1, d), lambda i, j: (0, 0)),
                  pl.BlockSpec((None, 1, d), lambda i, j: (seg_fn(i * tm), 0, 0)),
                  pl.BlockSpec((None, 1, d), lambda i, j: (seg_fn(i * tm), 0, 0)),
                  pl.BlockSpec((d, tn), lambda i, j: (0, j))],
        out_specs=pl.BlockSpec((nsl, tm, LANES), lambda i, j: (j, i, 0)),
        out_shape=jax.ShapeDtypeStruct((ncols // LANES, n, LANES), F32),
        scratch_shapes=[pltpu.VMEM((tm, d), BF16)],
        compiler_params=_cparams(("parallel", "arbitrary")),
        name="inproj",
    )(x, g.reshape(1, d), sc, sh, w)


def _conv_kernel(prev_ref, cur_ref, next_ref, w_ref, o_ref, *, tt, tiles_lat, n_lat_tiles, tiles_ctx):
    i = pl.program_id(1)
    in_lat = i < n_lat_tiles
    pos = jnp.where(in_lat, i % tiles_lat, (i - n_lat_tiles) % tiles_ctx)
    last = jnp.where(in_lat, tiles_lat - 1, tiles_ctx - 1)
    keep_prev = (pos != 0).astype(F32)
    keep_next = (pos != last).astype(F32)
    ext = jnp.concatenate([prev_ref[...] * keep_prev, cur_ref[...], next_ref[...] * keep_next], axis=1)
    w = w_ref[...]
    acc = jnp.zeros(cur_ref.shape, F32) + w[:, CONV_K:CONV_K + 1, :]
    for k in range(CONV_K):
        shift = (CONV_K // 2 - k) % (tt + 2 * SUBLANES)
        r = ext if shift == 0 else pltpu.roll(ext, shift, 1)
        acc = acc + r[:, SUBLANES:SUBLANES + tt, :] * w[:, k:k + 1, :]
    o_ref[...] = (_silu(acc) * w[:, CONV_K + 1:CONV_K + 2, :]).astype(o_ref.dtype)


def _conv_slabs(p3, slab0, nslab, wpack, dims, sb):
    b, seq, ctxl = dims
    n = b * (seq + ctxl)
    tt = min(256, ctxl)
    t8 = tt // SUBLANES
    nblk8 = n // SUBLANES
    s0 = slab0 // sb
    kern = functools.partial(_conv_kernel, tt=tt, tiles_lat=seq // tt, n_lat_tiles=b * seq // tt,
                             tiles_ctx=ctxl // tt)
    return pl.pallas_call(
        kern,
        grid=(nslab // sb, n // tt),
        in_specs=[pl.BlockSpec((sb, SUBLANES, LANES), lambda s, i: (s0 + s, jnp.maximum(i * t8 - 1, 0), 0)),
                  pl.BlockSpec((sb, tt, LANES), lambda s, i: (s0 + s, i, 0)),
                  pl.BlockSpec((sb, SUBLANES, LANES),
                               lambda s, i: (s0 + s, jnp.minimum((i + 1) * t8, nblk8 - 1), 0)),
                  pl.BlockSpec((sb, SUBLANES, LANES), lambda s, i: (s, 0, 0))],
        out_specs=pl.BlockSpec((sb, tt, LANES), lambda s, i: (s, i, 0)),
        out_shape=jax.ShapeDtypeStruct((nslab, n, LANES), BF16),
        compiler_params=_cparams(("parallel", "parallel")),
        name="conv",
    )(p3, p3, p3, wpack)


def _conv_pack(conv_w, conv_b, scale):
    c = conv_w.shape[1]
    rows = jnp.concatenate([conv_w, conv_b[None], scale[None], jnp.zeros((1, c), F32)], axis=0)
    return rows.reshape(SUBLANES, c // LANES, LANES).transpose(1, 0, 2)


def _chunk_maps(dims):
    b, seq, ctxl = dims
    ncc, nlc = ctxl // CHUNK, seq // CHUNK

    def fwd(bi, i):
        return jnp.where(i < ncc, b * nlc + bi * ncc + i, bi * nlc + (i - ncc))

    def bwd(bi, i):
        return jnp.where(i < ncc, b * nlc + bi * ncc + (ncc - 1 - i), bi * nlc + (nlc - 1 - (i - ncc)))

    return fwd, bwd, ncc + nlc


def _mlstm_kernel(qkf_ref, vf_ref, gcf_ref, grf_ref, qkb_ref, vb_ref, gcb_ref, grb_ref, bc_ref, br_ref,
                  of_ref, ob_ref, c_ref, n_ref, m_ref):
    @pl.when(pl.program_id(1) == 0)
    def _():
        c_ref[...] = jnp.zeros_like(c_ref)
        n_ref[...] = jnp.zeros_like(n_ref)
        m_ref[...] = jnp.zeros_like(m_ref)

    ng = 2 * ML_H
    for d, (qk_ref, v_ref, gc_ref, gr_ref, o_ref) in enumerate(
            ((qkf_ref, vf_ref, gcf_ref, grf_ref, of_ref), (qkb_ref, vb_ref, gcb_ref, grb_ref, ob_ref))):
        rev = d == 1
        mask = _tri(rev)
        gcol = gc_ref[0][:, :2 * ng] + bc_ref[...]
        grow = gr_ref[...] + br_ref[...]
        ic_all = gcol[:, d * ML_H:(d + 1) * ML_H]
        lfc_all = jax.nn.log_sigmoid(gcol[:, ng + d * ML_H:ng + (d + 1) * ML_H])
        ir_all = grow[d * ML_H:(d + 1) * ML_H, :]
        lfr_all = jax.nn.log_sigmoid(grow[ng + d * ML_H:ng + (d + 1) * ML_H, :])
        incl = mask.astype(F32)
        bcol_all = _dot_hi(incl, lfc_all)
        brow_all = _dot_hi(lfr_all, _tri(not rev).astype(F32))
        last = 0 if rev else CHUNK - 1
        for h in range(ML_H):
            q = qk_ref[h]
            k = qk_ref[ML_H + h]
            v = jnp.concatenate([v_ref[2 * h], v_ref[2 * h + 1]], axis=1)
            bcol, icol = bcol_all[:, h:h + 1], ic_all[:, h:h + 1]
            brow, irow = brow_all[h:h + 1, :], ir_all[h:h + 1, :]
            cst = c_ref[d, h]
            nst = n_ref[d, h]
            mprev = m_ref[d, h][:, :1]
            logd = jnp.where(mask, bcol - brow + irow, -jnp.inf)
            inter = bcol + mprev
            m_t = jnp.maximum(inter, jnp.max(logd, axis=1, keepdims=True))
            s = _dot_nt(q, k) * jnp.exp(logd - m_t)
            sc = jnp.exp(inter - m_t)
            num = _dot(s, v) + sc * _dot_nt(q, cst)
            den = jnp.sum(s, axis=1, keepdims=True) + sc * jnp.sum(q * nst, axis=1, keepdims=True)
            hout = num / jnp.maximum(jnp.abs(den), jnp.exp(-m_t))
            o_ref[2 * h] = hout[:, :LANES]
            o_ref[2 * h + 1] = hout[:, LANES:]
            b_last = bcol[last:last + 1, :]
            wlog = b_last - bcol + icol
            m_new = jnp.maximum(b_last + mprev, jnp.max(wlog, axis=0, keepdims=True))
            w = jnp.exp(wlog - m_new)
            dec = jnp.exp(b_last + mprev - m_new)
            c_ref[d, h] = dec * cst + _dot_tn(w * v, k)
            n_ref[d, h] = dec * nst + jnp.sum(w * k, axis=0, keepdims=True)
            m_ref[d, h] = jnp.broadcast_to(m_new, (1, LANES))


def _mlstm(qk, p3, grow, bcol, brow, dims):
    b = dims[0]
    n = qk.shape[1]
    fwd, bwd, nch = _chunk_maps(dims)

    def specs(cm):
        return [pl.BlockSpec((2 * ML_H, CHUNK, LANES), lambda bi, i: (0, cm(bi, i), 0)),
                pl.BlockSpec((2 * ML_H, CHUNK, LANES), lambda bi, i: (AB_S_V // (2 * ML_H), cm(bi, i), 0)),
                pl.BlockSpec((1, CHUNK, LANES), lambda bi, i: (AB_S_GATE, cm(bi, i), 0)),
                pl.BlockSpec((None, 4 * ML_H, CHUNK), lambda bi, i: (cm(bi, i), 0, 0))]

    out_sd = jax.ShapeDtypeStruct((2 * ML_H, n, LANES), F32)
    return pl.pallas_call(
        _mlstm_kernel,
        grid=(b, nch),
        in_specs=specs(fwd) + specs(bwd) + [pl.BlockSpec((1, 4 * ML_H), lambda bi, i: (0, 0)),
                                            pl.BlockSpec((4 * ML_H, 1), lambda bi, i: (0, 0))],
        out_specs=[pl.BlockSpec((2 * ML_H, CHUNK, LANES), lambda bi, i: (0, fwd(bi, i), 0)),
                   pl.BlockSpec((2 * ML_H, CHUNK, LANES), lambda bi, i: (0, bwd(bi, i), 0))],
        out_shape=[out_sd, out_sd],
        scratch_shapes=[pltpu.VMEM((2, ML_H, ML_DV, ML_DK), F32),
                        pltpu.VMEM((2, ML_H, 1, ML_DK), F32),
                        pltpu.VMEM((2, ML_H, 1, LANES), F32)],
        compiler_params=_cparams(("parallel", "arbitrary")),
        name="mlstm",
    )(qk, p3, p3, grow, qk, p3, p3, grow, bcol, brow)


def _bcast_rows(a, rows, span):
    parts = [jnp.broadcast_to(a[r:r + 1, :], (span, a.shape[1])) for r in rows]
    return parts[0] if len(parts) == 1 else jnp.concatenate(parts, axis=0)


def _hgrn2_kernel(qf_ref, ff_ref, vf_ref, qb_ref, fb_ref, vb_ref, lb_ref, of_ref, ob_ref, s_ref):
    @pl.when(pl.program_id(1) == 0)
    def _():
        s_ref[...] = jnp.zeros_like(s_ref)

    t = lax.broadcasted_iota(jnp.int32, (CHUNK, CHUNK), 0)
    s = lax.broadcasted_iota(jnp.int32, (CHUNK, CHUNK), 1)
    refs = ((qf_ref, ff_ref, vf_ref, of_ref), (qb_ref, fb_ref, vb_ref, ob_ref))
    units = [(d, h) for d in range(2) for h in range(HG_H)]
    levels = (32, 16, 8)

    def level_mask(m, rev):
        same = (t // (2 * m)) == (s // (2 * m))
        t_late = ((t // m) % 2 == 0) if rev else ((t // m) % 2 == 1)
        s_early = ((s // m) % 2 == 1) if rev else ((s // m) % 2 == 0)
        return same & t_late & s_early

    p1 = {}
    for d, h in units:
        q_ref, f_ref, v_ref, _ = refs[d]
        lb = lb_ref[h]
        f = lb + (1.0 - lb) * jax.nn.sigmoid(f_ref[h])
        lg = jnp.log(f)
        a = _dot_hi(_tri(d == 1).astype(F32), lg)
        p1[d, h] = (q_ref[h], 1.0 - f, lg, v_ref[h], a, s_ref[d, h])
    p2 = {}
    for d, h in units:
        rev = d == 1
        q, k, lg, v, a, st = p1[d, h]
        last = 0 if rev else CHUNK - 1
        qs, ks = [], []
        for m in levels:
            nb = CHUNK // (2 * m)
            aref = _bcast_rows(a, [bi * 2 * m + (m if rev else m - 1) for bi in range(nb)], 2 * m)
            qs.append((q * jnp.exp(jnp.minimum(a - aref, 0.0))).astype(BF16))
            ks.append((k * jnp.exp(jnp.minimum(aref - a, 0.0))).astype(BF16))
        aref = _bcast_rows(a - lg, [bi * SUBLANES + (SUBLANES - 1 if rev else 0) for bi in range(CHUNK // SUBLANES)],
                           SUBLANES)
        qs.append((q * jnp.exp(a - aref)).astype(BF16))
        ks.append((k * jnp.exp(aref - a)).astype(BF16))
        a_last = a[last:last + 1, :]
        p2[d, h] = (qs, ks, (q * jnp.exp(a)).astype(BF16), (k * jnp.exp(a_last - a)).astype(BF16), jnp.exp(a_last))
    p3 = {}
    for d, h in units:
        qs, ks, qdec, kdec, edec = p2[d, h]
        st = p1[d, h][5]
        p3[d, h] = ([_dot_nt(qq, kk) for qq, kk in zip(qs, ks)], _dot_nt(qdec, st))
    for d, h in units:
        rev = d == 1
        prods, qst = p3[d, h]
        scores = jnp.zeros((CHUNK, CHUNK), F32)
        for m, pr in zip(levels, prods[:-1]):
            scores = scores + jnp.where(level_mask(m, rev), pr, 0.0)
        diag = ((t // SUBLANES) == (s // SUBLANES)) & _tri(rev)
        scores = scores + jnp.where(diag, prods[-1], 0.0)
        refs[d][3][h] = _dot(scores, p1[d, h][3]) + qst
    for d, h in units:
        v, st = p1[d, h][3], p1[d, h][5]
        kdec, edec = p2[d, h][3], p2[d, h][4]
        s_ref[d, h] = st * edec + _dot_tn(v, kdec)


def _hgrn2(p3, lb, dims):
    b = dims[0]
    n = b * (dims[1] + dims[2])
    fwd, bwd, nch = _chunk_maps(dims)

    def specs(cm, d):
        return [pl.BlockSpec((HG_H, CHUNK, LANES), lambda bi, i: (AB_S_HQ // HG_H, cm(bi, i), 0)),
                pl.BlockSpec((HG_H, CHUNK, LANES), lambda bi, i: (AB_S_HF // HG_H + d, cm(bi, i), 0)),
                pl.BlockSpec((HG_H, CHUNK, LANES), lambda bi, i: (AB_S_HI // HG_H, cm(bi, i), 0))]

    out_sd = jax.ShapeDtypeStruct((HG_H, n, LANES), F32)
    return pl.pallas_call(
        _hgrn2_kernel,
        grid=(b, nch),
        in_specs=specs(fwd, 0) + specs(bwd, 1) + [pl.BlockSpec((HG_H, 1, LANES), lambda bi, i: (0, 0, 0))],
        out_specs=[pl.BlockSpec((HG_H, CHUNK, LANES), lambda bi, i: (0, fwd(bi, i), 0)),
                   pl.BlockSpec((HG_H, CHUNK, LANES), lambda bi, i: (0, bwd(bi, i), 0))],
        out_shape=[out_sd, out_sd],
        scratch_shapes=[pltpu.VMEM((2, HG_H, HG_DV, HG_DK), F32)],
        compiler_params=_cparams(("parallel", "arbitrary")),
        name="hgrn2",
    )(p3, p3, p3, p3, p3, p3, lb)


def _ab_out_kernel(mf_ref, mb_ref, hf_ref, hb_ref, og_ref, hg_ref, mlg_ref, hgg_ref, w_ref, x_ref, g1_ref,
                   o_ref, lhs_ref):
    for h in range(ML_H):
        hs = jnp.concatenate([mf_ref[2 * h] + mb_ref[2 * h], mf_ref[2 * h + 1] + mb_ref[2 * h + 1]], axis=1)
        r = hs * lax.rsqrt(jnp.mean(hs * hs, axis=-1, keepdims=True) + EPS)
        og = jnp.concatenate([og_ref[2 * h], og_ref[2 * h + 1]], axis=1)
        y = jax.nn.sigmoid(og) * (r * mlg_ref[:, h * ML_DV:(h + 1) * ML_DV])
        lhs_ref[:, h * ML_DV:(h + 1) * ML_DV] = y.astype(BF16)
    for h in range(HG_H):
        hs = hf_ref[h] + hb_ref[h]
        r = hs * lax.rsqrt(jnp.mean(hs * hs, axis=-1, keepdims=True) + EPS)
        y = _silu(hg_ref[h]) * (r * hgg_ref[:, h * HG_DV:(h + 1) * HG_DV])
        lhs_ref[:, ML_V + h * HG_DV:ML_V + (h + 1) * HG_DV] = y.astype(BF16)
    acc = jnp.dot(lhs_ref[...], w_ref[...], preferred_element_type=F32)
    o_ref[...] = x_ref[...] + g1_ref[...] * acc


def _ab_out(hm, ho, p3, ml_g, hg_g, w_out, x, g1, seg_fn, n, tm):
    d = x.shape[1]
    slab8 = lambda idx: pl.BlockSpec((SUBLANES, tm, LANES), lambda i: (idx, i, 0))
    return pl.pallas_call(
        _ab_out_kernel,
        grid=(n // tm,),
        in_specs=[slab8(0), slab8(0), slab8(0), slab8(0), slab8(AB_S_OG // SUBLANES), slab8(AB_S_HG // SUBLANES),
                  pl.BlockSpec((1, ML_V), lambda i: (0, 0)),
                  pl.BlockSpec((1, HG_V), lambda i: (0, 0)),
                  pl.BlockSpec((AB_OUT, d), lambda i: (0, 0)),
                  pl.BlockSpec((tm, d), lambda i: (i, 0)),
                  pl.BlockSpec((None, 1, d), lambda i: (seg_fn(i * tm), 0, 0))],
        out_specs=pl.BlockSpec((tm, d), lambda i: (i, 0)),
        out_shape=jax.ShapeDtypeStruct((n, d), F32),
        scratch_shapes=[pltpu.VMEM((tm, AB_OUT), BF16)],
        compiler_params=_cparams(("parallel",)),
        name="ab_out",
    )(hm[0], hm[1], ho[0], ho[1], p3, p3, ml_g.reshape(1, ML_V), hg_g.reshape(1, HG_V), w_out, x, g1)


def _ab_weight(w_in):
    d = w_in.shape[0]
    o = [0, 2 * ML_QK, 2 * ML_QK + ML_V, 2 * ML_QK + 2 * ML_V]
    g0 = o[3]
    h0 = g0 + 4 * ML_H
    pad = AB_SLABS * LANES - (w_in.shape[1])
    return jnp.concatenate([w_in[:, :g0], w_in[:, h0:], w_in[:, g0:h0], jnp.zeros((d, pad), w_in.dtype)],
                           axis=1).astype(BF16)


def _ab_layer(x, dims, seg_fn, tm, n_out, g_norm, sc, sh, g1, w_in, conv_w, conv_b, ig_b, fg_b, lb, ml_g, hg_g,
              w_out):
    b, seq, ctxl = dims
    n = b * (seq + ctxl)
    p3 = _inproj(x, g_norm, sc, sh, _ab_weight(w_in), seg_fn, min(2 * tm, seq), 11 * LANES)
    kscale = jnp.concatenate([jnp.ones((ML_QK,), F32), jnp.full((ML_QK,), ML_DK ** -0.5, F32)])
    qk = _conv_slabs(p3, AB_S_Q, 2 * ML_H, _conv_pack(conv_w, conv_b, kscale), dims, 2 * ML_H)
    gates = p3[AB_S_GATE, :n, :4 * ML_H]
    grow = gates.reshape(n // CHUNK, CHUNK, 4 * ML_H).transpose(0, 2, 1)
    gbias = jnp.concatenate([ig_b.reshape(-1), fg_b.reshape(-1)])
    hm = _mlstm(qk, p3, grow, gbias.reshape(1, -1), gbias.reshape(-1, 1), dims)
    ho = _hgrn2(p3, lb.reshape(HG_H, 1, HG_DK), dims)
    return _ab_out(hm, ho, p3, ml_g, hg_g, w_out.astype(BF16), x, g1, seg_fn, n_out, min(tm, 256))


def _ssd_kernel(xf_ref, dcf_ref, drf_ref, xb_ref, dcb_ref, drb_ref, bc_ref, br_ref, ac_ref, ar_ref,
                of_ref, ob_ref, h_ref):
    @pl.when(pl.program_id(1) == 0)
    def _():
        h_ref[...] = jnp.zeros_like(h_ref)

    gw = SSD_R * SSD_P
    t_idx = lax.broadcasted_iota(jnp.int32, (CHUNK, gw), 0)
    s_idx = lax.broadcasted_iota(jnp.int32, (CHUNK, gw), 1) % SSD_P
    e_r = lax.broadcasted_iota(jnp.int32, (3 * SSD_R, gw), 0) % SSD_R
    e_c = lax.broadcasted_iota(jnp.int32, (3 * SSD_R, gw), 1) // SSD_P
    expand3 = (e_r == e_c).astype(BF16)
    low_half = lax.broadcasted_iota(jnp.int32, (CHUNK, LANES), 1) < SSD_P

    def expand(v):
        hi = v.astype(BF16).astype(F32)
        mid = (v - hi).astype(BF16).astype(F32)
        lo = v - hi - mid
        return jnp.dot(jnp.concatenate([hi, mid, lo], axis=1).astype(BF16), expand3, preferred_element_type=F32)

    refs = ((xf_ref, dcf_ref, drf_ref, of_ref), (xb_ref, dcb_ref, drb_ref, ob_ref))
    units = [(d, g) for d in range(2) for g in range(SSD_G)]
    gate = []
    for d, (x_ref, dc_ref, dr_ref, o_ref) in enumerate(refs):
        rev = d == 1
        mask = (s_idx >= t_idx) if rev else (s_idx <= t_idx)
        last = 0 if rev else CHUNK - 1
        hs = slice(d * SSD_H, (d + 1) * SSD_H)
        dtc = jax.nn.softplus(dc_ref[0][:, hs] + bc_ref[:, hs])
        lac = dtc * ac_ref[:, hs]
        dtr = jax.nn.softplus(dr_ref[hs, :] + br_ref[hs, :])
        lar = dtr * ar_ref[hs, :]
        cum_c = _dot_hi(_tri(rev).astype(F32), lac)
        cum_r = _dot_hi(lar, _tri(not rev).astype(F32))
        wgt = jnp.exp(cum_c[last:last + 1, :] - cum_c) * dtc
        gate.append((mask, last, cum_c, cum_r, dtr, wgt))
    p1 = {}
    for d, g in units:
        x_ref = refs[d][0]
        mask, last, cum_c, cum_r, dtr, wgt = gate[d]
        heads = slice(g * SSD_R, (g + 1) * SSD_R)
        x = jnp.concatenate([x_ref[4 * g + j] for j in range(4)], axis=1)
        bm = x_ref[4 * SSD_G + g]
        cm = x_ref[5 * SSD_G + g]
        hst = h_ref[d, g]
        p1[d, g] = (x, bm, hst, expand(cum_c[:, heads]), expand(wgt[:, heads]), _dot_nt(cm, bm), _dot(cm, hst))
    p2 = {}
    for d, g in units:
        mask, last, cum_c, cum_r, dtr, wgt = gate[d]
        x, bm, hst, cum_x, wgt_x, cb, yoff = p1[d, g]
        rows = lambda a: jnp.concatenate([a[g * SSD_R + r:g * SSD_R + r + 1, :] for r in range(SSD_R)], axis=1)
        seg = jnp.exp(jnp.where(mask, cum_x - rows(cum_r), -jnp.inf)) * rows(dtr)
        cb2 = jnp.concatenate([cb, cb], axis=1)
        ms = [(cb2 * seg[:, j * LANES:(j + 1) * LANES]).astype(BF16) for j in range(4)]
        p2[d, g] = (ms, jnp.exp(cum_x), (x * wgt_x).astype(BF16))
    for d, g in units:
        o_ref = refs[d][3]
        x, yoff = p1[d, g][0], p1[d, g][6]
        ms, ecum_x, xw = p2[d, g]
        for j in range(4):
            ls = slice(j * LANES, (j + 1) * LANES)
            xp = x[:, ls]
            bd = jnp.concatenate([jnp.where(low_half, xp, 0.0), jnp.where(low_half, 0.0, xp)], axis=0)
            o_ref[4 * g + j] = (_dot(ms[j], bd) + ecum_x[:, ls] * yoff[:, ls]).astype(o_ref.dtype)
    for d, g in units:
        last = gate[d][1]
        bm, hst = p1[d, g][1], p1[d, g][2]
        ms, ecum_x, xw = p2[d, g]
        h_ref[d, g] = hst * ecum_x[last:last + 1, :] + _dot_tn(bm, xw)


def _ssd_scan(xbc, p3, dtrow, dt_b, neg_a, dims):
    b = dims[0]
    n = xbc.shape[1]
    fwd, bwd, nch = _chunk_maps(dims)
    nxs = xbc.shape[0]

    def specs(cm):
        return [pl.BlockSpec((nxs, CHUNK, LANES), lambda bi, i: (0, cm(bi, i), 0)),
                pl.BlockSpec((1, CHUNK, LANES), lambda bi, i: (SSD_S_DT, cm(bi, i), 0)),
                pl.BlockSpec((None, 2 * SSD_H, CHUNK), lambda bi, i: (cm(bi, i), 0, 0))]

    vec = lambda shape: pl.BlockSpec(shape, lambda bi, i: (0, 0))
    out_sd = jax.ShapeDtypeStruct((D_INNER // LANES, n, LANES), BF16)
    return pl.pallas_call(
        _ssd_kernel,
        grid=(b, nch),
        in_specs=specs(fwd) + specs(bwd) + [vec((1, 2 * SSD_H)), vec((2 * SSD_H, 1)),
                                            vec((1, 2 * SSD_H)), vec((2 * SSD_H, 1))],
        out_specs=[pl.BlockSpec((D_INNER // LANES, CHUNK, LANES), lambda bi, i: (0, fwd(bi, i), 0)),
                   pl.BlockSpec((D_INNER // LANES, CHUNK, LANES), lambda bi, i: (0, bwd(bi, i), 0))],
        out_shape=[out_sd, out_sd],
        scratch_shapes=[pltpu.VMEM((2, SSD_G, SSD_N, SSD_R * SSD_P), F32)],
        compiler_params=_cparams(("parallel", "arbitrary")),
        name="ssd_scan",
    )(xbc, p3, dtrow, xbc, p3, dtrow, dt_b.reshape(1, -1), dt_b.reshape(-1, 1),
      neg_a.reshape(1, -1), neg_a.reshape(-1, 1))


def _ssd_out_kernel(yf_ref, yb_ref, xs_ref, z_ref, dsk_ref, ng_ref, w_ref, x_ref, g1_ref, o_ref, lhs_ref, acc_ref):
    k = pl.program_id(1)

    @pl.when(k == 0)
    def _():
        acc_ref[...] = jnp.zeros_like(acc_ref)

    gw = D_INNER // SSD_G
    for gg in range(lhs_ref.shape[1] // gw):
        cat = lambda ref: jnp.concatenate([ref[4 * gg + j] for j in range(4)], axis=1).astype(F32)
        cs = slice(gg * gw, (gg + 1) * gw)
        y = cat(yf_ref) + cat(yb_ref) + dsk_ref[:, cs] * cat(xs_ref)
        u = y * _silu(cat(z_ref))
        u = u * lax.rsqrt(jnp.mean(u * u, axis=-1, keepdims=True) + EPS) * ng_ref[:, cs]
        lhs_ref[:, cs] = u.astype(BF16)
    acc_ref[...] += jnp.dot(lhs_ref[...], w_ref[...], preferred_element_type=F32)

    @pl.when(k == pl.num_programs(1) - 1)
    def _():
        o_ref[...] = x_ref[...] + g1_ref[...] * acc_ref[...]


def _ssd_out(yf, yb, xbc, p3, dskip, norm_g, w_out, x, g1, seg_fn, n_rows, tm):
    d = x.shape[1]
    tk = 1024
    nsl = tk // LANES
    slab = lambda: pl.BlockSpec((nsl, tm, LANES), lambda i, k: (k, i, 0))
    return pl.pallas_call(
        _ssd_out_kernel,
        grid=(n_rows // tm, D_INNER // tk),
        in_specs=[slab(), slab(), slab(), slab(),
                  pl.BlockSpec((1, tk), lambda i, k: (0, k)),
                  pl.BlockSpec((1, tk), lambda i, k: (0, k)),
                  pl.BlockSpec((tk, d), lambda i, k: (k, 0)),
                  pl.BlockSpec((tm, d), lambda i, k: (i, 0)),
                  pl.BlockSpec((None, 1, d), lambda i, k: (seg_fn(i * tm), 0, 0))],
        out_specs=pl.BlockSpec((tm, d), lambda i, k: (i, 0)),
        out_shape=jax.ShapeDtypeStruct((n_rows, d), F32),
        scratch_shapes=[pltpu.VMEM((tm, tk), BF16), pltpu.VMEM((tm, d), F32)],
        compiler_params=_cparams(("parallel", "arbitrary")),
        name="ssd_out",
    )(yf, yb, xbc, p3, dskip, norm_g.reshape(1, -1), w_out, x, g1)


def _ssd_layer(x, dims, seg_fn, tm, n_out, g_norm, sc, sh, g1, w_in, conv_w, conv_b, dt_b, a_log, d_skip,
               norm_g, w_out):
    b, seq, ctxl = dims
    n = b * (seq + ctxl)
    p3 = _inproj(x, g_norm, sc, sh, w_in.astype(BF16), seg_fn, min(2 * tm, seq), 9 * LANES)
    nconv = conv_w.shape[1]
    xbc = _conv_slabs(p3, SSD_S_X, nconv // LANES, _conv_pack(conv_w, conv_b, jnp.ones((nconv,), F32)), dims, 16)
    dt = p3[SSD_S_DT, :n]
    dtrow = dt.reshape(n // CHUNK, CHUNK, 2 * SSD_H).transpose(0, 2, 1)
    yf, yb = _ssd_scan(xbc, p3, dtrow, dt_b.reshape(-1), -jnp.exp(a_log.astype(F32)).reshape(-1), dims)
    dskip = jnp.repeat(d_skip.astype(F32), SSD_P).reshape(1, D_INNER)
    return _ssd_out(yf, yb, xbc, p3, dskip, norm_g, w_out.astype(BF16), x, g1, seg_fn, n_out, tm)


def _regroup_kernel(x_ref, o_ref):
    a = x_ref.shape[0]
    for cc in range(SUBLANES):
        o_ref[cc * a:(cc + 1) * a, :] = x_ref[:, cc, :]


def _regroup(x, b, a, c, n_out):
    d = x.shape[1]
    cblk = c // SUBLANES
    return pl.pallas_call(
        _regroup_kernel,
        grid=(b, cblk),
        in_specs=[pl.BlockSpec((a, SUBLANES, d), lambda bi, j: (bi, j, 0))],
        out_specs=pl.BlockSpec((a * SUBLANES, d), lambda bi, j: (bi * cblk + j, 0)),
        out_shape=jax.ShapeDtypeStruct((n_out, d), F32),
        compiler_params=_cparams(("parallel", "parallel")),
        name="regroup",
    )(x.reshape(x.shape[0] // c, c, d))


MOE_TILE = 512


def _router_kernel(x_ref, g_ref, sc_ref, sh_ref, wt_ref, rb_ref, hn_ref, ii_ref, iw_ref, cnt_ref, carry_ref):
    @pl.when(pl.program_id(0) == 0)
    def _():
        carry_ref[...] = jnp.zeros_like(carry_ref)

    x = x_ref[...]
    tm = x.shape[0]
    y = x * lax.rsqrt(jnp.mean(x * x, axis=-1, keepdims=True) + EPS) * g_ref[...]
    hn = y * (1.0 + sc_ref[...]) + sh_ref[...]
    _store_token_rows(hn_ref, hn)
    logits = lax.dot_general(wt_ref[...], hn, (((1,), (1,)), ((), ())), precision=HI,
                             preferred_element_type=F32)
    score = jax.nn.sigmoid(logits)
    biased = score + rb_ref[...]
    rb = [biased[e:e + 1, :] for e in range(N_EXPERTS)]
    rs = [score[e:e + 1, :] for e in range(N_EXPERTS)]
    gsc = []
    for g in range(N_GROUPS):
        a, b, c, d = rb[EXPERTS_PER_GROUP * g:EXPERTS_PER_GROUP * (g + 1)]
        hi1, lo1, hi2, lo2 = jnp.maximum(a, b), jnp.minimum(a, b), jnp.maximum(c, d), jnp.minimum(c, d)
        gsc.append(jnp.maximum(hi1, hi2) + jnp.maximum(jnp.minimum(hi1, hi2), jnp.maximum(lo1, lo2)))
    best = jnp.zeros((1, tm), jnp.int32)
    bsc = gsc[0]
    for g in range(1, N_GROUPS):
        upd = gsc[g] > bsc
        best = jnp.where(upd, g, best)
        bsc = jnp.where(upd, gsc[g], bsc)

    def pick(rows, p):
        out = rows[p]
        for g in range(1, N_GROUPS):
            out = jnp.where(best == g, rows[EXPERTS_PER_GROUP * g + p], out)
        return out

    vals = [pick(rb, p) for p in range(EXPERTS_PER_GROUP)]
    scs = [pick(rs, p) for p in range(EXPERTS_PER_GROUP)]
    p1, v1, s1 = jnp.zeros((1, tm), jnp.int32), vals[0], scs[0]
    for p in range(1, EXPERTS_PER_GROUP):
        upd = vals[p] > v1
        p1, v1, s1 = jnp.where(upd, p, p1), jnp.where(upd, vals[p], v1), jnp.where(upd, scs[p], s1)
    p2 = jnp.zeros((1, tm), jnp.int32)
    v2 = jnp.full((1, tm), -jnp.inf, F32)
    s2 = jnp.zeros((1, tm), F32)
    for p in range(EXPERTS_PER_GROUP):
        upd = (p1 != p) & (vals[p] > v2)
        p2, v2, s2 = jnp.where(upd, p, p2), jnp.where(upd, vals[p], v2), jnp.where(upd, scs[p], s2)
    e1 = best * EXPERTS_PER_GROUP + p1
    e2 = best * EXPERTS_PER_GROUP + p2
    tot = s1 + s2
    eiota = lax.broadcasted_iota(jnp.int32, (N_EXPERTS, tm), 0)
    oh1 = (eiota == e1).astype(F32)
    oh2 = (eiota == e2).astype(F32)
    oh = oh1 + oh2
    before = (lax.broadcasted_iota(jnp.int32, (tm, tm), 0) < lax.broadcasted_iota(jnp.int32, (tm, tm), 1))
    cnt = _dot(oh, before.astype(BF16)) + carry_ref[:, :1]
    r1 = jnp.sum(oh1 * cnt, axis=0, keepdims=True).astype(jnp.int32)
    r2 = jnp.sum(oh2 * cnt, axis=0, keepdims=True).astype(jnp.int32)
    zi = jnp.zeros((SUBLANES - 4, tm), jnp.int32)
    ii_ref[...] = jnp.concatenate([e1, e2, r1, r2, zi], axis=0)
    iw_ref[...] = jnp.concatenate([s1 / tot, s2 / tot, jnp.zeros((SUBLANES - 2, tm), F32)], axis=0)
    carry = carry_ref[...] + jnp.sum(oh, axis=1, keepdims=True)
    carry_ref[...] = carry
    cnt_ref[...] = carry


def _router(x, n_rows, g, sc, sh, seg_fn, router_wt, router_b):
    d = x.shape[1]
    tm = MOE_TILE
    return pl.pallas_call(
        _router_kernel,
        grid=(n_rows // tm,),
        in_specs=[pl.BlockSpec((tm, d), lambda i: (i, 0)),
                  pl.BlockSpec((1, d), lambda i: (0, 0)),
                  pl.BlockSpec((None, 1, d), lambda i: (seg_fn(i * tm), 0, 0)),
                  pl.BlockSpec((None, 1, d), lambda i: (seg_fn(i * tm), 0, 0)),
                  pl.BlockSpec((N_EXPERTS, d), lambda i: (0, 0)),
                  pl.BlockSpec((N_EXPERTS, 1), lambda i: (0, 0))],
        out_specs=[pl.BlockSpec((tm, d // LANES, LANES), lambda i: (i, 0, 0)),
                   pl.BlockSpec((SUBLANES, tm), lambda i: (0, i)),
                   pl.BlockSpec((SUBLANES, tm), lambda i: (0, i)),
                   pl.BlockSpec((N_EXPERTS, LANES), lambda i: (0, 0))],
        out_shape=[jax.ShapeDtypeStruct((n_rows, d // LANES, LANES), F32),
                   jax.ShapeDtypeStruct((SUBLANES, n_rows), jnp.int32),
                   jax.ShapeDtypeStruct((SUBLANES, n_rows), F32),
                   jax.ShapeDtypeStruct((N_EXPERTS, LANES), F32)],
        scratch_shapes=[pltpu.VMEM((N_EXPERTS, LANES), F32)],
        compiler_params=_cparams(("arbitrary",)),
        name="router",
    )(x, g.reshape(1, d), sc, sh, router_wt, router_b.reshape(N_EXPERTS, 1))


TOK_SUB = D_MODEL // LANES


def _store_token_rows(ref, val):
    for s in range(TOK_SUB):
        ref[:, s, :] = val[:, s * LANES:(s + 1) * LANES]


def _row_copy(src_ref, o_ref, sem, src_row, dst_row):
    return pltpu.make_async_copy(src_ref.at[pl.ds(src_row, 1)], o_ref.at[pl.ds(dst_row, 1)], sem)


def _gather_kernel(idx_ref, src_ref, o_ref, sem):
    rows = o_ref.shape[0]

    def issue(r8, carry):
        for u in range(SUBLANES):
            r = r8 * SUBLANES + u
            _row_copy(src_ref, o_ref, sem, idx_ref[r], r).start()
        return carry

    lax.fori_loop(0, rows // SUBLANES, issue, 0)
    pltpu.make_async_copy(src_ref.at[pl.ds(0, rows)], o_ref, sem).wait()


def _gather_rows(idx, src, n_out):
    rows = MOE_TILE
    return pl.pallas_call(
        _gather_kernel,
        grid=(n_out // rows,),
        in_specs=[pl.BlockSpec((rows,), lambda i: (i,), memory_space=pltpu.SMEM),
                  pl.BlockSpec(memory_space=pl.ANY)],
        out_specs=pl.BlockSpec((rows,) + src.shape[1:], lambda i: (i, 0, 0)),
        out_shape=jax.ShapeDtypeStruct((n_out,) + src.shape[1:], src.dtype),
        scratch_shapes=[pltpu.SemaphoreType.DMA],
        compiler_params=_cparams(("arbitrary",)),
        name="gather_rows",
    )(idx, src)


def _expert_kernel(te_ref, nu_ref, xn_ref, w1_ref, w3_ref, w2_ref, o_ref, xb_ref, acc_ref, *, nf):
    i, f = pl.program_id(0), pl.program_id(1)
    compute = (i >= 1) & (i - 1 < nu_ref[0])
    part = xn_ref.shape[0] // nf

    def convert(fv):
        rows = slice(fv * part, (fv + 1) * part)
        for s in range(TOK_SUB):
            xb_ref[i % 2, rows, s * LANES:(s + 1) * LANES] = xn_ref[rows, s, :].astype(BF16)

    for fv in range(nf):
        @pl.when((f == fv) & (i == 0))
        def _(fv=fv):
            convert(fv)

        @pl.when((f == fv) & compute)
        def _(fv=fv):
            xb = xb_ref[(i - 1) % 2]
            a = _silu(_dot(xb, w1_ref[...])) * _dot(xb, w3_ref[...])
            y = _dot(a, w2_ref[...])
            convert(fv)
            if fv > 0:
                y = acc_ref[...] + y
            if fv < nf - 1:
                acc_ref[...] = y
            else:
                _store_token_rows(o_ref, y)

    @pl.when((f == nf - 1) & (i >= 1) & jnp.logical_not(compute))
    def _():
        o_ref[...] = jnp.zeros_like(o_ref)


def _experts(tile_e, n_used, xs, w1, w3, w2, layer):
    p = xs.shape[0]
    d = D_MODEL
    te, tf = MOE_TILE, 512
    nf = D_EXPERT // tf
    ntile = p // te
    tile = lambda i: jnp.maximum(i - 1, 0)
    fidx = lambda i, f, nu: jnp.where((i >= 1) & (i - 1 < nu[0]), f, nf - 1)
    grid_spec = pltpu.PrefetchScalarGridSpec(
        num_scalar_prefetch=2,
        grid=(ntile + 1, nf),
        in_specs=[pl.BlockSpec((te, TOK_SUB, LANES), lambda i, f, te_r, nu: (jnp.minimum(i, ntile - 1), 0, 0)),
                  pl.BlockSpec((None, None, d, tf),
                               lambda i, f, te_r, nu: (layer, te_r[tile(i)], 0, fidx(i, f, nu))),
                  pl.BlockSpec((None, None, d, tf),
                               lambda i, f, te_r, nu: (layer, te_r[tile(i)], 0, fidx(i, f, nu))),
                  pl.BlockSpec((None, None, tf, d),
                               lambda i, f, te_r, nu: (layer, te_r[tile(i)], fidx(i, f, nu), 0))],
        out_specs=pl.BlockSpec((te, TOK_SUB, LANES), lambda i, f, te_r, nu: (tile(i), 0, 0)),
        scratch_shapes=[pltpu.VMEM((2, te, d), BF16), pltpu.VMEM((te, d), F32)])
    return pl.pallas_call(
        functools.partial(_expert_kernel, nf=nf),
        grid_spec=grid_spec,
        out_shape=jax.ShapeDtypeStruct((p, TOK_SUB, LANES), F32),
        compiler_params=_cparams(("arbitrary", "arbitrary")),
        name="experts",
    )(tile_e, n_used, xs, w1, w3, w2)


def _combine_kernel(x_ref, y1_ref, y2_ref, w_ref, g2_ref, fg_ref, o_ref, *, final):
    w = w_ref[...]
    for s in range(TOK_SUB):
        cs = slice(s * LANES, (s + 1) * LANES)
        o_ref[:, cs] = x_ref[:, cs] + g2_ref[:, cs] * (w[:, 0:1] * y1_ref[:, s, :] + w[:, 1:2] * y2_ref[:, s, :])
    if final:
        out = o_ref[...]
        o_ref[...] = out * lax.rsqrt(jnp.mean(out * out, axis=-1, keepdims=True) + EPS) * fg_ref[...]


def _combine(x, n_rows, yg, wts, g2, seg_fn, final_g):
    d = x.shape[1]
    tm = 256
    nt = n_rows // tm
    fg = jnp.ones((1, d), F32) if final_g is None else final_g.reshape(1, d)
    return pl.pallas_call(
        functools.partial(_combine_kernel, final=final_g is not None),
        grid=(nt,),
        in_specs=[pl.BlockSpec((tm, d), lambda i: (i, 0)),
                  pl.BlockSpec((tm, TOK_SUB, LANES), lambda i: (i, 0, 0)),
                  pl.BlockSpec((tm, TOK_SUB, LANES), lambda i: (nt + i, 0, 0)),
                  pl.BlockSpec((tm, 2), lambda i: (i, 0)),
                  pl.BlockSpec((None, 1, d), lambda i: (seg_fn(i * tm), 0, 0)),
                  pl.BlockSpec((1, d), lambda i: (0, 0))],
        out_specs=pl.BlockSpec((tm, d), lambda i: (i, 0)),
        out_shape=jax.ShapeDtypeStruct((n_rows, d), F32),
        compiler_params=_cparams(("parallel",)),
        name="combine",
    )(x, yg, yg, wts, g2, fg)


def _moe(x, n_rows, seg_fn, g, sc, sh, g2, router_wt, router_b, w1, w3, w2, layer, final_g=None):
    te = MOE_TILE
    hn, ii, iw, cnt = _router(x, n_rows, g, sc, sh, seg_fn, router_wt, router_b)
    counts = cnt[:, 0].astype(jnp.int32)
    padded = (counts + te - 1) // te * te
    ends = jnp.cumsum(padded)
    base = ends - padded
    pos = jnp.concatenate([base[ii[0]] + ii[2], base[ii[1]] + ii[3]])
    p = 2 * n_rows + N_EXPERTS * te
    ntile = p // te
    n_used = ends[-1] // te
    tiles = jnp.arange(ntile, dtype=jnp.int32)
    tile_e = jnp.sum((jnp.minimum(tiles, n_used - 1)[:, None] * te >= ends[None, :]).astype(jnp.int32), axis=1)
    tok = jnp.arange(n_rows, dtype=jnp.int32)
    src = (jnp.arange(p, dtype=jnp.int32) % n_rows).at[pos].set(jnp.concatenate([tok, tok]))
    xs = _gather_rows(src, hn, p)
    ys = _experts(tile_e, n_used.reshape(1).astype(jnp.int32), xs, w1, w3, w2, layer)
    yg = _gather_rows(pos, ys, 2 * n_rows)
    return _combine(x, n_rows, yg, iw[:2].T, g2, seg_fn, final_g)


def kernel(x, c, ctx, c_ctx, ada_w, ada_b, norm1_g, norm2_g, ab_w_in, ab_conv_w, ab_conv_b, ml_ig_b, ml_fg_b,
           hg_lb, ml_norm_g, hg_norm_g, ab_w_out, ssd_w_in, ssd_conv_w, ssd_conv_b, ssd_dt_b, ssd_a_log, ssd_d,
           ssd_norm_g, ssd_w_out, router_w, router_b, moe_w1, moe_w3, moe_w2, final_g):
    b, seq, d = x.shape
    ctxl = ctx.shape[1]
    depth = ada_w.shape[0]
    dims = (b, seq, ctxl)
    nl, nc = b * seq, b * ctxl
    rows = seq // GRID_W
    tm = min(512, nc)
    seg_fn = lambda row: jnp.where(row < nl, row // seq, b)
    c8 = jnp.concatenate([c, c_ctx[None], jnp.zeros((SUBLANES - b - 1, d), F32)])
    mods = _modulation(c8, ada_w, ada_b)[:, :b + 1].reshape(depth, b + 1, 6, 1, d)
    lb_all = jnp.cumsum(jax.nn.softmax(hg_lb.astype(F32), axis=0), axis=0)
    router_wt = router_w.T
    pad = jnp.zeros((-(nl + nc) % min(2 * tm, seq), d), F32)
    xr = jnp.concatenate([x.reshape(nl, d), ctx.reshape(nc, d), pad])
    transposed = False
    for l in range(depth):
        sh1, sc1, g1, sh2, sc2, g2 = (mods[l][:, k] for k in range(6))
        keep_ctx = l < depth - 1
        n_out = nl + nc if keep_ctx else nl
        j = l // 2
        if (l % 2 == 1) != transposed:
            xt = _regroup(xr, b, GRID_W if transposed else rows, rows if transposed else GRID_W, nl)
            xr = jnp.concatenate([xt, xr[nl:nl + nc], pad])
            transposed = not transposed
        if l % 2 == 0:
            xr = _ab_layer(xr, dims, seg_fn, tm, n_out, norm1_g[l], sc1, sh1, g1, ab_w_in[j], ab_conv_w[j],
                           ab_conv_b[j], ml_ig_b[j], ml_fg_b[j], lb_all[l], ml_norm_g[j], hg_norm_g[j], ab_w_out[j])
        else:
            xr = _ssd_layer(xr, dims, seg_fn, tm, n_out, norm1_g[l], sc1, sh1, g1, ssd_w_in[j], ssd_conv_w[j],
                            ssd_conv_b[j], ssd_dt_b[j], ssd_a_log[j], ssd_d[j], ssd_norm_g[j], ssd_w_out[j])
        xr = _moe(xr, n_out, seg_fn, norm2_g[l], sc2, sh2, g2, router_wt, router_b, moe_w1, moe_w3, moe_w2, l,
                  final_g if l == depth - 1 else None)
    if transposed:
        xr = _regroup(xr, b, GRID_W, rows, nl)
    return xr[:nl].reshape(b, seq, d)
```

```python
import functools
import math

import jax
import jax.numpy as jnp
from jax import lax
from jax.experimental import pallas as pl
from jax.experimental.pallas import tpu as pltpu

F32 = jnp.float32
BF16 = jnp.bfloat16
HI = lax.Precision.HIGHEST

D_MODEL = 2048
GRID_W = 64
EPS = 1e-6
CHUNK = 64
CONV_K = 5
ML_H, ML_DK, ML_DV = 4, 128, 256
HG_H, HG_DK, HG_DV = 8, 128, 128
ML_QK, ML_V = ML_H * ML_DK, ML_H * ML_DV
HG_K, HG_V = HG_H * HG_DK, HG_H * HG_DV
AB_OUT = ML_V + HG_V
D_INNER = 2 * D_MODEL
SSD_P, SSD_G, SSD_N = 64, 8, 128
SSD_H = D_INNER // SSD_P
SSD_R = SSD_H // SSD_G
N_EXPERTS, N_GROUPS, TOP_K, D_EXPERT = 16, 4, 2, 1024
EXPERTS_PER_GROUP = N_EXPERTS // N_GROUPS

LANES = 128
SUBLANES = 8
VMEM_LIMIT = 56 * 1024 * 1024

AB_S_Q, AB_S_K, AB_S_V, AB_S_OG, AB_S_HQ, AB_S_HF, AB_S_HI, AB_S_HG, AB_S_GATE = 0, 4, 8, 16, 24, 32, 48, 56, 64
AB_SLABS = 66
SSD_S_Z, SSD_S_X, SSD_S_B, SSD_S_C, SSD_S_DT = 0, 32, 64, 72, 80
SSD_SLABS = 81


def _cparams(sem):
    return pltpu.CompilerParams(dimension_semantics=sem, vmem_limit_bytes=VMEM_LIMIT)


def _silu(x):
    return x * jax.nn.sigmoid(x)


def _dot(a, b):
    return jnp.dot(a.astype(BF16), b.astype(BF16), preferred_element_type=F32)


def _dot_nt(a, b):
    return lax.dot_general(a.astype(BF16), b.astype(BF16), (((1,), (1,)), ((), ())),
                           preferred_element_type=F32)


def _dot_tn(a, b):
    return lax.dot_general(a.astype(BF16), b.astype(BF16), (((0,), (0,)), ((), ())),
                           preferred_element_type=F32)


def _dot_hi(a, b):
    return jnp.dot(a, b, precision=HI, preferred_element_type=F32)


def _tri(rev):
    t = lax.broadcasted_iota(jnp.int32, (CHUNK, CHUNK), 0)
    s = lax.broadcasted_iota(jnp.int32, (CHUNK, CHUNK), 1)
    return (s >= t) if rev else (s <= t)


def _mod_kernel(c_ref, w_ref, b_ref, o_ref):
    c = c_ref[...]
    o_ref[...] = _dot(_silu(c), w_ref[...]) + b_ref[...]


def _modulation(c8, ada_w, ada_b):
    depth, d, d6 = ada_w.shape
    tn = 1024
    return pl.pallas_call(
        _mod_kernel,
        grid=(depth, d6 // tn),
        in_specs=[pl.BlockSpec((SUBLANES, d), lambda l, j: (0, 0)),
                  pl.BlockSpec((None, d, tn), lambda l, j: (l, 0, j)),
                  pl.BlockSpec((None, 1, tn), lambda l, j: (l, 0, j))],
        out_specs=pl.BlockSpec((None, SUBLANES, tn), lambda l, j: (l, 0, j)),
        out_shape=jax.ShapeDtypeStruct((depth, SUBLANES, d6), F32),
        compiler_params=_cparams(("parallel", "parallel")),
        name="modulation",
    )(c8, ada_w, ada_b.reshape(depth, 1, d6))


def _inproj_kernel(x_ref, g_ref, sc_ref, sh_ref, w_ref, o_ref, h_ref):
    @pl.when(pl.program_id(1) == 0)
    def _():
        x = x_ref[...]
        y = x * lax.rsqrt(jnp.mean(x * x, axis=-1, keepdims=True) + EPS) * g_ref[...]
        h_ref[...] = (y * (1.0 + sc_ref[...]) + sh_ref[...]).astype(BF16)

    acc = jnp.dot(h_ref[...], w_ref[...], preferred_element_type=F32)
    for s in range(o_ref.shape[0]):
        o_ref[s] = acc[:, s * LANES:(s + 1) * LANES]


def _inproj(x, g, sc, sh, w, seg_fn, tm, tn):
    n, d = x.shape
    ncols = w.shape[1]
    nsl = tn // LANES
    return pl.pallas_call(
        _inproj_kernel,
        grid=(n // tm, ncols // tn),
        in_specs=[pl.BlockSpec((tm, d), lambda i, j: (i, 0)),
                  pl.BlockSpec((1, d), lambda i, j: (0, 0)),
                  pl.BlockSpec((None, 1, d), lambda i, j: (seg_fn(i * tm), 0, 0)),
                  pl.BlockSpec((None, 1, d), lambda i, j: (seg_fn(i * tm), 0, 0)),
                  pl.BlockSpec((d, tn), lambda i, j: (0, j))],
        out_specs=pl.BlockSpec((nsl, tm, LANES), lambda i, j: (j, i, 0)),
        out_shape=jax.ShapeDtypeStruct((ncols // LANES, n, LANES), F32),
        scratch_shapes=[pltpu.VMEM((tm, d), BF16)],
        compiler_params=_cparams(("parallel", "arbitrary")),
        name="inproj",
    )(x, g.reshape(1, d), sc, sh, w)


def _conv_kernel(prev_ref, cur_ref, next_ref, w_ref, o_ref, *, tt, tiles_lat, n_lat_tiles, tiles_ctx):
    i = pl.program_id(1)
    in_lat = i < n_lat_tiles
    pos = jnp.where(in_lat, i % tiles_lat, (i - n_lat_tiles) % tiles_ctx)
    last = jnp.where(in_lat, tiles_lat - 1, tiles_ctx - 1)
    keep_prev = (pos != 0).astype(F32)
    keep_next = (pos != last).astype(F32)
    ext = jnp.concatenate([prev_ref[...] * keep_prev, cur_ref[...], next_ref[...] * keep_next], axis=1)
    w = w_ref[...]
    acc = jnp.zeros(cur_ref.shape, F32) + w[:, CONV_K:CONV_K + 1, :]
    for k in range(CONV_K):
        shift = (CONV_K // 2 - k) % (tt + 2 * SUBLANES)
        r = ext if shift == 0 else pltpu.roll(ext, shift, 1)
        acc = acc + r[:, SUBLANES:SUBLANES + tt, :] * w[:, k:k + 1, :]
    o_ref[...] = (_silu(acc) * w[:, CONV_K + 1:CONV_K + 2, :]).astype(o_ref.dtype)


def _conv_slabs(p3, slab0, nslab, wpack, dims, sb):
    b, seq, ctxl = dims
    n = b * (seq + ctxl)
    tt = min(256, ctxl)
    t8 = tt // SUBLANES
    nblk8 = n // SUBLANES
    s0 = slab0 // sb
    kern = functools.partial(_conv_kernel, tt=tt, tiles_lat=seq // tt, n_lat_tiles=b * seq // tt,
                             tiles_ctx=ctxl // tt)
    return pl.pallas_call(
        kern,
        grid=(nslab // sb, n // tt),
        in_specs=[pl.BlockSpec((sb, SUBLANES, LANES), lambda s, i: (s0 + s, jnp.maximum(i * t8 - 1, 0), 0)),
                  pl.BlockSpec((sb, tt, LANES), lambda s, i: (s0 + s, i, 0)),
                  pl.BlockSpec((sb, SUBLANES, LANES),
                               lambda s, i: (s0 + s, jnp.minimum((i + 1) * t8, nblk8 - 1), 0)),
                  pl.BlockSpec((sb, SUBLANES, LANES), lambda s, i: (s, 0, 0))],
        out_specs=pl.BlockSpec((sb, tt, LANES), lambda s, i: (s, i, 0)),
        out_shape=jax.ShapeDtypeStruct((nslab, n, LANES), BF16),
        compiler_params=_cparams(("parallel", "parallel")),
        name="conv",
    )(p3, p3, p3, wpack)


def _conv_pack(conv_w, conv_b, scale):
    c = conv_w.shape[1]
    rows = jnp.concatenate([conv_w, conv_b[None], scale[None], jnp.zeros((1, c), F32)], axis=0)
    return rows.reshape(SUBLANES, c // LANES, LANES).transpose(1, 0, 2)


def _chunk_maps(dims):
    b, seq, ctxl = dims
    ncc, nlc = ctxl // CHUNK, seq // CHUNK

    def fwd(bi, i):
        return jnp.where(i < ncc, b * nlc + bi * ncc + i, bi * nlc + (i - ncc))

    def bwd(bi, i):
        return jnp.where(i < ncc, b * nlc + bi * ncc + (ncc - 1 - i), bi * nlc + (nlc - 1 - (i - ncc)))

    return fwd, bwd, ncc + nlc


def _mlstm_kernel(qkf_ref, vf_ref, gcf_ref, grf_ref, qkb_ref, vb_ref, gcb_ref, grb_ref, bc_ref, br_ref,
                  of_ref, ob_ref, c_ref, n_ref, m_ref):
    @pl.when(pl.program_id(1) == 0)
    def _():
        c_ref[...] = jnp.zeros_like(c_ref)
        n_ref[...] = jnp.zeros_like(n_ref)
        m_ref[...] = jnp.zeros_like(m_ref)

    ng = 2 * ML_H
    for d, (qk_ref, v_ref, gc_ref, gr_ref, o_ref) in enumerate(
            ((qkf_ref, vf_ref, gcf_ref, grf_ref, of_ref), (qkb_ref, vb_ref, gcb_ref, grb_ref, ob_ref))):
        rev = d == 1
        mask = _tri(rev)
        gcol = gc_ref[0][:, :2 * ng] + bc_ref[...]
        grow = gr_ref[...] + br_ref[...]
        ic_all = gcol[:, d * ML_H:(d + 1) * ML_H]
        lfc_all = jax.nn.log_sigmoid(gcol[:, ng + d * ML_H:ng + (d + 1) * ML_H])
        ir_all = grow[d * ML_H:(d + 1) * ML_H, :]
        lfr_all = jax.nn.log_sigmoid(grow[ng + d * ML_H:ng + (d + 1) * ML_H, :])
        incl = mask.astype(F32)
        bcol_all = _dot_hi(incl, lfc_all)
        brow_all = _dot_hi(lfr_all, _tri(not rev).astype(F32))
        last = 0 if rev else CHUNK - 1
        for h in range(ML_H):
            q = qk_ref[h]
            k = qk_ref[ML_H + h]
            v = jnp.concatenate([v_ref[2 * h], v_ref[2 * h + 1]], axis=1)
            bcol, icol = bcol_all[:, h:h + 1], ic_all[:, h:h + 1]
            brow, irow = brow_all[h:h + 1, :], ir_all[h:h + 1, :]
            cst = c_ref[d, h]
            nst = n_ref[d, h]
            mprev = m_ref[d, h][:, :1]
            logd = jnp.where(mask, bcol - brow + irow, -jnp.inf)
            inter = bcol + mprev
            m_t = jnp.maximum(inter, jnp.max(logd, axis=1, keepdims=True))
            s = _dot_nt(q, k) * jnp.exp(logd - m_t)
            sc = jnp.exp(inter - m_t)
            num = _dot(s, v) + sc * _dot_nt(q, cst)
            den = jnp.sum(s, axis=1, keepdims=True) + sc * jnp.sum(q * nst, axis=1, keepdims=True)
            hout = num / jnp.maximum(jnp.abs(den), jnp.exp(-m_t))
            o_ref[2 * h] = hout[:, :LANES]
            o_ref[2 * h + 1] = hout[:, LANES:]
            b_last = bcol[last:last + 1, :]
            wlog = b_last - bcol + icol
            m_new = jnp.maximum(b_last + mprev, jnp.max(wlog, axis=0, keepdims=True))
            w = jnp.exp(wlog - m_new)
            dec = jnp.exp(b_last + mprev - m_new)
            c_ref[d, h] = dec * cst + _dot_tn(w * v, k)
            n_ref[d, h] = dec * nst + jnp.sum(w * k, axis=0, keepdims=True)
            m_ref[d, h] = jnp.broadcast_to(m_new, (1, LANES))


def _mlstm(qk, p3, grow, bcol, brow, dims):
    b = dims[0]
    n = qk.shape[1]
    fwd, bwd, nch = _chunk_maps(dims)

    def specs(cm):
        return [pl.BlockSpec((2 * ML_H, CHUNK, LANES), lambda bi, i: (0, cm(bi, i), 0)),
                pl.BlockSpec((2 * ML_H, CHUNK, LANES), lambda bi, i: (AB_S_V // (2 * ML_H), cm(bi, i), 0)),
                pl.BlockSpec((1, CHUNK, LANES), lambda bi, i: (AB_S_GATE, cm(bi, i), 0)),
                pl.BlockSpec((None, 4 * ML_H, CHUNK), lambda bi, i: (cm(bi, i), 0, 0))]

    out_sd = jax.ShapeDtypeStruct((2 * ML_H, n, LANES), F32)
    return pl.pallas_call(
        _mlstm_kernel,
        grid=(b, nch),
        in_specs=specs(fwd) + specs(bwd) + [pl.BlockSpec((1, 4 * ML_H), lambda bi, i: (0, 0)),
                                            pl.BlockSpec((4 * ML_H, 1), lambda bi, i: (0, 0))],
        out_specs=[pl.BlockSpec((2 * ML_H, CHUNK, LANES), lambda bi, i: (0, fwd(bi, i), 0)),
                   pl.BlockSpec((2 * ML_H, CHUNK, LANES), lambda bi, i: (0, bwd(bi, i), 0))],
        out_shape=[out_sd, out_sd],
        scratch_shapes=[pltpu.VMEM((2, ML_H, ML_DV, ML_DK), F32),
                        pltpu.VMEM((2, ML_H, 1, ML_DK), F32),
                        pltpu.VMEM((2, ML_H, 1, LANES), F32)],
        compiler_params=_cparams(("parallel", "arbitrary")),
        name="mlstm",
    )(qk, p3, p3, grow, qk, p3, p3, grow, bcol, brow)


def _bcast_rows(a, rows, span):
    parts = [jnp.broadcast_to(a[r:r + 1, :], (span, a.shape[1])) for r in rows]
    return parts[0] if len(parts) == 1 else jnp.concatenate(parts, axis=0)


def _hgrn2_kernel(qf_ref, ff_ref, vf_ref, qb_ref, fb_ref, vb_ref, lb_ref, of_ref, ob_ref, s_ref):
    @pl.when(pl.program_id(1) == 0)
    def _():
        s_ref[...] = jnp.zeros_like(s_ref)

    t = lax.broadcasted_iota(jnp.int32, (CHUNK, CHUNK), 0)
    s = lax.broadcasted_iota(jnp.int32, (CHUNK, CHUNK), 1)
    refs = ((qf_ref, ff_ref, vf_ref, of_ref), (qb_ref, fb_ref, vb_ref, ob_ref))
    units = [(d, h) for d in range(2) for h in range(HG_H)]
    levels = (32, 16, 8)

    def level_mask(m, rev):
        same = (t // (2 * m)) == (s // (2 * m))
        t_late = ((t // m) % 2 == 0) if rev else ((t // m) % 2 == 1)
        s_early = ((s // m) % 2 == 1) if rev else ((s // m) % 2 == 0)
        return same & t_late & s_early

    p1 = {}
    for d, h in units:
        q_ref, f_ref, v_ref, _ = refs[d]
        lb = lb_ref[h]
        f = lb + (1.0 - lb) * jax.nn.sigmoid(f_ref[h])
        lg = jnp.log(f)
        a = _dot_hi(_tri(d == 1).astype(F32), lg)
        p1[d, h] = (q_ref[h], 1.0 - f, lg, v_ref[h], a, s_ref[d, h])
    p2 = {}
    for d, h in units:
        rev = d == 1
        q, k, lg, v, a, st = p1[d, h]
        last = 0 if rev else CHUNK - 1
        qs, ks = [], []
        for m in levels:
            nb = CHUNK // (2 * m)
            aref = _bcast_rows(a, [bi * 2 * m + (m if rev else m - 1) for bi in range(nb)], 2 * m)
            qs.append((q * jnp.exp(jnp.minimum(a - aref, 0.0))).astype(BF16))
            ks.append((k * jnp.exp(jnp.minimum(aref - a, 0.0))).astype(BF16))
        aref = _bcast_rows(a - lg, [bi * SUBLANES + (SUBLANES - 1 if rev else 0) for bi in range(CHUNK // SUBLANES)],
                           SUBLANES)
        qs.append((q * jnp.exp(a - aref)).astype(BF16))
        ks.append((k * jnp.exp(aref - a)).astype(BF16))
        a_last = a[last:last + 1, :]
        p2[d, h] = (qs, ks, (q * jnp.exp(a)).astype(BF16), (k * jnp.exp(a_last - a)).astype(BF16), jnp.exp(a_last))
    p3 = {}
    for d, h in units:
        qs, ks, qdec, kdec, edec = p2[d, h]
        st = p1[d, h][5]
        p3[d, h] = ([_dot_nt(qq, kk) for qq, kk in zip(qs, ks)], _dot_nt(qdec, st))
    for d, h in units:
        rev = d == 1
        prods, qst = p3[d, h]
        scores = jnp.zeros((CHUNK, CHUNK), F32)
        for m, pr in zip(levels, prods[:-1]):
            scores = scores + jnp.where(level_mask(m, rev), pr, 0.0)
        diag = ((t // SUBLANES) == (s // SUBLANES)) & _tri(rev)
        scores = scores + jnp.where(diag, prods[-1], 0.0)
        refs[d][3][h] = _dot(scores, p1[d, h][3]) + qst
    for d, h in units:
        v, st = p1[d, h][3], p1[d, h][5]
        kdec, edec = p2[d, h][3], p2[d, h][4]
        s_ref[d, h] = st * edec + _dot_tn(v, kdec)


def _hgrn2(p3, lb, dims):
    b = dims[0]
    n = b * (dims[1] + dims[2])
    fwd, bwd, nch = _chunk_maps(dims)

    def specs(cm, d):
        return [pl.BlockSpec((HG_H, CHUNK, LANES), lambda bi, i: (AB_S_HQ // HG_H, cm(bi, i), 0)),
                pl.BlockSpec((HG_H, CHUNK, LANES), lambda bi, i: (AB_S_HF // HG_H + d, cm(bi, i), 0)),
                pl.BlockSpec((HG_H, CHUNK, LANES), lambda bi, i: (AB_S_HI // HG_H, cm(bi, i), 0))]

    out_sd = jax.ShapeDtypeStruct((HG_H, n, LANES), F32)
    return pl.pallas_call(
        _hgrn2_kernel,
        grid=(b, nch),
        in_specs=specs(fwd, 0) + specs(bwd, 1) + [pl.BlockSpec((HG_H, 1, LANES), lambda bi, i: (0, 0, 0))],
        out_specs=[pl.BlockSpec((HG_H, CHUNK, LANES), lambda bi, i: (0, fwd(bi, i), 0)),
                   pl.BlockSpec((HG_H, CHUNK, LANES), lambda bi, i: (0, bwd(bi, i), 0))],
        out_shape=[out_sd, out_sd],
        scratch_shapes=[pltpu.VMEM((2, HG_H, HG_DV, HG_DK), F32)],
        compiler_params=_cparams(("parallel", "arbitrary")),
        name="hgrn2",
    )(p3, p3, p3, p3, p3, p3, lb)


def _ab_out_kernel(mf_ref, mb_ref, hf_ref, hb_ref, og_ref, hg_ref, mlg_ref, hgg_ref, w_ref, x_ref, g1_ref,
                   o_ref, lhs_ref):
    for h in range(ML_H):
        hs = jnp.concatenate([mf_ref[2 * h] + mb_ref[2 * h], mf_ref[2 * h + 1] + mb_ref[2 * h + 1]], axis=1)
        r = hs * lax.rsqrt(jnp.mean(hs * hs, axis=-1, keepdims=True) + EPS)
        og = jnp.concatenate([og_ref[2 * h], og_ref[2 * h + 1]], axis=1)
        y = jax.nn.sigmoid(og) * (r * mlg_ref[:, h * ML_DV:(h + 1) * ML_DV])
        lhs_ref[:, h * ML_DV:(h + 1) * ML_DV] = y.astype(BF16)
    for h in range(HG_H):
        hs = hf_ref[h] + hb_ref[h]
        r = hs * lax.rsqrt(jnp.mean(hs * hs, axis=-1, keepdims=True) + EPS)
        y = _silu(hg_ref[h]) * (r * hgg_ref[:, h * HG_DV:(h + 1) * HG_DV])
        lhs_ref[:, ML_V + h * HG_DV:ML_V + (h + 1) * HG_DV] = y.astype(BF16)
    acc = jnp.dot(lhs_ref[...], w_ref[...], preferred_element_type=F32)
    o_ref[...] = x_ref[...] + g1_ref[...] * acc


def _ab_out(hm, ho, p3, ml_g, hg_g, w_out, x, g1, seg_fn, n, tm):
    d = x.shape[1]
    slab8 = lambda idx: pl.BlockSpec((SUBLANES, tm, LANES), lambda i: (idx, i, 0))
    return pl.pallas_call(
        _ab_out_kernel,
        grid=(n // tm,),
        in_specs=[slab8(0), slab8(0), slab8(0), slab8(0), slab8(AB_S_OG // SUBLANES), slab8(AB_S_HG // SUBLANES),
                  pl.BlockSpec((1, ML_V), lambda i: (0, 0)),
                  pl.BlockSpec((1, HG_V), lambda i: (0, 0)),
                  pl.BlockSpec((AB_OUT, d), lambda i: (0, 0)),
                  pl.BlockSpec((tm, d), lambda i: (i, 0)),
                  pl.BlockSpec((None, 1, d), lambda i: (seg_fn(i * tm), 0, 0))],
        out_specs=pl.BlockSpec((tm, d), lambda i: (i, 0)),
        out_shape=jax.ShapeDtypeStruct((n, d), F32),
        scratch_shapes=[pltpu.VMEM((tm, AB_OUT), BF16)],
        compiler_params=_cparams(("parallel",)),
        name="ab_out",
    )(hm[0], hm[1], ho[0], ho[1], p3, p3, ml_g.reshape(1, ML_V), hg_g.reshape(1, HG_V), w_out, x, g1)


def _ab_weight(w_in):
    d = w_in.shape[0]
    o = [0, 2 * ML_QK, 2 * ML_QK + ML_V, 2 * ML_QK + 2 * ML_V]
    g0 = o[3]
    h0 = g0 + 4 * ML_H
    pad = AB_SLABS * LANES - (w_in.shape[1])
    return jnp.concatenate([w_in[:, :g0], w_in[:, h0:], w_in[:, g0:h0], jnp.zeros((d, pad), w_in.dtype)],
                           axis=1).astype(BF16)


def _ab_layer(x, dims, seg_fn, tm, n_out, g_norm, sc, sh, g1, w_in, conv_w, conv_b, ig_b, fg_b, lb, ml_g, hg_g,
              w_out):
    b, seq, ctxl = dims
    n = b * (seq + ctxl)
    p3 = _inproj(x, g_norm, sc, sh, _ab_weight(w_in), seg_fn, min(2 * tm, seq), 11 * LANES)
    kscale = jnp.concatenate([jnp.ones((ML_QK,), F32), jnp.full((ML_QK,), ML_DK ** -0.5, F32)])
    qk = _conv_slabs(p3, AB_S_Q, 2 * ML_H, _conv_pack(conv_w, conv_b, kscale), dims, 2 * ML_H)
    gates = p3[AB_S_GATE, :n, :4 * ML_H]
    grow = gates.reshape(n // CHUNK, CHUNK, 4 * ML_H).transpose(0, 2, 1)
    gbias = jnp.concatenate([ig_b.reshape(-1), fg_b.reshape(-1)])
    hm = _mlstm(qk, p3, grow, gbias.reshape(1, -1), gbias.reshape(-1, 1), dims)
    ho = _hgrn2(p3, lb.reshape(HG_H, 1, HG_DK), dims)
    return _ab_out(hm, ho, p3, ml_g, hg_g, w_out.astype(BF16), x, g1, seg_fn, n_out, min(tm, 256))


def _ssd_kernel(xf_ref, dcf_ref, drf_ref, xb_ref, dcb_ref, drb_ref, bc_ref, br_ref, ac_ref, ar_ref,
                of_ref, ob_ref, h_ref):
    @pl.when(pl.program_id(1) == 0)
    def _():
        h_ref[...] = jnp.zeros_like(h_ref)

    gw = SSD_R * SSD_P
    t_idx = lax.broadcasted_iota(jnp.int32, (CHUNK, gw), 0)
    s_idx = lax.broadcasted_iota(jnp.int32, (CHUNK, gw), 1) % SSD_P
    e_r = lax.broadcasted_iota(jnp.int32, (3 * SSD_R, gw), 0) % SSD_R
    e_c = lax.broadcasted_iota(jnp.int32, (3 * SSD_R, gw), 1) // SSD_P
    expand3 = (e_r == e_c).astype(BF16)
    low_half = lax.broadcasted_iota(jnp.int32, (CHUNK, LANES), 1) < SSD_P

    def expand(v):
        hi = v.astype(BF16).astype(F32)
        mid = (v - hi).astype(BF16).astype(F32)
        lo = v - hi - mid
        return jnp.dot(jnp.concatenate([hi, mid, lo], axis=1).astype(BF16), expand3, preferred_element_type=F32)

    refs = ((xf_ref, dcf_ref, drf_ref, of_ref), (xb_ref, dcb_ref, drb_ref, ob_ref))
    units = [(d, g) for d in range(2) for g in range(SSD_G)]
    gate = []
    for d, (x_ref, dc_ref, dr_ref, o_ref) in enumerate(refs):
        rev = d == 1
        mask = (s_idx >= t_idx) if rev else (s_idx <= t_idx)
        last = 0 if rev else CHUNK - 1
        hs = slice(d * SSD_H, (d + 1) * SSD_H)
        dtc = jax.nn.softplus(dc_ref[0][:, hs] + bc_ref[:, hs])
        lac = dtc * ac_ref[:, hs]
        dtr = jax.nn.softplus(dr_ref[hs, :] + br_ref[hs, :])
        lar = dtr * ar_ref[hs, :]
        cum_c = _dot_hi(_tri(rev).astype(F32), lac)
        cum_r = _dot_hi(lar, _tri(not rev).astype(F32))
        wgt = jnp.exp(cum_c[last:last + 1, :] - cum_c) * dtc
        gate.append((mask, last, cum_c, cum_r, dtr, wgt))
    p1 = {}
    for d, g in units:
        x_ref = refs[d][0]
        mask, last, cum_c, cum_r, dtr, wgt = gate[d]
        heads = slice(g * SSD_R, (g + 1) * SSD_R)
        x = jnp.concatenate([x_ref[4 * g + j] for j in range(4)], axis=1)
        bm = x_ref[4 * SSD_G + g]
        cm = x_ref[5 * SSD_G + g]
        hst = h_ref[d, g]
        p1[d, g] = (x, bm, hst, expand(cum_c[:, heads]), expand(wgt[:, heads]), _dot_nt(cm, bm), _dot(cm, hst))
    p2 = {}
    for d, g in units:
        mask, last, cum_c, cum_r, dtr, wgt = gate[d]
        x, bm, hst, cum_x, wgt_x, cb, yoff = p1[d, g]
        rows = lambda a: jnp.concatenate([a[g * SSD_R + r:g * SSD_R + r + 1, :] for r in range(SSD_R)], axis=1)
        seg = jnp.exp(jnp.where(mask, cum_x - rows(cum_r), -jnp.inf)) * rows(dtr)
        cb2 = jnp.concatenate([cb, cb], axis=1)
        ms = [(cb2 * seg[:, j * LANES:(j + 1) * LANES]).astype(BF16) for j in range(4)]
        p2[d, g] = (ms, jnp.exp(cum_x), (x * wgt_x).astype(BF16))
    for d, g in units:
        o_ref = refs[d][3]
        x, yoff = p1[d, g][0], p1[d, g][6]
        ms, ecum_x, xw = p2[d, g]
        for j in range(4):
            ls = slice(j * LANES, (j + 1) * LANES)
            xp = x[:, ls]
            bd = jnp.concatenate([jnp.where(low_half, xp, 0.0), jnp.where(low_half, 0.0, xp)], axis=0)
            o_ref[4 * g + j] = (_dot(ms[j], bd) + ecum_x[:, ls] * yoff[:, ls]).astype(o_ref.dtype)
    for d, g in units:
        last = gate[d][1]
        bm, hst = p1[d, g][1], p1[d, g][2]
        ms, ecum_x, xw = p2[d, g]
        h_ref[d, g] = hst * ecum_x[last:last + 1, :] + _dot_tn(bm, xw)


def _ssd_scan(xbc, p3, dtrow, dt_b, neg_a, dims):
    b = dims[0]
    n = xbc.shape[1]
    fwd, bwd, nch = _chunk_maps(dims)
    nxs = xbc.shape[0]

    def specs(cm):
        return [pl.BlockSpec((nxs, CHUNK, LANES), lambda bi, i: (0, cm(bi, i), 0)),
                pl.BlockSpec((1, CHUNK, LANES), lambda bi, i: (SSD_S_DT, cm(bi, i), 0)),
                pl.BlockSpec((None, 2 * SSD_H, CHUNK), lambda bi, i: (cm(bi, i), 0, 0))]

    vec = lambda shape: pl.BlockSpec(shape, lambda bi, i: (0, 0))
    out_sd = jax.ShapeDtypeStruct((D_INNER // LANES, n, LANES), BF16)
    return pl.pallas_call(
        _ssd_kernel,
        grid=(b, nch),
        in_specs=specs(fwd) + specs(bwd) + [vec((1, 2 * SSD_H)), vec((2 * SSD_H, 1)),
                                            vec((1, 2 * SSD_H)), vec((2 * SSD_H, 1))],
        out_specs=[pl.BlockSpec((D_INNER // LANES, CHUNK, LANES), lambda bi, i: (0, fwd(bi, i), 0)),
                   pl.BlockSpec((D_INNER // LANES, CHUNK, LANES), lambda bi, i: (0, bwd(bi, i), 0))],
        out_shape=[out_sd, out_sd],
        scratch_shapes=[pltpu.VMEM((2, SSD_G, SSD_N, SSD_R * SSD_P), F32)],
        compiler_params=_cparams(("parallel", "arbitrary")),
        name="ssd_scan",
    )(xbc, p3, dtrow, xbc, p3, dtrow, dt_b.reshape(1, -1), dt_b.reshape(-1, 1),
      neg_a.reshape(1, -1), neg_a.reshape(-1, 1))


def _ssd_out_kernel(yf_ref, yb_ref, xs_ref, z_ref, dsk_ref, ng_ref, w_ref, x_ref, g1_ref, o_ref, lhs_ref, acc_ref):
    k = pl.program_id(1)

    @pl.when(k == 0)
    def _():
        acc_ref[...] = jnp.zeros_like(acc_ref)

    gw = D_INNER // SSD_G
    for gg in range(lhs_ref.shape[1] // gw):
        cat = lambda ref: jnp.concatenate([ref[4 * gg + j] for j in range(4)], axis=1).astype(F32)
        cs = slice(gg * gw, (gg + 1) * gw)
        y = cat(yf_ref) + cat(yb_ref) + dsk_ref[:, cs] * cat(xs_ref)
        u = y * _silu(cat(z_ref))
        u = u * lax.rsqrt(jnp.mean(u * u, axis=-1, keepdims=True) + EPS) * ng_ref[:, cs]
        lhs_ref[:, cs] = u.astype(BF16)
    acc_ref[...] += jnp.dot(lhs_ref[...], w_ref[...], preferred_element_type=F32)

    @pl.when(k == pl.num_programs(1) - 1)
    def _():
        o_ref[...] = x_ref[...] + g1_ref[...] * acc_ref[...]


def _ssd_out(yf, yb, xbc, p3, dskip, norm_g, w_out, x, g1, seg_fn, n_rows, tm):
    d = x.shape[1]
    tk = 1024
    nsl = tk // LANES
    slab = lambda: pl.BlockSpec((nsl, tm, LANES), lambda i, k: (k, i, 0))
    return pl.pallas_call(
        _ssd_out_kernel,
        grid=(n_rows // tm, D_INNER // tk),
        in_specs=[slab(), slab(), slab(), slab(),
                  pl.BlockSpec((1, tk), lambda i, k: (0, k)),
                  pl.BlockSpec((1, tk), lambda i, k: (0, k)),
                  pl.BlockSpec((tk, d), lambda i, k: (k, 0)),
                  pl.BlockSpec((tm, d), lambda i, k: (i, 0)),
                  pl.BlockSpec((None, 1, d), lambda i, k: (seg_fn(i * tm), 0, 0))],
        out_specs=pl.BlockSpec((tm, d), lambda i, k: (i, 0)),
        out_shape=jax.ShapeDtypeStruct((n_rows, d), F32),
        scratch_shapes=[pltpu.VMEM((tm, tk), BF16), pltpu.VMEM((tm, d), F32)],
        compiler_params=_cparams(("parallel", "arbitrary")),
        name="ssd_out",
    )(yf, yb, xbc, p3, dskip, norm_g.reshape(1, -1), w_out, x, g1)


def _ssd_layer(x, dims, seg_fn, tm, n_out, g_norm, sc, sh, g1, w_in, conv_w, conv_b, dt_b, a_log, d_skip,
               norm_g, w_out):
    b, seq, ctxl = dims
    n = b * (seq + ctxl)
    p3 = _inproj(x, g_norm, sc, sh, w_in.astype(BF16), seg_fn, min(2 * tm, seq), 9 * LANES)
    nconv = conv_w.shape[1]
    xbc = _conv_slabs(p3, SSD_S_X, nconv // LANES, _conv_pack(conv_w, conv_b, jnp.ones((nconv,), F32)), dims, 16)
    dt = p3[SSD_S_DT, :n]
    dtrow = dt.reshape(n // CHUNK, CHUNK, 2 * SSD_H).transpose(0, 2, 1)
    yf, yb = _ssd_scan(xbc, p3, dtrow, dt_b.reshape(-1), -jnp.exp(a_log.astype(F32)).reshape(-1), dims)
    dskip = jnp.repeat(d_skip.astype(F32), SSD_P).reshape(1, D_INNER)
    return _ssd_out(yf, yb, xbc, p3, dskip, norm_g, w_out.astype(BF16), x, g1, seg_fn, n_out, tm)


def _regroup_kernel(x_ref, o_ref):
    a = x_ref.shape[0]
    for cc in range(SUBLANES):
        o_ref[cc * a:(cc + 1) * a, :] = x_ref[:, cc, :]


def _regroup(x, b, a, c, n_out):
    d = x.shape[1]
    cblk = c // SUBLANES
    return pl.pallas_call(
        _regroup_kernel,
        grid=(b, cblk),
        in_specs=[pl.BlockSpec((a, SUBLANES, d), lambda bi, j: (bi, j, 0))],
        out_specs=pl.BlockSpec((a * SUBLANES, d), lambda bi, j: (bi * cblk + j, 0)),
        out_shape=jax.ShapeDtypeStruct((n_out, d), F32),
        compiler_params=_cparams(("parallel", "parallel")),
        name="regroup",
    )(x.reshape(x.shape[0] // c, c, d))


MOE_TILE = 512


def _router_kernel(x_ref, g_ref, sc_ref, sh_ref, wt_ref, rb_ref, hn_ref, ii_ref, iw_ref, cnt_ref, carry_ref):
    @pl.when(pl.program_id(0) == 0)
    def _():
        carry_ref[...] = jnp.zeros_like(carry_ref)

    x = x_ref[...]
    tm = x.shape[0]
    y = x * lax.rsqrt(jnp.mean(x * x, axis=-1, keepdims=True) + EPS) * g_ref[...]
    hn = y * (1.0 + sc_ref[...]) + sh_ref[...]
    _store_token_rows(hn_ref, hn)
    logits = lax.dot_general(wt_ref[...], hn, (((1,), (1,)), ((), ())), precision=HI,
                             preferred_element_type=F32)
    score = jax.nn.sigmoid(logits)
    biased = score + rb_ref[...]
    rb = [biased[e:e + 1, :] for e in range(N_EXPERTS)]
    rs = [score[e:e + 1, :] for e in range(N_EXPERTS)]
    gsc = []
    for g in range(N_GROUPS):
        a, b, c, d = rb[EXPERTS_PER_GROUP * g:EXPERTS_PER_GROUP * (g + 1)]
        hi1, lo1, hi2, lo2 = jnp.maximum(a, b), jnp.minimum(a, b), jnp.maximum(c, d), jnp.minimum(c, d)
        gsc.append(jnp.maximum(hi1, hi2) + jnp.maximum(jnp.minimum(hi1, hi2), jnp.maximum(lo1, lo2)))
    best = jnp.zeros((1, tm), jnp.int32)
    bsc = gsc[0]
    for g in range(1, N_GROUPS):
        upd = gsc[g] > bsc
        best = jnp.where(upd, g, best)
        bsc = jnp.where(upd, gsc[g], bsc)

    def pick(rows, p):
        out = rows[p]
        for g in range(1, N_GROUPS):
            out = jnp.where(best == g, rows[EXPERTS_PER_GROUP * g + p], out)
        return out

    vals = [pick(rb, p) for p in range(EXPERTS_PER_GROUP)]
    scs = [pick(rs, p) for p in range(EXPERTS_PER_GROUP)]
    p1, v1, s1 = jnp.zeros((1, tm), jnp.int32), vals[0], scs[0]
    for p in range(1, EXPERTS_PER_GROUP):
        upd = vals[p] > v1
        p1, v1, s1 = jnp.where(upd, p, p1), jnp.where(upd, vals[p], v1), jnp.where(upd, scs[p], s1)
    p2 = jnp.zeros((1, tm), jnp.int32)
    v2 = jnp.full((1, tm), -jnp.inf, F32)
    s2 = jnp.zeros((1, tm), F32)
    for p in range(EXPERTS_PER_GROUP):
        upd = (p1 != p) & (vals[p] > v2)
        p2, v2, s2 = jnp.where(upd, p, p2), jnp.where(upd, vals[p], v2), jnp.where(upd, scs[p], s2)
    e1 = best * EXPERTS_PER_GROUP + p1
    e2 = best * EXPERTS_PER_GROUP + p2
    tot = s1 + s2
    eiota = lax.broadcasted_iota(jnp.int32, (N_EXPERTS, tm), 0)
    oh1 = (eiota == e1).astype(F32)
    oh2 = (eiota == e2).astype(F32)
    oh = oh1 + oh2
    before = (lax.broadcasted_iota(jnp.int32, (tm, tm), 0) < lax.broadcasted_iota(jnp.int32, (tm, tm), 1))
    cnt = _dot(oh, before.astype(BF16)) + carry_ref[:, :1]
    r1 = jnp.sum(oh1 * cnt, axis=0, keepdims=True).astype(jnp.int32)
    r2 = jnp.sum(oh2 * cnt, axis=0, keepdims=True).astype(jnp.int32)
    zi = jnp.zeros((SUBLANES - 4, tm), jnp.int32)
    ii_ref[...] = jnp.concatenate([e1, e2, r1, r2, zi], axis=0)
    iw_ref[...] = jnp.concatenate([s1 / tot, s2 / tot, jnp.zeros((SUBLANES - 2, tm), F32)], axis=0)
    carry = carry_ref[...] + jnp.sum(oh, axis=1, keepdims=True)
    carry_ref[...] = carry
    cnt_ref[...] = carry


def _router(x, n_rows, g, sc, sh, seg_fn, router_wt, router_b):
    d = x.shape[1]
    tm = MOE_TILE
    return pl.pallas_call(
        _router_kernel,
        grid=(n_rows // tm,),
        in_specs=[pl.BlockSpec((tm, d), lambda i: (i, 0)),
                  pl.BlockSpec((1, d), lambda i: (0, 0)),
                  pl.BlockSpec((None, 1, d), lambda i: (seg_fn(i * tm), 0, 0)),
                  pl.BlockSpec((None, 1, d), lambda i: (seg_fn(i * tm), 0, 0)),
                  pl.BlockSpec((N_EXPERTS, d), lambda i: (0, 0)),
                  pl.BlockSpec((N_EXPERTS, 1), lambda i: (0, 0))],
        out_specs=[pl.BlockSpec((tm, d // LANES, LANES), lambda i: (i, 0, 0)),
                   pl.BlockSpec((SUBLANES, tm), lambda i: (0, i)),
                   pl.BlockSpec((SUBLANES, tm), lambda i: (0, i)),
                   pl.BlockSpec((N_EXPERTS, LANES), lambda i: (0, 0))],
        out_shape=[jax.ShapeDtypeStruct((n_rows, d // LANES, LANES), F32),
                   jax.ShapeDtypeStruct((SUBLANES, n_rows), jnp.int32),
                   jax.ShapeDtypeStruct((SUBLANES, n_rows), F32),
                   jax.ShapeDtypeStruct((N_EXPERTS, LANES), F32)],
        scratch_shapes=[pltpu.VMEM((N_EXPERTS, LANES), F32)],
        compiler_params=_cparams(("arbitrary",)),
        name="router",
    )(x, g.reshape(1, d), sc, sh, router_wt, router_b.reshape(N_EXPERTS, 1))


TOK_SUB = D_MODEL // LANES


def _store_token_rows(ref, val):
    for s in range(TOK_SUB):
        ref[:, s, :] = val[:, s * LANES:(s + 1) * LANES]


def _row_copy(src_ref, o_ref, sem, src_row, dst_row):
    return pltpu.make_async_copy(src_ref.at[pl.ds(src_row, 1)], o_ref.at[pl.ds(dst_row, 1)], sem)


def _expert_kernel(te_ref, nu_ref, xn_ref, w1_ref, w3_ref, w2_ref, o_ref, xb_ref, acc_ref, *, nf):
    i, f = pl.program_id(0), pl.program_id(1)
    compute = (i >= 1) & (i - 1 < nu_ref[0])
    part = xn_ref.shape[0] // nf

    def convert(fv):
        rows = slice(fv * part, (fv + 1) * part)
        for s in range(TOK_SUB):
            xb_ref[i % 2, rows, s * LANES:(s + 1) * LANES] = xn_ref[rows, s, :].astype(BF16)

    for fv in range(nf):
        @pl.when((f == fv) & (i == 0))
        def _(fv=fv):
            convert(fv)

        @pl.when((f == fv) & compute)
        def _(fv=fv):
            xb = xb_ref[(i - 1) % 2]
            a = _silu(_dot(xb, w1_ref[...])) * _dot(xb, w3_ref[...])
            y = _dot(a, w2_ref[...])
            convert(fv)
            if fv > 0:
                y = acc_ref[...] + y
            if fv < nf - 1:
                acc_ref[...] = y
            else:
                _store_token_rows(o_ref, y)

    @pl.when((f == nf - 1) & (i >= 1) & jnp.logical_not(compute))
    def _():
        o_ref[...] = jnp.zeros_like(o_ref)


def _experts(tile_e, n_used, xs, w1, w3, w2, layer):
    p = xs.shape[0]
    d = D_MODEL
    te, tf = MOE_TILE, 512
    nf = D_EXPERT // tf
    ntile = p // te
    tile = lambda i: jnp.maximum(i - 1, 0)
    fidx = lambda i, f, nu: jnp.where((i >= 1) & (i - 1 < nu[0]), f, nf - 1)
    grid_spec = pltpu.PrefetchScalarGridSpec(
        num_scalar_prefetch=2,
        grid=(ntile + 1, nf),
        in_specs=[pl.BlockSpec((te, TOK_SUB, LANES), lambda i, f, te_r, nu: (jnp.minimum(i, ntile - 1), 0, 0)),
                  pl.BlockSpec((None, None, d, tf),
                               lambda i, f, te_r, nu: (layer, te_r[tile(i)], 0, fidx(i, f, nu))),
                  pl.BlockSpec((None, None, d, tf),
                               lambda i, f, te_r, nu: (layer, te_r[tile(i)], 0, fidx(i, f, nu))),
                  pl.BlockSpec((None, None, tf, d),
                               lambda i, f, te_r, nu: (layer, te_r[tile(i)], fidx(i, f, nu), 0))],
        out_specs=pl.BlockSpec((te, TOK_SUB, LANES), lambda i, f, te_r, nu: (tile(i), 0, 0)),
        scratch_shapes=[pltpu.VMEM((2, te, d), BF16), pltpu.VMEM((te, d), F32)])
    return pl.pallas_call(
        functools.partial(_expert_kernel, nf=nf),
        grid_spec=grid_spec,
        out_shape=jax.ShapeDtypeStruct((p, TOK_SUB, LANES), F32),
        compiler_params=_cparams(("arbitrary", "arbitrary")),
        name="experts",
    )(tile_e, n_used, xs, w1, w3, w2)


def _combine_kernel(p1_ref, p2_ref, ys_ref, x_ref, w_ref, g2_ref, fg_ref, o_ref, y_ref, sem, *, final):
    tm = x_ref.shape[0]

    def issue(r8, carry):
        for u in range(SUBLANES):
            r = r8 * SUBLANES + u
            _row_copy(ys_ref, y_ref.at[0], sem, p1_ref[r], r).start()
            _row_copy(ys_ref, y_ref.at[1], sem, p2_ref[r], r).start()
        return carry

    lax.fori_loop(0, tm // SUBLANES, issue, 0)
    for k in range(TOP_K):
        pltpu.make_async_copy(ys_ref.at[pl.ds(0, tm)], y_ref.at[k], sem).wait()
    w = w_ref[...]
    w1 = jnp.broadcast_to(w[:, 0:1], (tm, LANES))
    w2 = jnp.broadcast_to(w[:, 1:2], (tm, LANES))
    for s in range(TOK_SUB):
        cs = slice(s * LANES, (s + 1) * LANES)
        o_ref[:, cs] = x_ref[:, cs] + g2_ref[:, cs] * (w1 * y_ref[0, :, s, :] + w2 * y_ref[1, :, s, :])
    if final:
        out = o_ref[...]
        o_ref[...] = out * lax.rsqrt(jnp.mean(out * out, axis=-1, keepdims=True) + EPS) * fg_ref[...]


def _combine(x, n_rows, pos, ys, wts, g2, seg_fn, final_g):
    d = x.shape[1]
    tm = MOE_TILE
    nt = n_rows // tm
    fg = jnp.ones((1, d), F32) if final_g is None else final_g.reshape(1, d)
    return pl.pallas_call(
        functools.partial(_combine_kernel, final=final_g is not None),
        grid=(nt,),
        in_specs=[pl.BlockSpec((tm,), lambda i: (i,), memory_space=pltpu.SMEM),
                  pl.BlockSpec((tm,), lambda i: (nt + i,), memory_space=pltpu.SMEM),
                  pl.BlockSpec(memory_space=pl.ANY),
                  pl.BlockSpec((tm, d), lambda i: (i, 0)),
                  pl.BlockSpec((tm, 2), lambda i: (i, 0)),
                  pl.BlockSpec((None, 1, d), lambda i: (seg_fn(i * tm), 0, 0)),
                  pl.BlockSpec((1, d), lambda i: (0, 0))],
        out_specs=pl.BlockSpec((tm, d), lambda i: (i, 0)),
        out_shape=jax.ShapeDtypeStruct((n_rows, d), F32),
        scratch_shapes=[pltpu.VMEM((TOP_K, tm, TOK_SUB, LANES), F32), pltpu.SemaphoreType.DMA],
        compiler_params=_cparams(("arbitrary",)),
        name="combine",
    )(pos, pos, ys, x, wts, g2, fg)


def _dispatch_kernel(p1_ref, p2_ref, ends_ref, hn_ref, xs_ref, zero_ref, sem, zsem, *, n_pos):
    tm = hn_ref.shape[0]
    zrows = zero_ref.shape[0]

    @pl.when(pl.program_id(0) == 0)
    def _():
        zero_ref[...] = jnp.zeros_like(zero_ref)
        starts = [jnp.maximum(ends_ref[e] - zrows, 0) for e in range(N_EXPERTS)]
        tail = [ends_ref[N_EXPERTS - 1] + j * zrows for j in range(N_EXPERTS)]
        for st in starts:
            pltpu.make_async_copy(zero_ref, xs_ref.at[pl.ds(st, zrows)], zsem).start()
        for st in tail:
            @pl.when(st < n_pos)
            def _(st=st):
                pltpu.make_async_copy(zero_ref, xs_ref.at[pl.ds(st, zrows)], zsem).start()
        for st in starts:
            pltpu.make_async_copy(zero_ref, xs_ref.at[pl.ds(st, zrows)], zsem).wait()
        for st in tail:
            @pl.when(st < n_pos)
            def _(st=st):
                pltpu.make_async_copy(zero_ref, xs_ref.at[pl.ds(st, zrows)], zsem).wait()

    def issue(r8, carry):
        for u in range(SUBLANES):
            r = r8 * SUBLANES + u
            _row_copy(hn_ref, xs_ref, sem, r, p1_ref[r]).start()
            _row_copy(hn_ref, xs_ref, sem, r, p2_ref[r]).start()
        return carry

    lax.fori_loop(0, tm // SUBLANES, issue, 0)
    for k in range(TOP_K):
        pltpu.make_async_copy(hn_ref, xs_ref.at[pl.ds(0, tm)], sem).wait()


def _dispatch(pos, ends, hn, n_rows, n_pos):
    tm = MOE_TILE
    nt = n_rows // tm
    return pl.pallas_call(
        functools.partial(_dispatch_kernel, n_pos=n_pos),
        grid=(nt,),
        in_specs=[pl.BlockSpec((tm,), lambda i: (i,), memory_space=pltpu.SMEM),
                  pl.BlockSpec((tm,), lambda i: (nt + i,), memory_space=pltpu.SMEM),
                  pl.BlockSpec(memory_space=pltpu.SMEM),
                  pl.BlockSpec((tm,) + hn.shape[1:], lambda i: (i, 0, 0))],
        out_specs=pl.BlockSpec(memory_space=pl.ANY),
        out_shape=jax.ShapeDtypeStruct((n_pos,) + hn.shape[1:], hn.dtype),
        scratch_shapes=[pltpu.VMEM((tm,) + hn.shape[1:], hn.dtype), pltpu.SemaphoreType.DMA,
                        pltpu.SemaphoreType.DMA],
        compiler_params=_cparams(("arbitrary",)),
        name="dispatch",
    )(pos, pos, ends, hn)


def _moe(x, n_rows, seg_fn, g, sc, sh, g2, router_wt, router_b, w1, w3, w2, layer, final_g=None):
    te = MOE_TILE
    hn, ii, iw, cnt = _router(x, n_rows, g, sc, sh, seg_fn, router_wt, router_b)
    counts = cnt[:, 0].astype(jnp.int32)
    padded = (counts + te - 1) // te * te
    ends = jnp.cumsum(padded)
    base = ends - padded
    pos = jnp.concatenate([base[ii[0]] + ii[2], base[ii[1]] + ii[3]])
    p = 2 * n_rows + N_EXPERTS * te
    ntile = p // te
    n_used = ends[-1] // te
    tiles = jnp.arange(ntile, dtype=jnp.int32)
    tile_e = jnp.sum((jnp.minimum(tiles, n_used - 1)[:, None] * te >= ends[None, :]).astype(jnp.int32), axis=1)
    xs = _dispatch(pos, ends, hn, n_rows, p)
    ys = _experts(tile_e, n_used.reshape(1).astype(jnp.int32), xs, w1, w3, w2, layer)
    return _combine(x, n_rows, pos, ys, iw[:2].T, g2, seg_fn, final_g)


def kernel(x, c, ctx, c_ctx, ada_w, ada_b, norm1_g, norm2_g, ab_w_in, ab_conv_w, ab_conv_b, ml_ig_b, ml_fg_b,
           hg_lb, ml_norm_g, hg_norm_g, ab_w_out, ssd_w_in, ssd_conv_w, ssd_conv_b, ssd_dt_b, ssd_a_log, ssd_d,
           ssd_norm_g, ssd_w_out, router_w, router_b, moe_w1, moe_w3, moe_w2, final_g):
    b, seq, d = x.shape
    ctxl = ctx.shape[1]
    depth = ada_w.shape[0]
    dims = (b, seq, ctxl)
    nl, nc = b * seq, b * ctxl
    rows = seq // GRID_W
    tm = min(512, nc)
    seg_fn = lambda row: jnp.where(row < nl, row // seq, b)
    c8 = jnp.concatenate([c, c_ctx[None], jnp.zeros((SUBLANES - b - 1, d), F32)])
    mods = _modulation(c8, ada_w, ada_b)[:, :b + 1].reshape(depth, b + 1, 6, 1, d)
    lb_all = jnp.cumsum(jax.nn.softmax(hg_lb.astype(F32), axis=0), axis=0)
    router_wt = router_w.T
    pad = jnp.zeros((-(nl + nc) % min(2 * tm, seq), d), F32)
    xr = jnp.concatenate([x.reshape(nl, d), ctx.reshape(nc, d), pad])
    transposed = False
    for l in range(depth):
        sh1, sc1, g1, sh2, sc2, g2 = (mods[l][:, k] for k in range(6))
        keep_ctx = l < depth - 1
        n_out = nl + nc if keep_ctx else nl
        j = l // 2
        if (l % 2 == 1) != transposed:
            xt = _regroup(xr, b, GRID_W if transposed else rows, rows if transposed else GRID_W, nl)
            xr = jnp.concatenate([xt, xr[nl:nl + nc], pad])
            transposed = not transposed
        if l % 2 == 0:
            xr = _ab_layer(xr, dims, seg_fn, tm, n_out, norm1_g[l], sc1, sh1, g1, ab_w_in[j], ab_conv_w[j],
                           ab_conv_b[j], ml_ig_b[j], ml_fg_b[j], lb_all[l], ml_norm_g[j], hg_norm_g[j], ab_w_out[j])
        else:
            xr = _ssd_layer(xr, dims, seg_fn, tm, n_out, norm1_g[l], sc1, sh1, g1, ssd_w_in[j], ssd_conv_w[j],
                            ssd_conv_b[j], ssd_dt_b[j], ssd_a_log[j], ssd_d[j], ssd_norm_g[j], ssd_w_out[j])
        xr = _moe(xr, n_out, seg_fn, norm2_g[l], sc2, sh2, g2, router_wt, router_b, moe_w1, moe_w3, moe_w2, l,
                  final_g if l == depth - 1 else None)
    if transposed:
        xr = _regroup(xr, b, GRID_W, rows, nl)
    return xr[:nl].reshape(b, seq, d)
```

```python
import functools
import math

import jax
import jax.numpy as jnp
from jax import lax
from jax.experimental import pallas as pl
from jax.experimental.pallas import tpu as pltpu

F32 = jnp.float32
BF16 = jnp.bfloat16
HI = lax.Precision.HIGHEST

D_MODEL = 2048
GRID_W = 64
EPS = 1e-6
CHUNK = 64
CONV_K = 5
ML_H, ML_DK, ML_DV = 4, 128, 256
HG_H, HG_DK, HG_DV = 8, 128, 128
ML_QK, ML_V = ML_H * ML_DK, ML_H * ML_DV
HG_K, HG_V = HG_H * HG_DK, HG_H * HG_DV
AB_OUT = ML_V + HG_V
D_INNER = 2 * D_MODEL
SSD_P, SSD_G, SSD_N = 64, 8, 128
SSD_H = D_INNER // SSD_P
SSD_R = SSD_H // SSD_G
N_EXPERTS, N_GROUPS, TOP_K, D_EXPERT = 16, 4, 2, 1024
EXPERTS_PER_GROUP = N_EXPERTS // N_GROUPS

LANES = 128
SUBLANES = 8
VMEM_LIMIT = 56 * 1024 * 1024

AB_S_Q, AB_S_K, AB_S_V, AB_S_OG, AB_S_HQ, AB_S_HF, AB_S_HI, AB_S_HG, AB_S_GATE = 0, 4, 8, 16, 24, 32, 48, 56, 64
AB_SLABS = 66
SSD_S_Z, SSD_S_X, SSD_S_B, SSD_S_C, SSD_S_DT = 0, 32, 64, 72, 80
SSD_SLABS = 81


def _cparams(sem):
    return pltpu.CompilerParams(dimension_semantics=sem, vmem_limit_bytes=VMEM_LIMIT)


def _silu(x):
    return x * jax.nn.sigmoid(x)


def _dot(a, b):
    return jnp.dot(a.astype(BF16), b.astype(BF16), preferred_element_type=F32)


def _dot_nt(a, b):
    return lax.dot_general(a.astype(BF16), b.astype(BF16), (((1,), (1,)), ((), ())),
                           preferred_element_type=F32)


def _dot_tn(a, b):
    return lax.dot_general(a.astype(BF16), b.astype(BF16), (((0,), (0,)), ((), ())),
                           preferred_element_type=F32)


def _dot_hi(a, b):
    return jnp.dot(a, b, precision=HI, preferred_element_type=F32)


def _tri(rev):
    t = lax.broadcasted_iota(jnp.int32, (CHUNK, CHUNK), 0)
    s = lax.broadcasted_iota(jnp.int32, (CHUNK, CHUNK), 1)
    return (s >= t) if rev else (s <= t)


def _mod_kernel(c_ref, w_ref, b_ref, o_ref):
    c = c_ref[...]
    o_ref[...] = _dot(_silu(c), w_ref[...]) + b_ref[...]


def _modulation(c8, ada_w, ada_b):
    depth, d, d6 = ada_w.shape
    tn = 1024
    return pl.pallas_call(
        _mod_kernel,
        grid=(depth, d6 // tn),
        in_specs=[pl.BlockSpec((SUBLANES, d), lambda l, j: (0, 0)),
                  pl.BlockSpec((None, d, tn), lambda l, j: (l, 0, j)),
                  pl.BlockSpec((None, 1, tn), lambda l, j: (l, 0, j))],
        out_specs=pl.BlockSpec((None, SUBLANES, tn), lambda l, j: (l, 0, j)),
        out_shape=jax.ShapeDtypeStruct((depth, SUBLANES, d6), F32),
        compiler_params=_cparams(("parallel", "parallel")),
        name="modulation",
    )(c8, ada_w, ada_b.reshape(depth, 1, d6))


def _inproj_kernel(x_ref, g_ref, sc_ref, sh_ref, w_ref, o_ref, h_ref):
    @pl.when(pl.program_id(1) == 0)
    def _():
        x = x_ref[...]
        y = x * lax.rsqrt(jnp.mean(x * x, axis=-1, keepdims=True) + EPS) * g_ref[...]
        h_ref[...] = (y * (1.0 + sc_ref[...]) + sh_ref[...]).astype(BF16)

    acc = jnp.dot(h_ref[...], w_ref[...], preferred_element_type=F32)
    for s in range(o_ref.shape[0]):
        o_ref[s] = acc[:, s * LANES:(s + 1) * LANES]


def _inproj(x, g, sc, sh, w, seg_fn, tm, tn):
    n, d = x.shape
    ncols = w.shape[1]
    nsl = tn // LANES
    return pl.pallas_call(
        _inproj_kernel,
        grid=(n // tm, ncols // tn),
        in_specs=[pl.BlockSpec((tm, d), lambda i, j: (i, 0)),
                  pl.BlockSpec((1, d), lambda i, j: (0, 0)),
                  pl.BlockSpec((None, 1, d), lambda i, j: (seg_fn(i * tm), 0, 0)),
                  pl.BlockSpec((None, 1, d), lambda i, j: (seg_fn(i * tm), 0, 0)),
                  pl.BlockSpec((d, tn), lambda i, j: (0, j))],
        out_specs=pl.BlockSpec((nsl, tm, LANES), lambda i, j: (j, i, 0)),
        out_shape=jax.ShapeDtypeStruct((ncols // LANES, n, LANES), F32),
        scratch_shapes=[pltpu.VMEM((tm, d), BF16)],
        compiler_params=_cparams(("parallel", "arbitrary")),
        name="inproj",
    )(x, g.reshape(1, d), sc, sh, w)


def _conv_kernel(prev_ref, cur_ref, next_ref, w_ref, o_ref, *, tt, tiles_lat, n_lat_tiles, tiles_ctx):
    i = pl.program_id(1)
    in_lat = i < n_lat_tiles
    pos = jnp.where(in_lat, i % tiles_lat, (i - n_lat_tiles) % tiles_ctx)
    last = jnp.where(in_lat, tiles_lat - 1, tiles_ctx - 1)
    keep_prev = (pos != 0).astype(F32)
    keep_next = (pos != last).astype(F32)
    ext = jnp.concatenate([prev_ref[...] * keep_prev, cur_ref[...], next_ref[...] * keep_next], axis=1)
    w = w_ref[...]
    acc = jnp.zeros(cur_ref.shape, F32) + w[:, CONV_K:CONV_K + 1, :]
    for k in range(CONV_K):
        shift = (CONV_K // 2 - k) % (tt + 2 * SUBLANES)
        r = ext if shift == 0 else pltpu.roll(ext, shift, 1)
        acc = acc + r[:, SUBLANES:SUBLANES + tt, :] * w[:, k:k + 1, :]
    o_ref[...] = (_silu(acc) * w[:, CONV_K + 1:CONV_K + 2, :]).astype(o_ref.dtype)


def _conv_slabs(p3, slab0, nslab, wpack, dims, sb):
    b, seq, ctxl = dims
    n = b * (seq + ctxl)
    tt = min(256, ctxl)
    t8 = tt // SUBLANES
    nblk8 = n // SUBLANES
    s0 = slab0 // sb
    kern = functools.partial(_conv_kernel, tt=tt, tiles_lat=seq // tt, n_lat_tiles=b * seq // tt,
                             tiles_ctx=ctxl // tt)
    return pl.pallas_call(
        kern,
        grid=(nslab // sb, n // tt),
        in_specs=[pl.BlockSpec((sb, SUBLANES, LANES), lambda s, i: (s0 + s, jnp.maximum(i * t8 - 1, 0), 0)),
                  pl.BlockSpec((sb, tt, LANES), lambda s, i: (s0 + s, i, 0)),
                  pl.BlockSpec((sb, SUBLANES, LANES),
                               lambda s, i: (s0 + s, jnp.minimum((i + 1) * t8, nblk8 - 1), 0)),
                  pl.BlockSpec((sb, SUBLANES, LANES), lambda s, i: (s, 0, 0))],
        out_specs=pl.BlockSpec((sb, tt, LANES), lambda s, i: (s, i, 0)),
        out_shape=jax.ShapeDtypeStruct((nslab, n, LANES), BF16),
        compiler_params=_cparams(("parallel", "parallel")),
        name="conv",
    )(p3, p3, p3, wpack)


def _conv_pack(conv_w, conv_b, scale):
    c = conv_w.shape[1]
    rows = jnp.concatenate([conv_w, conv_b[None], scale[None], jnp.zeros((1, c), F32)], axis=0)
    return rows.reshape(SUBLANES, c // LANES, LANES).transpose(1, 0, 2)


def _chunk_maps(dims):
    b, seq, ctxl = dims
    ncc, nlc = ctxl // CHUNK, seq // CHUNK

    def fwd(bi, i):
        return jnp.where(i < ncc, b * nlc + bi * ncc + i, bi * nlc + (i - ncc))

    def bwd(bi, i):
        return jnp.where(i < ncc, b * nlc + bi * ncc + (ncc - 1 - i), bi * nlc + (nlc - 1 - (i - ncc)))

    return fwd, bwd, ncc + nlc


def _mlstm_kernel(qkf_ref, vf_ref, gcf_ref, grf_ref, qkb_ref, vb_ref, gcb_ref, grb_ref, bc_ref, br_ref,
                  of_ref, ob_ref, c_ref, n_ref, m_ref):
    @pl.when(pl.program_id(1) == 0)
    def _():
        c_ref[...] = jnp.zeros_like(c_ref)
        n_ref[...] = jnp.zeros_like(n_ref)
        m_ref[...] = jnp.zeros_like(m_ref)

    ng = 2 * ML_H
    for d, (qk_ref, v_ref, gc_ref, gr_ref, o_ref) in enumerate(
            ((qkf_ref, vf_ref, gcf_ref, grf_ref, of_ref), (qkb_ref, vb_ref, gcb_ref, grb_ref, ob_ref))):
        rev = d == 1
        mask = _tri(rev)
        gcol = gc_ref[0][:, :2 * ng] + bc_ref[...]
        grow = gr_ref[...] + br_ref[...]
        ic_all = gcol[:, d * ML_H:(d + 1) * ML_H]
        lfc_all = jax.nn.log_sigmoid(gcol[:, ng + d * ML_H:ng + (d + 1) * ML_H])
        ir_all = grow[d * ML_H:(d + 1) * ML_H, :]
        lfr_all = jax.nn.log_sigmoid(grow[ng + d * ML_H:ng + (d + 1) * ML_H, :])
        incl = mask.astype(F32)
        bcol_all = _dot_hi(incl, lfc_all)
        brow_all = _dot_hi(lfr_all, _tri(not rev).astype(F32))
        last = 0 if rev else CHUNK - 1
        for h in range(ML_H):
            q = qk_ref[h]
            k = qk_ref[ML_H + h]
            v = jnp.concatenate([v_ref[2 * h], v_ref[2 * h + 1]], axis=1)
            bcol, icol = bcol_all[:, h:h + 1], ic_all[:, h:h + 1]
            brow, irow = brow_all[h:h + 1, :], ir_all[h:h + 1, :]
            cst = c_ref[d, h]
            nst = n_ref[d, h]
            mprev = m_ref[d, h][:, :1]
            logd = jnp.where(mask, bcol - brow + irow, -jnp.inf)
            inter = bcol + mprev
            m_t = jnp.maximum(inter, jnp.max(logd, axis=1, keepdims=True))
            s = _dot_nt(q, k) * jnp.exp(logd - m_t)
            sc = jnp.exp(inter - m_t)
            num = _dot(s, v) + sc * _dot_nt(q, cst)
            den = jnp.sum(s, axis=1, keepdims=True) + sc * jnp.sum(q * nst, axis=1, keepdims=True)
            hout = num / jnp.maximum(jnp.abs(den), jnp.exp(-m_t))
            o_ref[2 * h] = hout[:, :LANES]
            o_ref[2 * h + 1] = hout[:, LANES:]
            b_last = bcol[last:last + 1, :]
            wlog = b_last - bcol + icol
            m_new = jnp.maximum(b_last + mprev, jnp.max(wlog, axis=0, keepdims=True))
            w = jnp.exp(wlog - m_new)
            dec = jnp.exp(b_last + mprev - m_new)
            c_ref[d, h] = dec * cst + _dot_tn(w * v, k)
            n_ref[d, h] = dec * nst + jnp.sum(w * k, axis=0, keepdims=True)
            m_ref[d, h] = jnp.broadcast_to(m_new, (1, LANES))


def _mlstm(qk, p3, grow, bcol, brow, dims):
    b = dims[0]
    n = qk.shape[1]
    fwd, bwd, nch = _chunk_maps(dims)

    def specs(cm):
        return [pl.BlockSpec((2 * ML_H, CHUNK, LANES), lambda bi, i: (0, cm(bi, i), 0)),
                pl.BlockSpec((2 * ML_H, CHUNK, LANES), lambda bi, i: (AB_S_V // (2 * ML_H), cm(bi, i), 0)),
                pl.BlockSpec((1, CHUNK, LANES), lambda bi, i: (AB_S_GATE, cm(bi, i), 0)),
                pl.BlockSpec((None, 4 * ML_H, CHUNK), lambda bi, i: (cm(bi, i), 0, 0))]

    out_sd = jax.ShapeDtypeStruct((2 * ML_H, n, LANES), F32)
    return pl.pallas_call(
        _mlstm_kernel,
        grid=(b, nch),
        in_specs=specs(fwd) + specs(bwd) + [pl.BlockSpec((1, 4 * ML_H), lambda bi, i: (0, 0)),
                                            pl.BlockSpec((4 * ML_H, 1), lambda bi, i: (0, 0))],
        out_specs=[pl.BlockSpec((2 * ML_H, CHUNK, LANES), lambda bi, i: (0, fwd(bi, i), 0)),
                   pl.BlockSpec((2 * ML_H, CHUNK, LANES), lambda bi, i: (0, bwd(bi, i), 0))],
        out_shape=[out_sd, out_sd],
        scratch_shapes=[pltpu.VMEM((2, ML_H, ML_DV, ML_DK), F32),
                        pltpu.VMEM((2, ML_H, 1, ML_DK), F32),
                        pltpu.VMEM((2, ML_H, 1, LANES), F32)],
        compiler_params=_cparams(("parallel", "arbitrary")),
        name="mlstm",
    )(qk, p3, p3, grow, qk, p3, p3, grow, bcol, brow)


def _bcast_rows(a, rows, span):
    parts = [jnp.broadcast_to(a[r:r + 1, :], (span, a.shape[1])) for r in rows]
    return parts[0] if len(parts) == 1 else jnp.concatenate(parts, axis=0)


def _hgrn2_kernel(qf_ref, ff_ref, vf_ref, qb_ref, fb_ref, vb_ref, lb_ref, of_ref, ob_ref, s_ref):
    @pl.when(pl.program_id(1) == 0)
    def _():
        s_ref[...] = jnp.zeros_like(s_ref)

    t = lax.broadcasted_iota(jnp.int32, (CHUNK, CHUNK), 0)
    s = lax.broadcasted_iota(jnp.int32, (CHUNK, CHUNK), 1)
    refs = ((qf_ref, ff_ref, vf_ref, of_ref), (qb_ref, fb_ref, vb_ref, ob_ref))
    units = [(d, h) for d in range(2) for h in range(HG_H)]
    levels = (32, 16, 8)

    def level_mask(m, rev):
        same = (t // (2 * m)) == (s // (2 * m))
        t_late = ((t // m) % 2 == 0) if rev else ((t // m) % 2 == 1)
        s_early = ((s // m) % 2 == 1) if rev else ((s // m) % 2 == 0)
        return same & t_late & s_early

    p1 = {}
    for d, h in units:
        q_ref, f_ref, v_ref, _ = refs[d]
        lb = lb_ref[h]
        f = lb + (1.0 - lb) * jax.nn.sigmoid(f_ref[h])
        lg = jnp.log(f)
        a = _dot_hi(_tri(d == 1).astype(F32), lg)
        p1[d, h] = (q_ref[h], 1.0 - f, lg, v_ref[h], a, s_ref[d, h])
    p2 = {}
    for d, h in units:
        rev = d == 1
        q, k, lg, v, a, st = p1[d, h]
        last = 0 if rev else CHUNK - 1
        qs, ks = [], []
        for m in levels:
            nb = CHUNK // (2 * m)
            aref = _bcast_rows(a, [bi * 2 * m + (m if rev else m - 1) for bi in range(nb)], 2 * m)
            qs.append((q * jnp.exp(jnp.minimum(a - aref, 0.0))).astype(BF16))
            ks.append((k * jnp.exp(jnp.minimum(aref - a, 0.0))).astype(BF16))
        aref = _bcast_rows(a - lg, [bi * SUBLANES + (SUBLANES - 1 if rev else 0) for bi in range(CHUNK // SUBLANES)],
                           SUBLANES)
        qs.append((q * jnp.exp(a - aref)).astype(BF16))
        ks.append((k * jnp.exp(aref - a)).astype(BF16))
        a_last = a[last:last + 1, :]
        p2[d, h] = (qs, ks, (q * jnp.exp(a)).astype(BF16), (k * jnp.exp(a_last - a)).astype(BF16), jnp.exp(a_last))
    p3 = {}
    for d, h in units:
        qs, ks, qdec, kdec, edec = p2[d, h]
        st = p1[d, h][5]
        p3[d, h] = ([_dot_nt(qq, kk) for qq, kk in zip(qs, ks)], _dot_nt(qdec, st))
    for d, h in units:
        rev = d == 1
        prods, qst = p3[d, h]
        scores = jnp.zeros((CHUNK, CHUNK), F32)
        for m, pr in zip(levels, prods[:-1]):
            scores = scores + jnp.where(level_mask(m, rev), pr, 0.0)
        diag = ((t // SUBLANES) == (s // SUBLANES)) & _tri(rev)
        scores = scores + jnp.where(diag, prods[-1], 0.0)
        refs[d][3][h] = _dot(scores, p1[d, h][3]) + qst
    for d, h in units:
        v, st = p1[d, h][3], p1[d, h][5]
        kdec, edec = p2[d, h][3], p2[d, h][4]
        s_ref[d, h] = st * edec + _dot_tn(v, kdec)


def _hgrn2(p3, lb, dims):
    b = dims[0]
    n = b * (dims[1] + dims[2])
    fwd, bwd, nch = _chunk_maps(dims)

    def specs(cm, d):
        return [pl.BlockSpec((HG_H, CHUNK, LANES), lambda bi, i: (AB_S_HQ // HG_H, cm(bi, i), 0)),
                pl.BlockSpec((HG_H, CHUNK, LANES), lambda bi, i: (AB_S_HF // HG_H + d, cm(bi, i), 0)),
                pl.BlockSpec((HG_H, CHUNK, LANES), lambda bi, i: (AB_S_HI // HG_H, cm(bi, i), 0))]

    out_sd = jax.ShapeDtypeStruct((HG_H, n, LANES), F32)
    return pl.pallas_call(
        _hgrn2_kernel,
        grid=(b, nch),
        in_specs=specs(fwd, 0) + specs(bwd, 1) + [pl.BlockSpec((HG_H, 1, LANES), lambda bi, i: (0, 0, 0))],
        out_specs=[pl.BlockSpec((HG_H, CHUNK, LANES), lambda bi, i: (0, fwd(bi, i), 0)),
                   pl.BlockSpec((HG_H, CHUNK, LANES), lambda bi, i: (0, bwd(bi, i), 0))],
        out_shape=[out_sd, out_sd],
        scratch_shapes=[pltpu.VMEM((2, HG_H, HG_DV, HG_DK), F32)],
        compiler_params=_cparams(("parallel", "arbitrary")),
        name="hgrn2",
    )(p3, p3, p3, p3, p3, p3, lb)


def _ab_out_kernel(mf_ref, mb_ref, hf_ref, hb_ref, og_ref, hg_ref, mlg_ref, hgg_ref, w_ref, x_ref, g1_ref,
                   o_ref, lhs_ref):
    for h in range(ML_H):
        hs = jnp.concatenate([mf_ref[2 * h] + mb_ref[2 * h], mf_ref[2 * h + 1] + mb_ref[2 * h + 1]], axis=1)
        r = hs * lax.rsqrt(jnp.mean(hs * hs, axis=-1, keepdims=True) + EPS)
        og = jnp.concatenate([og_ref[2 * h], og_ref[2 * h + 1]], axis=1)
        y = jax.nn.sigmoid(og) * (r * mlg_ref[:, h * ML_DV:(h + 1) * ML_DV])
        lhs_ref[:, h * ML_DV:(h + 1) * ML_DV] = y.astype(BF16)
    for h in range(HG_H):
        hs = hf_ref[h] + hb_ref[h]
        r = hs * lax.rsqrt(jnp.mean(hs * hs, axis=-1, keepdims=True) + EPS)
        y = _silu(hg_ref[h]) * (r * hgg_ref[:, h * HG_DV:(h + 1) * HG_DV])
        lhs_ref[:, ML_V + h * HG_DV:ML_V + (h + 1) * HG_DV] = y.astype(BF16)
    acc = jnp.dot(lhs_ref[...], w_ref[...], preferred_element_type=F32)
    o_ref[...] = x_ref[...] + g1_ref[...] * acc


def _ab_out(hm, ho, p3, ml_g, hg_g, w_out, x, g1, seg_fn, n, tm):
    d = x.shape[1]
    slab8 = lambda idx: pl.BlockSpec((SUBLANES, tm, LANES), lambda i: (idx, i, 0))
    return pl.pallas_call(
        _ab_out_kernel,
        grid=(n // tm,),
        in_specs=[slab8(0), slab8(0), slab8(0), slab8(0), slab8(AB_S_OG // SUBLANES), slab8(AB_S_HG // SUBLANES),
                  pl.BlockSpec((1, ML_V), lambda i: (0, 0)),
                  pl.BlockSpec((1, HG_V), lambda i: (0, 0)),
                  pl.BlockSpec((AB_OUT, d), lambda i: (0, 0)),
                  pl.BlockSpec((tm, d), lambda i: (i, 0)),
                  pl.BlockSpec((None, 1, d), lambda i: (seg_fn(i * tm), 0, 0))],
        out_specs=pl.BlockSpec((tm, d), lambda i: (i, 0)),
        out_shape=jax.ShapeDtypeStruct((n, d), F32),
        scratch_shapes=[pltpu.VMEM((tm, AB_OUT), BF16)],
        compiler_params=_cparams(("parallel",)),
        name="ab_out",
    )(hm[0], hm[1], ho[0], ho[1], p3, p3, ml_g.reshape(1, ML_V), hg_g.reshape(1, HG_V), w_out, x, g1)


def _ab_weight(w_in):
    d = w_in.shape[0]
    o = [0, 2 * ML_QK, 2 * ML_QK + ML_V, 2 * ML_QK + 2 * ML_V]
    g0 = o[3]
    h0 = g0 + 4 * ML_H
    pad = AB_SLABS * LANES - (w_in.shape[1])
    return jnp.concatenate([w_in[:, :g0], w_in[:, h0:], w_in[:, g0:h0], jnp.zeros((d, pad), w_in.dtype)],
                           axis=1).astype(BF16)


def _ab_layer(x, dims, seg_fn, tm, n_out, g_norm, sc, sh, g1, w_in, conv_w, conv_b, ig_b, fg_b, lb, ml_g, hg_g,
              w_out):
    b, seq, ctxl = dims
    n = b * (seq + ctxl)
    p3 = _inproj(x, g_norm, sc, sh, _ab_weight(w_in), seg_fn, min(2 * tm, seq), 11 * LANES)
    kscale = jnp.concatenate([jnp.ones((ML_QK,), F32), jnp.full((ML_QK,), ML_DK ** -0.5, F32)])
    qk = _conv_slabs(p3, AB_S_Q, 2 * ML_H, _conv_pack(conv_w, conv_b, kscale), dims, 2 * ML_H)
    gates = p3[AB_S_GATE, :n, :4 * ML_H]
    grow = gates.reshape(n // CHUNK, CHUNK, 4 * ML_H).transpose(0, 2, 1)
    gbias = jnp.concatenate([ig_b.reshape(-1), fg_b.reshape(-1)])
    hm = _mlstm(qk, p3, grow, gbias.reshape(1, -1), gbias.reshape(-1, 1), dims)
    ho = _hgrn2(p3, lb.reshape(HG_H, 1, HG_DK), dims)
    return _ab_out(hm, ho, p3, ml_g, hg_g, w_out.astype(BF16), x, g1, seg_fn, n_out, min(tm, 256))


def _ssd_kernel(xf_ref, dcf_ref, drf_ref, xb_ref, dcb_ref, drb_ref, bc_ref, br_ref, ac_ref, ar_ref,
                of_ref, ob_ref, h_ref):
    @pl.when(pl.program_id(1) == 0)
    def _():
        h_ref[...] = jnp.zeros_like(h_ref)

    gw = SSD_R * SSD_P
    t_idx = lax.broadcasted_iota(jnp.int32, (CHUNK, gw), 0)
    s_idx = lax.broadcasted_iota(jnp.int32, (CHUNK, gw), 1) % SSD_P
    e_r = lax.broadcasted_iota(jnp.int32, (3 * SSD_R, gw), 0) % SSD_R
    e_c = lax.broadcasted_iota(jnp.int32, (3 * SSD_R, gw), 1) // SSD_P
    expand3 = (e_r == e_c).astype(BF16)
    low_half = lax.broadcasted_iota(jnp.int32, (CHUNK, LANES), 1) < SSD_P

    def expand(v):
        hi = v.astype(BF16).astype(F32)
        mid = (v - hi).astype(BF16).astype(F32)
        lo = v - hi - mid
        return jnp.dot(jnp.concatenate([hi, mid, lo], axis=1).astype(BF16), expand3, preferred_element_type=F32)

    refs = ((xf_ref, dcf_ref, drf_ref, of_ref), (xb_ref, dcb_ref, drb_ref, ob_ref))
    units = [(d, g) for d in range(2) for g in range(SSD_G)]
    gate = []
    for d, (x_ref, dc_ref, dr_ref, o_ref) in enumerate(refs):
        rev = d == 1
        mask = (s_idx >= t_idx) if rev else (s_idx <= t_idx)
        last = 0 if rev else CHUNK - 1
        hs = slice(d * SSD_H, (d + 1) * SSD_H)
        dtc = jax.nn.softplus(dc_ref[0][:, hs] + bc_ref[:, hs])
        lac = dtc * ac_ref[:, hs]
        dtr = jax.nn.softplus(dr_ref[hs, :] + br_ref[hs, :])
        lar = dtr * ar_ref[hs, :]
        cum_c = _dot_hi(_tri(rev).astype(F32), lac)
        cum_r = _dot_hi(lar, _tri(not rev).astype(F32))
        wgt = jnp.exp(cum_c[last:last + 1, :] - cum_c) * dtc
        gate.append((mask, last, cum_c, cum_r, dtr, wgt))
    p1 = {}
    for d, g in units:
        x_ref = refs[d][0]
        mask, last, cum_c, cum_r, dtr, wgt = gate[d]
        heads = slice(g * SSD_R, (g + 1) * SSD_R)
        x = jnp.concatenate([x_ref[4 * g + j] for j in range(4)], axis=1)
        bm = x_ref[4 * SSD_G + g]
        cm = x_ref[5 * SSD_G + g]
        hst = h_ref[d, g]
        p1[d, g] = (x, bm, hst, expand(cum_c[:, heads]), expand(wgt[:, heads]), _dot_nt(cm, bm), _dot(cm, hst))
    p2 = {}
    for d, g in units:
        mask, last, cum_c, cum_r, dtr, wgt = gate[d]
        x, bm, hst, cum_x, wgt_x, cb, yoff = p1[d, g]
        rows = lambda a: jnp.concatenate([a[g * SSD_R + r:g * SSD_R + r + 1, :] for r in range(SSD_R)], axis=1)
        seg = jnp.exp(jnp.where(mask, cum_x - rows(cum_r), -jnp.inf)) * rows(dtr)
        cb2 = jnp.concatenate([cb, cb], axis=1)
        ms = [(cb2 * seg[:, j * LANES:(j + 1) * LANES]).astype(BF16) for j in range(4)]
        p2[d, g] = (ms, jnp.exp(cum_x), (x * wgt_x).astype(BF16))
    for d, g in units:
        o_ref = refs[d][3]
        x, yoff = p1[d, g][0], p1[d, g][6]
        ms, ecum_x, xw = p2[d, g]
        for j in range(4):
            ls = slice(j * LANES, (j + 1) * LANES)
            xp = x[:, ls]
            bd = jnp.concatenate([jnp.where(low_half, xp, 0.0), jnp.where(low_half, 0.0, xp)], axis=0)
            o_ref[4 * g + j] = (_dot(ms[j], bd) + ecum_x[:, ls] * yoff[:, ls]).astype(o_ref.dtype)
    for d, g in units:
        last = gate[d][1]
        bm, hst = p1[d, g][1], p1[d, g][2]
        ms, ecum_x, xw = p2[d, g]
        h_ref[d, g] = hst * ecum_x[last:last + 1, :] + _dot_tn(bm, xw)


def _ssd_scan(xbc, p3, dtrow, dt_b, neg_a, dims):
    b = dims[0]
    n = xbc.shape[1]
    fwd, bwd, nch = _chunk_maps(dims)
    nxs = xbc.shape[0]

    def specs(cm):
        return [pl.BlockSpec((nxs, CHUNK, LANES), lambda bi, i: (0, cm(bi, i), 0)),
                pl.BlockSpec((1, CHUNK, LANES), lambda bi, i: (SSD_S_DT, cm(bi, i), 0)),
                pl.BlockSpec((None, 2 * SSD_H, CHUNK), lambda bi, i: (cm(bi, i), 0, 0))]

    vec = lambda shape: pl.BlockSpec(shape, lambda bi, i: (0, 0))
    out_sd = jax.ShapeDtypeStruct((D_INNER // LANES, n, LANES), BF16)
    return pl.pallas_call(
        _ssd_kernel,
        grid=(b, nch),
        in_specs=specs(fwd) + specs(bwd) + [vec((1, 2 * SSD_H)), vec((2 * SSD_H, 1)),
                                            vec((1, 2 * SSD_H)), vec((2 * SSD_H, 1))],
        out_specs=[pl.BlockSpec((D_INNER // LANES, CHUNK, LANES), lambda bi, i: (0, fwd(bi, i), 0)),
                   pl.BlockSpec((D_INNER // LANES, CHUNK, LANES), lambda bi, i: (0, bwd(bi, i), 0))],
        out_shape=[out_sd, out_sd],
        scratch_shapes=[pltpu.VMEM((2, SSD_G, SSD_N, SSD_R * SSD_P), F32)],
        compiler_params=_cparams(("parallel", "arbitrary")),
        name="ssd_scan",
    )(xbc, p3, dtrow, xbc, p3, dtrow, dt_b.reshape(1, -1), dt_b.reshape(-1, 1),
      neg_a.reshape(1, -1), neg_a.reshape(-1, 1))


def _ssd_out_kernel(yf_ref, yb_ref, xs_ref, z_ref, dsk_ref, ng_ref, w_ref, x_ref, g1_ref, o_ref, lhs_ref, acc_ref):
    k = pl.program_id(1)

    @pl.when(k == 0)
    def _():
        acc_ref[...] = jnp.zeros_like(acc_ref)

    gw = D_INNER // SSD_G
    for gg in range(lhs_ref.shape[1] // gw):
        cat = lambda ref: jnp.concatenate([ref[4 * gg + j] for j in range(4)], axis=1).astype(F32)
        cs = slice(gg * gw, (gg + 1) * gw)
        y = cat(yf_ref) + cat(yb_ref) + dsk_ref[:, cs] * cat(xs_ref)
        u = y * _silu(cat(z_ref))
        u = u * lax.rsqrt(jnp.mean(u * u, axis=-1, keepdims=True) + EPS) * ng_ref[:, cs]
        lhs_ref[:, cs] = u.astype(BF16)
    acc_ref[...] += jnp.dot(lhs_ref[...], w_ref[...], preferred_element_type=F32)

    @pl.when(k == pl.num_programs(1) - 1)
    def _():
        o_ref[...] = x_ref[...] + g1_ref[...] * acc_ref[...]


def _ssd_out(yf, yb, xbc, p3, dskip, norm_g, w_out, x, g1, seg_fn, n_rows, tm):
    d = x.shape[1]
    tk = 1024
    nsl = tk // LANES
    slab = lambda: pl.BlockSpec((nsl, tm, LANES), lambda i, k: (k, i, 0))
    return pl.pallas_call(
        _ssd_out_kernel,
        grid=(n_rows // tm, D_INNER // tk),
        in_specs=[slab(), slab(), slab(), slab(),
                  pl.BlockSpec((1, tk), lambda i, k: (0, k)),
                  pl.BlockSpec((1, tk), lambda i, k: (0, k)),
                  pl.BlockSpec((tk, d), lambda i, k: (k, 0)),
                  pl.BlockSpec((tm, d), lambda i, k: (i, 0)),
                  pl.BlockSpec((None, 1, d), lambda i, k: (seg_fn(i * tm), 0, 0))],
        out_specs=pl.BlockSpec((tm, d), lambda i, k: (i, 0)),
        out_shape=jax.ShapeDtypeStruct((n_rows, d), F32),
        scratch_shapes=[pltpu.VMEM((tm, tk), BF16), pltpu.VMEM((tm, d), F32)],
        compiler_params=_cparams(("parallel", "arbitrary")),
        name="ssd_out",
    )(yf, yb, xbc, p3, dskip, norm_g.reshape(1, -1), w_out, x, g1)


def _ssd_layer(x, dims, seg_fn, tm, n_out, g_norm, sc, sh, g1, w_in, conv_w, conv_b, dt_b, a_log, d_skip,
               norm_g, w_out):
    b, seq, ctxl = dims
    n = b * (seq + ctxl)
    p3 = _inproj(x, g_norm, sc, sh, w_in.astype(BF16), seg_fn, min(2 * tm, seq), 9 * LANES)
    nconv = conv_w.shape[1]
    xbc = _conv_slabs(p3, SSD_S_X, nconv // LANES, _conv_pack(conv_w, conv_b, jnp.ones((nconv,), F32)), dims, 16)
    dt = p3[SSD_S_DT, :n]
    dtrow = dt.reshape(n // CHUNK, CHUNK, 2 * SSD_H).transpose(0, 2, 1)
    yf, yb = _ssd_scan(xbc, p3, dtrow, dt_b.reshape(-1), -jnp.exp(a_log.astype(F32)).reshape(-1), dims)
    dskip = jnp.repeat(d_skip.astype(F32), SSD_P).reshape(1, D_INNER)
    return _ssd_out(yf, yb, xbc, p3, dskip, norm_g, w_out.astype(BF16), x, g1, seg_fn, n_out, tm)


def _regroup_kernel(x_ref, o_ref):
    a = x_ref.shape[0]
    for cc in range(SUBLANES):
        o_ref[cc * a:(cc + 1) * a, :] = x_ref[:, cc, :]


def _regroup(x, b, a, c, n_out):
    d = x.shape[1]
    cblk = c // SUBLANES
    return pl.pallas_call(
        _regroup_kernel,
        grid=(b, cblk),
        in_specs=[pl.BlockSpec((a, SUBLANES, d), lambda bi, j: (bi, j, 0))],
        out_specs=pl.BlockSpec((a * SUBLANES, d), lambda bi, j: (bi * cblk + j, 0)),
        out_shape=jax.ShapeDtypeStruct((n_out, d), F32),
        compiler_params=_cparams(("parallel", "parallel")),
        name="regroup",
    )(x.reshape(x.shape[0] // c, c, d))


MOE_TILE = 512


def _router_kernel(x_ref, g_ref, sc_ref, sh_ref, wt_ref, rb_ref, hn_ref, ii_ref, iw_ref, cnt_ref, carry_ref):
    @pl.when(pl.program_id(0) == 0)
    def _():
        carry_ref[...] = jnp.zeros_like(carry_ref)

    x = x_ref[...]
    tm = x.shape[0]
    y = x * lax.rsqrt(jnp.mean(x * x, axis=-1, keepdims=True) + EPS) * g_ref[...]
    hn = y * (1.0 + sc_ref[...]) + sh_ref[...]
    _store_token_rows(hn_ref, hn)
    logits = lax.dot_general(wt_ref[...], hn, (((1,), (1,)), ((), ())), precision=HI,
                             preferred_element_type=F32)
    score = jax.nn.sigmoid(logits)
    biased = score + rb_ref[...]
    rb = [biased[e:e + 1, :] for e in range(N_EXPERTS)]
    rs = [score[e:e + 1, :] for e in range(N_EXPERTS)]
    gsc = []
    for g in range(N_GROUPS):
        a, b, c, d = rb[EXPERTS_PER_GROUP * g:EXPERTS_PER_GROUP * (g + 1)]
        hi1, lo1, hi2, lo2 = jnp.maximum(a, b), jnp.minimum(a, b), jnp.maximum(c, d), jnp.minimum(c, d)
        gsc.append(jnp.maximum(hi1, hi2) + jnp.maximum(jnp.minimum(hi1, hi2), jnp.maximum(lo1, lo2)))
    best = jnp.zeros((1, tm), jnp.int32)
    bsc = gsc[0]
    for g in range(1, N_GROUPS):
        upd = gsc[g] > bsc
        best = jnp.where(upd, g, best)
        bsc = jnp.where(upd, gsc[g], bsc)

    def pick(rows, p):
        out = rows[p]
        for g in range(1, N_GROUPS):
            out = jnp.where(best == g, rows[EXPERTS_PER_GROUP * g + p], out)
        return out

    vals = [pick(rb, p) for p in range(EXPERTS_PER_GROUP)]
    scs = [pick(rs, p) for p in range(EXPERTS_PER_GROUP)]
    p1, v1, s1 = jnp.zeros((1, tm), jnp.int32), vals[0], scs[0]
    for p in range(1, EXPERTS_PER_GROUP):
        upd = vals[p] > v1
        p1, v1, s1 = jnp.where(upd, p, p1), jnp.where(upd, vals[p], v1), jnp.where(upd, scs[p], s1)
    p2 = jnp.zeros((1, tm), jnp.int32)
    v2 = jnp.full((1, tm), -jnp.inf, F32)
    s2 = jnp.zeros((1, tm), F32)
    for p in range(EXPERTS_PER_GROUP):
        upd = (p1 != p) & (vals[p] > v2)
        p2, v2, s2 = jnp.where(upd, p, p2), jnp.where(upd, vals[p], v2), jnp.where(upd, scs[p], s2)
    e1 = best * EXPERTS_PER_GROUP + p1
    e2 = best * EXPERTS_PER_GROUP + p2
    tot = s1 + s2
    eiota = lax.broadcasted_iota(jnp.int32, (N_EXPERTS, tm), 0)
    oh1 = (eiota == e1).astype(F32)
    oh2 = (eiota == e2).astype(F32)
    oh = oh1 + oh2
    before = (lax.broadcasted_iota(jnp.int32, (tm, tm), 0) < lax.broadcasted_iota(jnp.int32, (tm, tm), 1))
    cnt = _dot(oh, before.astype(BF16)) + carry_ref[:, :1]
    r1 = jnp.sum(oh1 * cnt, axis=0, keepdims=True).astype(jnp.int32)
    r2 = jnp.sum(oh2 * cnt, axis=0, keepdims=True).astype(jnp.int32)
    zi = jnp.zeros((SUBLANES - 4, tm), jnp.int32)
    ii_ref[...] = jnp.concatenate([e1, e2, r1, r2, zi], axis=0)
    iw_ref[...] = jnp.concatenate([s1 / tot, s2 / tot, jnp.zeros((SUBLANES - 2, tm), F32)], axis=0)
    carry = carry_ref[...] + jnp.sum(oh, axis=1, keepdims=True)
    carry_ref[...] = carry
    cnt_ref[...] = carry


def _router(x, n_rows, g, sc, sh, seg_fn, router_wt, router_b):
    d = x.shape[1]
    tm = MOE_TILE
    return pl.pallas_call(
        _router_kernel,
        grid=(n_rows // tm,),
        in_specs=[pl.BlockSpec((tm, d), lambda i: (i, 0)),
                  pl.BlockSpec((1, d), lambda i: (0, 0)),
                  pl.BlockSpec((None, 1, d), lambda i: (seg_fn(i * tm), 0, 0)),
                  pl.BlockSpec((None, 1, d), lambda i: (seg_fn(i * tm), 0, 0)),
                  pl.BlockSpec((N_EXPERTS, d), lambda i: (0, 0)),
                  pl.BlockSpec((N_EXPERTS, 1), lambda i: (0, 0))],
        out_specs=[pl.BlockSpec((tm, d // LANES, LANES), lambda i: (i, 0, 0)),
                   pl.BlockSpec((SUBLANES, tm), lambda i: (0, i)),
                   pl.BlockSpec((SUBLANES, tm), lambda i: (0, i)),
                   pl.BlockSpec((N_EXPERTS, LANES), lambda i: (0, 0))],
        out_shape=[jax.ShapeDtypeStruct((n_rows, d // LANES, LANES), F32),
                   jax.ShapeDtypeStruct((SUBLANES, n_rows), jnp.int32),
                   jax.ShapeDtypeStruct((SUBLANES, n_rows), F32),
                   jax.ShapeDtypeStruct((N_EXPERTS, LANES), F32)],
        scratch_shapes=[pltpu.VMEM((N_EXPERTS, LANES), F32)],
        compiler_params=_cparams(("arbitrary",)),
        name="router",
    )(x, g.reshape(1, d), sc, sh, router_wt, router_b.reshape(N_EXPERTS, 1))


TOK_SUB = D_MODEL // LANES


def _store_token_rows(ref, val):
    for s in range(TOK_SUB):
        ref[:, s, :] = val[:, s * LANES:(s + 1) * LANES]


def _row_copy(src_ref, o_ref, sem, src_row, dst_row):
    return pltpu.make_async_copy(src_ref.at[pl.ds(src_row, 1)], o_ref.at[pl.ds(dst_row, 1)], sem)


def _expert_kernel(te_ref, nu_ref, xn_ref, w1_ref, w3_ref, w2_ref, o_ref, xb_ref, acc_ref, *, nf):
    i, f = pl.program_id(0), pl.program_id(1)
    compute = (i >= 1) & (i - 1 < nu_ref[0])
    part = xn_ref.shape[0] // nf

    def convert(fv):
        rows = slice(fv * part, (fv + 1) * part)
        for s in range(TOK_SUB):
            xb_ref[i % 2, rows, s * LANES:(s + 1) * LANES] = xn_ref[rows, s, :].astype(BF16)

    for fv in range(nf):
        @pl.when((f == fv) & (i == 0))
        def _(fv=fv):
            convert(fv)

        @pl.when((f == fv) & compute)
        def _(fv=fv):
            xb = xb_ref[(i - 1) % 2]
            a = _silu(_dot(xb, w1_ref[...])) * _dot(xb, w3_ref[...])
            y = _dot(a, w2_ref[...])
            convert(fv)
            if fv > 0:
                y = acc_ref[...] + y
            if fv < nf - 1:
                acc_ref[...] = y
            else:
                _store_token_rows(o_ref, y)

    @pl.when((f == nf - 1) & (i >= 1) & jnp.logical_not(compute))
    def _():
        o_ref[...] = jnp.zeros_like(o_ref)


def _experts(tile_e, n_used, xs, w1, w3, w2, layer):
    p = xs.shape[0]
    d = D_MODEL
    te, tf = MOE_TILE, 512
    nf = D_EXPERT // tf
    ntile = p // te
    tile = lambda i: jnp.maximum(i - 1, 0)
    fidx = lambda i, f, nu: jnp.where((i >= 1) & (i - 1 < nu[0]), f, nf - 1)
    grid_spec = pltpu.PrefetchScalarGridSpec(
        num_scalar_prefetch=2,
        grid=(ntile + 1, nf),
        in_specs=[pl.BlockSpec((te, TOK_SUB, LANES), lambda i, f, te_r, nu: (jnp.minimum(i, ntile - 1), 0, 0)),
                  pl.BlockSpec((None, None, d, tf),
                               lambda i, f, te_r, nu: (layer, te_r[tile(i)], 0, fidx(i, f, nu))),
                  pl.BlockSpec((None, None, d, tf),
                               lambda i, f, te_r, nu: (layer, te_r[tile(i)], 0, fidx(i, f, nu))),
                  pl.BlockSpec((None, None, tf, d),
                               lambda i, f, te_r, nu: (layer, te_r[tile(i)], fidx(i, f, nu), 0))],
        out_specs=pl.BlockSpec((te, TOK_SUB, LANES), lambda i, f, te_r, nu: (tile(i), 0, 0)),
        scratch_shapes=[pltpu.VMEM((2, te, d), BF16), pltpu.VMEM((te, d), F32)])
    return pl.pallas_call(
        functools.partial(_expert_kernel, nf=nf),
        grid_spec=grid_spec,
        out_shape=jax.ShapeDtypeStruct((p, TOK_SUB, LANES), F32),
        compiler_params=_cparams(("arbitrary", "arbitrary")),
        name="experts",
    )(tile_e, n_used, xs, w1, w3, w2)


def _combine_kernel(p1_ref, p2_ref, q1_ref, q2_ref, ys_ref, x_ref, w_ref, g2_ref, fg_ref, o_ref, y_ref, sem, *,
                    final):
    i = pl.program_id(0)
    tm = x_ref.shape[0]
    slot = i % 2

    def fetch(pa_ref, pb_ref, sl):
        def issue(r8, carry):
            for u in range(SUBLANES):
                r = r8 * SUBLANES + u
                _row_copy(ys_ref, y_ref.at[sl, 0], sem.at[sl], pa_ref[r], r).start()
                _row_copy(ys_ref, y_ref.at[sl, 1], sem.at[sl], pb_ref[r], r).start()
            return carry

        lax.fori_loop(0, tm // SUBLANES, issue, 0)

    @pl.when(i == 0)
    def _():
        fetch(p1_ref, p2_ref, 0)

    @pl.when(i + 1 < pl.num_programs(0))
    def _():
        fetch(q1_ref, q2_ref, 1 - slot)

    for k in range(TOP_K):
        pltpu.make_async_copy(ys_ref.at[pl.ds(0, tm)], y_ref.at[slot, k], sem.at[slot]).wait()
    w = w_ref[...]
    w1 = jnp.broadcast_to(w[:, 0:1], (tm, LANES))
    w2 = jnp.broadcast_to(w[:, 1:2], (tm, LANES))
    for s in range(TOK_SUB):
        cs = slice(s * LANES, (s + 1) * LANES)
        o_ref[:, cs] = x_ref[:, cs] + g2_ref[:, cs] * (w1 * y_ref[slot, 0, :, s, :] + w2 * y_ref[slot, 1, :, s, :])
    if final:
        out = o_ref[...]
        o_ref[...] = out * lax.rsqrt(jnp.mean(out * out, axis=-1, keepdims=True) + EPS) * fg_ref[...]


def _combine(x, n_rows, pos, ys, wts, g2, seg_fn, final_g):
    d = x.shape[1]
    tm = MOE_TILE
    nt = n_rows // tm
    fg = jnp.ones((1, d), F32) if final_g is None else final_g.reshape(1, d)
    nxt = lambda i: jnp.minimum(i + 1, nt - 1)
    smem = lambda fn: pl.BlockSpec((tm,), fn, memory_space=pltpu.SMEM)
    return pl.pallas_call(
        functools.partial(_combine_kernel, final=final_g is not None),
        grid=(nt,),
        in_specs=[smem(lambda i: (i,)), smem(lambda i: (nt + i,)),
                  smem(lambda i: (nxt(i),)), smem(lambda i: (nt + nxt(i),)),
                  pl.BlockSpec(memory_space=pl.ANY),
                  pl.BlockSpec((tm, d), lambda i: (i, 0)),
                  pl.BlockSpec((tm, 2), lambda i: (i, 0)),
                  pl.BlockSpec((None, 1, d), lambda i: (seg_fn(i * tm), 0, 0)),
                  pl.BlockSpec((1, d), lambda i: (0, 0))],
        out_specs=pl.BlockSpec((tm, d), lambda i: (i, 0)),
        out_shape=jax.ShapeDtypeStruct((n_rows, d), F32),
        scratch_shapes=[pltpu.VMEM((2, TOP_K, tm, TOK_SUB, LANES), F32), pltpu.SemaphoreType.DMA((2,))],
        compiler_params=_cparams(("arbitrary",)),
        name="combine",
    )(pos, pos, pos, pos, ys, x, wts, g2, fg)


def _dispatch_kernel(p1_ref, p2_ref, ends_ref, hn_ref, xs_ref, zero_ref, sem, zsem, *, n_pos):
    tm = hn_ref.shape[0]
    zrows = zero_ref.shape[0]

    @pl.when(pl.program_id(0) == 0)
    def _():
        zero_ref[...] = jnp.zeros_like(zero_ref)
        starts = [jnp.maximum(ends_ref[e] - zrows, 0) for e in range(N_EXPERTS)]
        tail = [ends_ref[N_EXPERTS - 1] + j * zrows for j in range(N_EXPERTS)]
        for st in starts:
            pltpu.make_async_copy(zero_ref, xs_ref.at[pl.ds(st, zrows)], zsem).start()
        for st in tail:
            @pl.when(st < n_pos)
            def _(st=st):
                pltpu.make_async_copy(zero_ref, xs_ref.at[pl.ds(st, zrows)], zsem).start()
        for st in starts:
            pltpu.make_async_copy(zero_ref, xs_ref.at[pl.ds(st, zrows)], zsem).wait()
        for st in tail:
            @pl.when(st < n_pos)
            def _(st=st):
                pltpu.make_async_copy(zero_ref, xs_ref.at[pl.ds(st, zrows)], zsem).wait()

    def issue(r8, carry):
        for u in range(SUBLANES):
            r = r8 * SUBLANES + u
            _row_copy(hn_ref, xs_ref, sem, r, p1_ref[r]).start()
            _row_copy(hn_ref, xs_ref, sem, r, p2_ref[r]).start()
        return carry

    lax.fori_loop(0, tm // SUBLANES, issue, 0)
    for k in range(TOP_K):
        pltpu.make_async_copy(hn_ref, xs_ref.at[pl.ds(0, tm)], sem).wait()


def _dispatch(pos, ends, hn, n_rows, n_pos):
    tm = MOE_TILE
    nt = n_rows // tm
    return pl.pallas_call(
        functools.partial(_dispatch_kernel, n_pos=n_pos),
        grid=(nt,),
        in_specs=[pl.BlockSpec((tm,), lambda i: (i,), memory_space=pltpu.SMEM),
                  pl.BlockSpec((tm,), lambda i: (nt + i,), memory_space=pltpu.SMEM),
                  pl.BlockSpec(memory_space=pltpu.SMEM),
                  pl.BlockSpec((tm,) + hn.shape[1:], lambda i: (i, 0, 0))],
        out_specs=pl.BlockSpec(memory_space=pl.ANY),
        out_shape=jax.ShapeDtypeStruct((n_pos,) + hn.shape[1:], hn.dtype),
        scratch_shapes=[pltpu.VMEM((tm,) + hn.shape[1:], hn.dtype), pltpu.SemaphoreType.DMA,
                        pltpu.SemaphoreType.DMA],
        compiler_params=_cparams(("arbitrary",)),
        name="dispatch",
    )(pos, pos, ends, hn)


def _moe(x, n_rows, seg_fn, g, sc, sh, g2, router_wt, router_b, w1, w3, w2, layer, final_g=None):
    te = MOE_TILE
    hn, ii, iw, cnt = _router(x, n_rows, g, sc, sh, seg_fn, router_wt, router_b)
    counts = cnt[:, 0].astype(jnp.int32)
    padded = (counts + te - 1) // te * te
    ends = jnp.cumsum(padded)
    base = ends - padded
    pos = jnp.concatenate([base[ii[0]] + ii[2], base[ii[1]] + ii[3]])
    p = 2 * n_rows + N_EXPERTS * te
    ntile = p // te
    n_used = ends[-1] // te
    tiles = jnp.arange(ntile, dtype=jnp.int32)
    tile_e = jnp.sum((jnp.minimum(tiles, n_used - 1)[:, None] * te >= ends[None, :]).astype(jnp.int32), axis=1)
    xs = _dispatch(pos, ends, hn, n_rows, p)
    ys = _experts(tile_e, n_used.reshape(1).astype(jnp.int32), xs, w1, w3, w2, layer)
    return _combine(x, n_rows, pos, ys, iw[:2].T, g2, seg_fn, final_g)


def kernel(x, c, ctx, c_ctx, ada_w, ada_b, norm1_g, norm2_g, ab_w_in, ab_conv_w, ab_conv_b, ml_ig_b, ml_fg_b,
           hg_lb, ml_norm_g, hg_norm_g, ab_w_out, ssd_w_in, ssd_conv_w, ssd_conv_b, ssd_dt_b, ssd_a_log, ssd_d,
           ssd_norm_g, ssd_w_out, router_w, router_b, moe_w1, moe_w3, moe_w2, final_g):
    b, seq, d = x.shape
    ctxl = ctx.shape[1]
    depth = ada_w.shape[0]
    dims = (b, seq, ctxl)
    nl, nc = b * seq, b * ctxl
    rows = seq // GRID_W
    tm = min(512, nc)
    seg_fn = lambda row: jnp.where(row < nl, row // seq, b)
    c8 = jnp.concatenate([c, c_ctx[None], jnp.zeros((SUBLANES - b - 1, d), F32)])
    mods = _modulation(c8, ada_w, ada_b)[:, :b + 1].reshape(depth, b + 1, 6, 1, d)
    lb_all = jnp.cumsum(jax.nn.softmax(hg_lb.astype(F32), axis=0), axis=0)
    router_wt = router_w.T
    pad = jnp.zeros((-(nl + nc) % min(2 * tm, seq), d), F32)
    xr = jnp.concatenate([x.reshape(nl, d), ctx.reshape(nc, d), pad])
    transposed = False
    for l in range(depth):
        sh1, sc1, g1, sh2, sc2, g2 = (mods[l][:, k] for k in range(6))
        keep_ctx = l < depth - 1
        n_out = nl + nc if keep_ctx else nl
        j = l // 2
        if (l % 2 == 1) != transposed:
            xt = _regroup(xr, b, GRID_W if transposed else rows, rows if transposed else GRID_W, nl)
            xr = jnp.concatenate([xt, xr[nl:nl + nc], pad])
            transposed = not transposed
        if l % 2 == 0:
            xr = _ab_layer(xr, dims, seg_fn, tm, n_out, norm1_g[l], sc1, sh1, g1, ab_w_in[j], ab_conv_w[j],
                           ab_conv_b[j], ml_ig_b[j], ml_fg_b[j], lb_all[l], ml_norm_g[j], hg_norm_g[j], ab_w_out[j])
        else:
            xr = _ssd_layer(xr, dims, seg_fn, tm, n_out, norm1_g[l], sc1, sh1, g1, ssd_w_in[j], ssd_conv_w[j],
                            ssd_conv_b[j], ssd_dt_b[j], ssd_a_log[j], ssd_d[j], ssd_norm_g[j], ssd_w_out[j])
        xr = _moe(xr, n_out, seg_fn, norm2_g[l], sc2, sh2, g2, router_wt, router_b, moe_w1, moe_w3, moe_w2, l,
                  final_g if l == depth - 1 else None)
    if transposed:
        xr = _regroup(xr, b, GRID_W, rows, nl)
    return xr[:nl].reshape(b, seq, d)
```

```python
import functools
import math

import jax
import jax.numpy as jnp
from jax import lax
from jax.experimental import pallas as pl
from jax.experimental.pallas import tpu as pltpu

F32 = jnp.float32
BF16 = jnp.bfloat16
HI = lax.Precision.HIGHEST

D_MODEL = 2048
GRID_W = 64
EPS = 1e-6
CHUNK = 64
CONV_K = 5
ML_H, ML_DK, ML_DV = 4, 128, 256
HG_H, HG_DK, HG_DV = 8, 128, 128
ML_QK, ML_V = ML_H * ML_DK, ML_H * ML_DV
HG_K, HG_V = HG_H * HG_DK, HG_H * HG_DV
AB_OUT = ML_V + HG_V
D_INNER = 2 * D_MODEL
SSD_P, SSD_G, SSD_N = 64, 8, 128
SSD_H = D_INNER // SSD_P
SSD_R = SSD_H // SSD_G
N_EXPERTS, N_GROUPS, TOP_K, D_EXPERT = 16, 4, 2, 1024
EXPERTS_PER_GROUP = N_EXPERTS // N_GROUPS

LANES = 128
SUBLANES = 8
VMEM_LIMIT = 56 * 1024 * 1024

AB_S_Q, AB_S_K, AB_S_V, AB_S_OG, AB_S_HQ, AB_S_HF, AB_S_HI, AB_S_HG, AB_S_GATE = 0, 4, 8, 16, 24, 32, 48, 56, 64
AB_SLABS = 66
SSD_S_Z, SSD_S_X, SSD_S_B, SSD_S_C, SSD_S_DT = 0, 32, 64, 72, 80
SSD_SLABS = 81


def _cparams(sem):
    return pltpu.CompilerParams(dimension_semantics=sem, vmem_limit_bytes=VMEM_LIMIT)


def _silu(x):
    return x * jax.nn.sigmoid(x)


def _dot(a, b):
    return jnp.dot(a.astype(BF16), b.astype(BF16), preferred_element_type=F32)


def _dot_nt(a, b):
    return lax.dot_general(a.astype(BF16), b.astype(BF16), (((1,), (1,)), ((), ())),
                           preferred_element_type=F32)


def _dot_tn(a, b):
    return lax.dot_general(a.astype(BF16), b.astype(BF16), (((0,), (0,)), ((), ())),
                           preferred_element_type=F32)


def _dot_hi(a, b):
    return jnp.dot(a, b, precision=HI, preferred_element_type=F32)


def _tri(rev):
    t = lax.broadcasted_iota(jnp.int32, (CHUNK, CHUNK), 0)
    s = lax.broadcasted_iota(jnp.int32, (CHUNK, CHUNK), 1)
    return (s >= t) if rev else (s <= t)


def _mod_kernel(c_ref, w_ref, b_ref, o_ref):
    c = c_ref[...]
    o_ref[...] = _dot(_silu(c), w_ref[...]) + b_ref[...]


def _modulation(c8, ada_w, ada_b):
    depth, d, d6 = ada_w.shape
    tn = 1024
    return pl.pallas_call(
        _mod_kernel,
        grid=(depth, d6 // tn),
        in_specs=[pl.BlockSpec((SUBLANES, d), lambda l, j: (0, 0)),
                  pl.BlockSpec((None, d, tn), lambda l, j: (l, 0, j)),
                  pl.BlockSpec((None, 1, tn), lambda l, j: (l, 0, j))],
        out_specs=pl.BlockSpec((None, SUBLANES, tn), lambda l, j: (l, 0, j)),
        out_shape=jax.ShapeDtypeStruct((depth, SUBLANES, d6), F32),
        compiler_params=_cparams(("parallel", "parallel")),
        name="modulation",
    )(c8, ada_w, ada_b.reshape(depth, 1, d6))


def _inproj_kernel(x_ref, g_ref, sc_ref, sh_ref, w_ref, o_ref, h_ref):
    @pl.when(pl.program_id(1) == 0)
    def _():
        x = x_ref[...]
        y = x * lax.rsqrt(jnp.mean(x * x, axis=-1, keepdims=True) + EPS) * g_ref[...]
        h_ref[...] = (y * (1.0 + sc_ref[...]) + sh_ref[...]).astype(BF16)

    acc = jnp.dot(h_ref[...], w_ref[...], preferred_element_type=F32)
    for s in range(o_ref.shape[0]):
        o_ref[s] = acc[:, s * LANES:(s + 1) * LANES]


def _inproj(x, g, sc, sh, w, seg_fn, tm, tn):
    n, d = x.shape
    ncols = w.shape[1]
    nsl = tn // LANES
    return pl.pallas_call(
        _inproj_kernel,
        grid=(n // tm, ncols // tn),
        in_specs=[pl.BlockSpec((tm, d), lambda i, j: (i, 0)),
                  pl.BlockSpec((1, d), lambda i, j: (0, 0)),
                  pl.BlockSpec((None, 1, d), lambda i, j: (seg_fn(i * tm), 0, 0)),
                  pl.BlockSpec((None, 1, d), lambda i, j: (seg_fn(i * tm), 0, 0)),
                  pl.BlockSpec((d, tn), lambda i, j: (0, j))],
        out_specs=pl.BlockSpec((nsl, tm, LANES), lambda i, j: (j, i, 0)),
        out_shape=jax.ShapeDtypeStruct((ncols // LANES, n, LANES), F32),
        scratch_shapes=[pltpu.VMEM((tm, d), BF16)],
        compiler_params=_cparams(("parallel", "arbitrary")),
        name="inproj",
    )(x, g.reshape(1, d), sc, sh, w)


def _conv_kernel(prev_ref, cur_ref, next_ref, w_ref, o_ref, *, tt, tiles_lat, n_lat_tiles, tiles_ctx):
    i = pl.program_id(1)
    in_lat = i < n_lat_tiles
    pos = jnp.where(in_lat, i % tiles_lat, (i - n_lat_tiles) % tiles_ctx)
    last = jnp.where(in_lat, tiles_lat - 1, tiles_ctx - 1)
    keep_prev = (pos != 0).astype(F32)
    keep_next = (pos != last).astype(F32)
    ext = jnp.concatenate([prev_ref[...] * keep_prev, cur_ref[...], next_ref[...] * keep_next], axis=1)
    w = w_ref[...]
    acc = jnp.zeros(cur_ref.shape, F32) + w[:, CONV_K:CONV_K + 1, :]
    for k in range(CONV_K):
        shift = (CONV_K // 2 - k) % (tt + 2 * SUBLANES)
        r = ext if shift == 0 else pltpu.roll(ext, shift, 1)
        acc = acc + r[:, SUBLANES:SUBLANES + tt, :] * w[:, k:k + 1, :]
    o_ref[...] = (_silu(acc) * w[:, CONV_K + 1:CONV_K + 2, :]).astype(o_ref.dtype)


def _conv_slabs(p3, slab0, nslab, wpack, dims, sb):
    b, seq, ctxl = dims
    n = b * (seq + ctxl)
    tt = min(256, ctxl)
    t8 = tt // SUBLANES
    nblk8 = n // SUBLANES
    s0 = slab0 // sb
    kern = functools.partial(_conv_kernel, tt=tt, tiles_lat=seq // tt, n_lat_tiles=b * seq // tt,
                             tiles_ctx=ctxl // tt)
    return pl.pallas_call(
        kern,
        grid=(nslab // sb, n // tt),
        in_specs=[pl.BlockSpec((sb, SUBLANES, LANES), lambda s, i: (s0 + s, jnp.maximum(i * t8 - 1, 0), 0)),
                  pl.BlockSpec((sb, tt, LANES), lambda s, i: (s0 + s, i, 0)),
                  pl.BlockSpec((sb, SUBLANES, LANES),
                               lambda s, i: (s0 + s, jnp.minimum((i + 1) * t8, nblk8 - 1), 0)),
                  pl.BlockSpec((sb, SUBLANES, LANES), lambda s, i: (s, 0, 0))],
        out_specs=pl.BlockSpec((sb, tt, LANES), lambda s, i: (s, i, 0)),
        out_shape=jax.ShapeDtypeStruct((nslab, n, LANES), BF16),
        compiler_params=_cparams(("parallel", "parallel")),
        name="conv",
    )(p3, p3, p3, wpack)


def _conv_pack(conv_w, conv_b, scale):
    c = conv_w.shape[1]
    rows = jnp.concatenate([conv_w, conv_b[None], scale[None], jnp.zeros((1, c), F32)], axis=0)
    return rows.reshape(SUBLANES, c // LANES, LANES).transpose(1, 0, 2)


def _chunk_maps(dims):
    b, seq, ctxl = dims
    ncc, nlc = ctxl // CHUNK, seq // CHUNK

    def fwd(bi, i):
        return jnp.where(i < ncc, b * nlc + bi * ncc + i, bi * nlc + (i - ncc))

    def bwd(bi, i):
        return jnp.where(i < ncc, b * nlc + bi * ncc + (ncc - 1 - i), bi * nlc + (nlc - 1 - (i - ncc)))

    return fwd, bwd, ncc + nlc


def _mlstm_kernel(qkf_ref, vf_ref, gcf_ref, grf_ref, qkb_ref, vb_ref, gcb_ref, grb_ref, bc_ref, br_ref,
                  of_ref, ob_ref, c_ref, n_ref, m_ref):
    @pl.when(pl.program_id(1) == 0)
    def _():
        c_ref[...] = jnp.zeros_like(c_ref)
        n_ref[...] = jnp.zeros_like(n_ref)
        m_ref[...] = jnp.zeros_like(m_ref)

    ng = 2 * ML_H
    for d, (qk_ref, v_ref, gc_ref, gr_ref, o_ref) in enumerate(
            ((qkf_ref, vf_ref, gcf_ref, grf_ref, of_ref), (qkb_ref, vb_ref, gcb_ref, grb_ref, ob_ref))):
        rev = d == 1
        mask = _tri(rev)
        gcol = gc_ref[0][:, :2 * ng] + bc_ref[...]
        grow = gr_ref[...] + br_ref[...]
        ic_all = gcol[:, d * ML_H:(d + 1) * ML_H]
        lfc_all = jax.nn.log_sigmoid(gcol[:, ng + d * ML_H:ng + (d + 1) * ML_H])
        ir_all = grow[d * ML_H:(d + 1) * ML_H, :]
        lfr_all = jax.nn.log_sigmoid(grow[ng + d * ML_H:ng + (d + 1) * ML_H, :])
        incl = mask.astype(F32)
        bcol_all = _dot_hi(incl, lfc_all)
        brow_all = _dot_hi(lfr_all, _tri(not rev).astype(F32))
        last = 0 if rev else CHUNK - 1
        for h in range(ML_H):
            q = qk_ref[h]
            k = qk_ref[ML_H + h]
            v = jnp.concatenate([v_ref[2 * h], v_ref[2 * h + 1]], axis=1)
            bcol, icol = bcol_all[:, h:h + 1], ic_all[:, h:h + 1]
            brow, irow = brow_all[h:h + 1, :], ir_all[h:h + 1, :]
            cst = c_ref[d, h]
            nst = n_ref[d, h]
            mprev = m_ref[d, h][:, :1]
            logd = jnp.where(mask, bcol - brow + irow, -jnp.inf)
            inter = bcol + mprev
            m_t = jnp.maximum(inter, jnp.max(logd, axis=1, keepdims=True))
            s = _dot_nt(q, k) * jnp.exp(logd - m_t)
            sc = jnp.exp(inter - m_t)
            num = _dot(s, v) + sc * _dot_nt(q, cst)
            den = jnp.sum(s, axis=1, keepdims=True) + sc * jnp.sum(q * nst, axis=1, keepdims=True)
            hout = num / jnp.maximum(jnp.abs(den), jnp.exp(-m_t))
            o_ref[2 * h] = hout[:, :LANES]
            o_ref[2 * h + 1] = hout[:, LANES:]
            b_last = bcol[last:last + 1, :]
            wlog = b_last - bcol + icol
            m_new = jnp.maximum(b_last + mprev, jnp.max(wlog, axis=0, keepdims=True))
            w = jnp.exp(wlog - m_new)
            dec = jnp.exp(b_last + mprev - m_new)
            c_ref[d, h] = dec * cst + _dot_tn(w * v, k)
            n_ref[d, h] = dec * nst + jnp.sum(w * k, axis=0, keepdims=True)
            m_ref[d, h] = jnp.broadcast_to(m_new, (1, LANES))


def _mlstm(qk, p3, grow, bcol, brow, dims):
    b = dims[0]
    n = qk.shape[1]
    fwd, bwd, nch = _chunk_maps(dims)

    def specs(cm):
        return [pl.BlockSpec((2 * ML_H, CHUNK, LANES), lambda bi, i: (0, cm(bi, i), 0)),
                pl.BlockSpec((2 * ML_H, CHUNK, LANES), lambda bi, i: (AB_S_V // (2 * ML_H), cm(bi, i), 0)),
                pl.BlockSpec((1, CHUNK, LANES), lambda bi, i: (AB_S_GATE, cm(bi, i), 0)),
                pl.BlockSpec((None, 4 * ML_H, CHUNK), lambda bi, i: (cm(bi, i), 0, 0))]

    out_sd = jax.ShapeDtypeStruct((2 * ML_H, n, LANES), F32)
    return pl.pallas_call(
        _mlstm_kernel,
        grid=(b, nch),
        in_specs=specs(fwd) + specs(bwd) + [pl.BlockSpec((1, 4 * ML_H), lambda bi, i: (0, 0)),
                                            pl.BlockSpec((4 * ML_H, 1), lambda bi, i: (0, 0))],
        out_specs=[pl.BlockSpec((2 * ML_H, CHUNK, LANES), lambda bi, i: (0, fwd(bi, i), 0)),
                   pl.BlockSpec((2 * ML_H, CHUNK, LANES), lambda bi, i: (0, bwd(bi, i), 0))],
        out_shape=[out_sd, out_sd],
        scratch_shapes=[pltpu.VMEM((2, ML_H, ML_DV, ML_DK), F32),
                        pltpu.VMEM((2, ML_H, 1, ML_DK), F32),
                        pltpu.VMEM((2, ML_H, 1, LANES), F32)],
        compiler_params=_cparams(("parallel", "arbitrary")),
        name="mlstm",
    )(qk, p3, p3, grow, qk, p3, p3, grow, bcol, brow)


def _bcast_rows(a, rows, span):
    parts = [jnp.broadcast_to(a[r:r + 1, :], (span, a.shape[1])) for r in rows]
    return parts[0] if len(parts) == 1 else jnp.concatenate(parts, axis=0)


def _hgrn2_kernel(qf_ref, ff_ref, vf_ref, qb_ref, fb_ref, vb_ref, lb_ref, of_ref, ob_ref, s_ref):
    @pl.when(pl.program_id(1) == 0)
    def _():
        s_ref[...] = jnp.zeros_like(s_ref)

    t = lax.broadcasted_iota(jnp.int32, (CHUNK, CHUNK), 0)
    s = lax.broadcasted_iota(jnp.int32, (CHUNK, CHUNK), 1)
    refs = ((qf_ref, ff_ref, vf_ref, of_ref), (qb_ref, fb_ref, vb_ref, ob_ref))
    units = [(d, h) for d in range(2) for h in range(HG_H)]
    levels = (32, 16, 8)

    def level_mask(m, rev):
        same = (t // (2 * m)) == (s // (2 * m))
        t_late = ((t // m) % 2 == 0) if rev else ((t // m) % 2 == 1)
        s_early = ((s // m) % 2 == 1) if rev else ((s // m) % 2 == 0)
        return same & t_late & s_early

    p1 = {}
    for d, h in units:
        q_ref, f_ref, v_ref, _ = refs[d]
        lb = lb_ref[h]
        f = lb + (1.0 - lb) * jax.nn.sigmoid(f_ref[h])
        lg = jnp.log(f)
        a = _dot_hi(_tri(d == 1).astype(F32), lg)
        p1[d, h] = (q_ref[h], 1.0 - f, lg, v_ref[h], a, s_ref[d, h])
    p2 = {}
    for d, h in units:
        rev = d == 1
        q, k, lg, v, a, st = p1[d, h]
        last = 0 if rev else CHUNK - 1
        qs, ks = [], []
        for m in levels:
            nb = CHUNK // (2 * m)
            aref = _bcast_rows(a, [bi * 2 * m + (m if rev else m - 1) for bi in range(nb)], 2 * m)
            qs.append((q * jnp.exp(jnp.minimum(a - aref, 0.0))).astype(BF16))
            ks.append((k * jnp.exp(jnp.minimum(aref - a, 0.0))).astype(BF16))
        aref = _bcast_rows(a - lg, [bi * SUBLANES + (SUBLANES - 1 if rev else 0) for bi in range(CHUNK // SUBLANES)],
                           SUBLANES)
        qs.append((q * jnp.exp(a - aref)).astype(BF16))
        ks.append((k * jnp.exp(aref - a)).astype(BF16))
        a_last = a[last:last + 1, :]
        p2[d, h] = (qs, ks, (q * jnp.exp(a)).astype(BF16), (k * jnp.exp(a_last - a)).astype(BF16), jnp.exp(a_last))
    p3 = {}
    for d, h in units:
        qs, ks, qdec, kdec, edec = p2[d, h]
        st = p1[d, h][5]
        p3[d, h] = ([_dot_nt(qq, kk) for qq, kk in zip(qs, ks)], _dot_nt(qdec, st))
    for d, h in units:
        rev = d == 1
        prods, qst = p3[d, h]
        scores = jnp.zeros((CHUNK, CHUNK), F32)
        for m, pr in zip(levels, prods[:-1]):
            scores = scores + jnp.where(level_mask(m, rev), pr, 0.0)
        diag = ((t // SUBLANES) == (s // SUBLANES)) & _tri(rev)
        scores = scores + jnp.where(diag, prods[-1], 0.0)
        refs[d][3][h] = _dot(scores, p1[d, h][3]) + qst
    for d, h in units:
        v, st = p1[d, h][3], p1[d, h][5]
        kdec, edec = p2[d, h][3], p2[d, h][4]
        s_ref[d, h] = st * edec + _dot_tn(v, kdec)


def _hgrn2(p3, lb, dims):
    b = dims[0]
    n = b * (dims[1] + dims[2])
    fwd, bwd, nch = _chunk_maps(dims)

    def specs(cm, d):
        return [pl.BlockSpec((HG_H, CHUNK, LANES), lambda bi, i: (AB_S_HQ // HG_H, cm(bi, i), 0)),
                pl.BlockSpec((HG_H, CHUNK, LANES), lambda bi, i: (AB_S_HF // HG_H + d, cm(bi, i), 0)),
                pl.BlockSpec((HG_H, CHUNK, LANES), lambda bi, i: (AB_S_HI // HG_H, cm(bi, i), 0))]

    out_sd = jax.ShapeDtypeStruct((HG_H, n, LANES), F32)
    return pl.pallas_call(
        _hgrn2_kernel,
        grid=(b, nch),
        in_specs=specs(fwd, 0) + specs(bwd, 1) + [pl.BlockSpec((HG_H, 1, LANES), lambda bi, i: (0, 0, 0))],
        out_specs=[pl.BlockSpec((HG_H, CHUNK, LANES), lambda bi, i: (0, fwd(bi, i), 0)),
                   pl.BlockSpec((HG_H, CHUNK, LANES), lambda bi, i: (0, bwd(bi, i), 0))],
        out_shape=[out_sd, out_sd],
        scratch_shapes=[pltpu.VMEM((2, HG_H, HG_DV, HG_DK), F32)],
        compiler_params=_cparams(("parallel", "arbitrary")),
        name="hgrn2",
    )(p3, p3, p3, p3, p3, p3, lb)


def _ab_out_kernel(mf_ref, mb_ref, hf_ref, hb_ref, og_ref, hg_ref, mlg_ref, hgg_ref, w_ref, x_ref, g1_ref,
                   o_ref, lhs_ref):
    for h in range(ML_H):
        hs = jnp.concatenate([mf_ref[2 * h] + mb_ref[2 * h], mf_ref[2 * h + 1] + mb_ref[2 * h + 1]], axis=1)
        r = hs * lax.rsqrt(jnp.mean(hs * hs, axis=-1, keepdims=True) + EPS)
        og = jnp.concatenate([og_ref[2 * h], og_ref[2 * h + 1]], axis=1)
        y = jax.nn.sigmoid(og) * (r * mlg_ref[:, h * ML_DV:(h + 1) * ML_DV])
        lhs_ref[:, h * ML_DV:(h + 1) * ML_DV] = y.astype(BF16)
    for h in range(HG_H):
        hs = hf_ref[h] + hb_ref[h]
        r = hs * lax.rsqrt(jnp.mean(hs * hs, axis=-1, keepdims=True) + EPS)
        y = _silu(hg_ref[h]) * (r * hgg_ref[:, h * HG_DV:(h + 1) * HG_DV])
        lhs_ref[:, ML_V + h * HG_DV:ML_V + (h + 1) * HG_DV] = y.astype(BF16)
    acc = jnp.dot(lhs_ref[...], w_ref[...], preferred_element_type=F32)
    o_ref[...] = x_ref[...] + g1_ref[...] * acc


def _ab_out(hm, ho, p3, ml_g, hg_g, w_out, x, g1, seg_fn, n, tm):
    d = x.shape[1]
    slab8 = lambda idx: pl.BlockSpec((SUBLANES, tm, LANES), lambda i: (idx, i, 0))
    return pl.pallas_call(
        _ab_out_kernel,
        grid=(n // tm,),
        in_specs=[slab8(0), slab8(0), slab8(0), slab8(0), slab8(AB_S_OG // SUBLANES), slab8(AB_S_HG // SUBLANES),
                  pl.BlockSpec((1, ML_V), lambda i: (0, 0)),
                  pl.BlockSpec((1, HG_V), lambda i: (0, 0)),
                  pl.BlockSpec((AB_OUT, d), lambda i: (0, 0)),
                  pl.BlockSpec((tm, d), lambda i: (i, 0)),
                  pl.BlockSpec((None, 1, d), lambda i: (seg_fn(i * tm), 0, 0))],
        out_specs=pl.BlockSpec((tm, d), lambda i: (i, 0)),
        out_shape=jax.ShapeDtypeStruct((n, d), F32),
        scratch_shapes=[pltpu.VMEM((tm, AB_OUT), BF16)],
        compiler_params=_cparams(("parallel",)),
        name="ab_out",
    )(hm[0], hm[1], ho[0], ho[1], p3, p3, ml_g.reshape(1, ML_V), hg_g.reshape(1, HG_V), w_out, x, g1)


def _ab_weight(w_in):
    d = w_in.shape[0]
    o = [0, 2 * ML_QK, 2 * ML_QK + ML_V, 2 * ML_QK + 2 * ML_V]
    g0 = o[3]
    h0 = g0 + 4 * ML_H
    pad = AB_SLABS * LANES - (w_in.shape[1])
    return jnp.concatenate([w_in[:, :g0], w_in[:, h0:], w_in[:, g0:h0], jnp.zeros((d, pad), w_in.dtype)],
                           axis=1).astype(BF16)


def _ab_layer(x, dims, seg_fn, tm, n_out, g_norm, sc, sh, g1, w_in, conv_w, conv_b, ig_b, fg_b, lb, ml_g, hg_g,
              w_out):
    b, seq, ctxl = dims
    n = b * (seq + ctxl)
    p3 = _inproj(x, g_norm, sc, sh, _ab_weight(w_in), seg_fn, min(2 * tm, seq), 11 * LANES)
    kscale = jnp.concatenate([jnp.ones((ML_QK,), F32), jnp.full((ML_QK,), ML_DK ** -0.5, F32)])
    qk = _conv_slabs(p3, AB_S_Q, 2 * ML_H, _conv_pack(conv_w, conv_b, kscale), dims, 2 * ML_H)
    gates = p3[AB_S_GATE, :n, :4 * ML_H]
    grow = gates.reshape(n // CHUNK, CHUNK, 4 * ML_H).transpose(0, 2, 1)
    gbias = jnp.concatenate([ig_b.reshape(-1), fg_b.reshape(-1)])
    hm = _mlstm(qk, p3, grow, gbias.reshape(1, -1), gbias.reshape(-1, 1), dims)
    ho = _hgrn2(p3, lb.reshape(HG_H, 1, HG_DK), dims)
    return _ab_out(hm, ho, p3, ml_g, hg_g, w_out.astype(BF16), x, g1, seg_fn, n_out, min(tm, 256))


def _ssd_kernel(xf_ref, dcf_ref, drf_ref, xb_ref, dcb_ref, drb_ref, bc_ref, br_ref, ac_ref, ar_ref,
                of_ref, ob_ref, h_ref):
    @pl.when(pl.program_id(1) == 0)
    def _():
        h_ref[...] = jnp.zeros_like(h_ref)

    gw = SSD_R * SSD_P
    t_idx = lax.broadcasted_iota(jnp.int32, (CHUNK, gw), 0)
    s_idx = lax.broadcasted_iota(jnp.int32, (CHUNK, gw), 1) % SSD_P
    e_r = lax.broadcasted_iota(jnp.int32, (3 * SSD_R, gw), 0) % SSD_R
    e_c = lax.broadcasted_iota(jnp.int32, (3 * SSD_R, gw), 1) // SSD_P
    expand3 = (e_r == e_c).astype(BF16)
    low_half = lax.broadcasted_iota(jnp.int32, (CHUNK, LANES), 1) < SSD_P

    def expand(v):
        hi = v.astype(BF16).astype(F32)
        mid = (v - hi).astype(BF16).astype(F32)
        lo = v - hi - mid
        return jnp.dot(jnp.concatenate([hi, mid, lo], axis=1).astype(BF16), expand3, preferred_element_type=F32)

    refs = ((xf_ref, dcf_ref, drf_ref, of_ref), (xb_ref, dcb_ref, drb_ref, ob_ref))
    units = [(d, g) for d in range(2) for g in range(SSD_G)]
    gate = []
    for d, (x_ref, dc_ref, dr_ref, o_ref) in enumerate(refs):
        rev = d == 1
        mask = (s_idx >= t_idx) if rev else (s_idx <= t_idx)
        last = 0 if rev else CHUNK - 1
        hs = slice(d * SSD_H, (d + 1) * SSD_H)
        dtc = jax.nn.softplus(dc_ref[0][:, hs] + bc_ref[:, hs])
        lac = dtc * ac_ref[:, hs]
        dtr = jax.nn.softplus(dr_ref[hs, :] + br_ref[hs, :])
        lar = dtr * ar_ref[hs, :]
        cum_c = _dot_hi(_tri(rev).astype(F32), lac)
        cum_r = _dot_hi(lar, _tri(not rev).astype(F32))
        wgt = jnp.exp(cum_c[last:last + 1, :] - cum_c) * dtc
        gate.append((mask, last, cum_c, cum_r, dtr, wgt))
    p1 = {}
    for d, g in units:
        x_ref = refs[d][0]
        mask, last, cum_c, cum_r, dtr, wgt = gate[d]
        heads = slice(g * SSD_R, (g + 1) * SSD_R)
        x = jnp.concatenate([x_ref[4 * g + j] for j in range(4)], axis=1)
        bm = x_ref[4 * SSD_G + g]
        cm = x_ref[5 * SSD_G + g]
        hst = h_ref[d, g]
        p1[d, g] = (x, bm, hst, expand(cum_c[:, heads]), expand(wgt[:, heads]), _dot_nt(cm, bm), _dot(cm, hst))
    p2 = {}
    for d, g in units:
        mask, last, cum_c, cum_r, dtr, wgt = gate[d]
        x, bm, hst, cum_x, wgt_x, cb, yoff = p1[d, g]
        rows = lambda a: jnp.concatenate([a[g * SSD_R + r:g * SSD_R + r + 1, :] for r in range(SSD_R)], axis=1)
        seg = jnp.exp(jnp.where(mask, cum_x - rows(cum_r), -jnp.inf)) * rows(dtr)
        cb2 = jnp.concatenate([cb, cb], axis=1)
        ms = [(cb2 * seg[:, j * LANES:(j + 1) * LANES]).astype(BF16) for j in range(4)]
        p2[d, g] = (ms, jnp.exp(cum_x), (x * wgt_x).astype(BF16))
    for d, g in units:
        o_ref = refs[d][3]
        x, yoff = p1[d, g][0], p1[d, g][6]
        ms, ecum_x, xw = p2[d, g]
        for j in range(4):
            ls = slice(j * LANES, (j + 1) * LANES)
            xp = x[:, ls]
            bd = jnp.concatenate([jnp.where(low_half, xp, 0.0), jnp.where(low_half, 0.0, xp)], axis=0)
            o_ref[4 * g + j] = (_dot(ms[j], bd) + ecum_x[:, ls] * yoff[:, ls]).astype(o_ref.dtype)
    for d, g in units:
        last = gate[d][1]
        bm, hst = p1[d, g][1], p1[d, g][2]
        ms, ecum_x, xw = p2[d, g]
        h_ref[d, g] = hst * ecum_x[last:last + 1, :] + _dot_tn(bm, xw)


def _ssd_scan(xbc, p3, dtrow, dt_b, neg_a, dims):
    b = dims[0]
    n = xbc.shape[1]
    fwd, bwd, nch = _chunk_maps(dims)
    nxs = xbc.shape[0]

    def specs(cm):
        return [pl.BlockSpec((nxs, CHUNK, LANES), lambda bi, i: (0, cm(bi, i), 0)),
                pl.BlockSpec((1, CHUNK, LANES), lambda bi, i: (SSD_S_DT, cm(bi, i), 0)),
                pl.BlockSpec((None, 2 * SSD_H, CHUNK), lambda bi, i: (cm(bi, i), 0, 0))]

    vec = lambda shape: pl.BlockSpec(shape, lambda bi, i: (0, 0))
    out_sd = jax.ShapeDtypeStruct((D_INNER // LANES, n, LANES), BF16)
    return pl.pallas_call(
        _ssd_kernel,
        grid=(b, nch),
        in_specs=specs(fwd) + specs(bwd) + [vec((1, 2 * SSD_H)), vec((2 * SSD_H, 1)),
                                            vec((1, 2 * SSD_H)), vec((2 * SSD_H, 1))],
        out_specs=[pl.BlockSpec((D_INNER // LANES, CHUNK, LANES), lambda bi, i: (0, fwd(bi, i), 0)),
                   pl.BlockSpec((D_INNER // LANES, CHUNK, LANES), lambda bi, i: (0, bwd(bi, i), 0))],
        out_shape=[out_sd, out_sd],
        scratch_shapes=[pltpu.VMEM((2, SSD_G, SSD_N, SSD_R * SSD_P), F32)],
        compiler_params=_cparams(("parallel", "arbitrary")),
        name="ssd_scan",
    )(xbc, p3, dtrow, xbc, p3, dtrow, dt_b.reshape(1, -1), dt_b.reshape(-1, 1),
      neg_a.reshape(1, -1), neg_a.reshape(-1, 1))


def _ssd_out_kernel(yf_ref, yb_ref, xs_ref, z_ref, dsk_ref, ng_ref, w_ref, x_ref, g1_ref, o_ref, lhs_ref, acc_ref):
    k = pl.program_id(1)

    @pl.when(k == 0)
    def _():
        acc_ref[...] = jnp.zeros_like(acc_ref)

    gw = D_INNER // SSD_G
    for gg in range(lhs_ref.shape[1] // gw):
        cat = lambda ref: jnp.concatenate([ref[4 * gg + j] for j in range(4)], axis=1).astype(F32)
        cs = slice(gg * gw, (gg + 1) * gw)
        y = cat(yf_ref) + cat(yb_ref) + dsk_ref[:, cs] * cat(xs_ref)
        u = y * _silu(cat(z_ref))
        u = u * lax.rsqrt(jnp.mean(u * u, axis=-1, keepdims=True) + EPS) * ng_ref[:, cs]
        lhs_ref[:, cs] = u.astype(BF16)
    acc_ref[...] += jnp.dot(lhs_ref[...], w_ref[...], preferred_element_type=F32)

    @pl.when(k == pl.num_programs(1) - 1)
    def _():
        o_ref[...] = x_ref[...] + g1_ref[...] * acc_ref[...]


def _ssd_out(yf, yb, xbc, p3, dskip, norm_g, w_out, x, g1, seg_fn, n_rows, tm):
    d = x.shape[1]
    tk = 1024
    nsl = tk // LANES
    slab = lambda: pl.BlockSpec((nsl, tm, LANES), lambda i, k: (k, i, 0))
    return pl.pallas_call(
        _ssd_out_kernel,
        grid=(n_rows // tm, D_INNER // tk),
        in_specs=[slab(), slab(), slab(), slab(),
                  pl.BlockSpec((1, tk), lambda i, k: (0, k)),
                  pl.BlockSpec((1, tk), lambda i, k: (0, k)),
                  pl.BlockSpec((tk, d), lambda i, k: (k, 0)),
                  pl.BlockSpec((tm, d), lambda i, k: (i, 0)),
                  pl.BlockSpec((None, 1, d), lambda i, k: (seg_fn(i * tm), 0, 0))],
        out_specs=pl.BlockSpec((tm, d), lambda i, k: (i, 0)),
        out_shape=jax.ShapeDtypeStruct((n_rows, d), F32),
        scratch_shapes=[pltpu.VMEM((tm, tk), BF16), pltpu.VMEM((tm, d), F32)],
        compiler_params=_cparams(("parallel", "arbitrary")),
        name="ssd_out",
    )(yf, yb, xbc, p3, dskip, norm_g.reshape(1, -1), w_out, x, g1)


def _ssd_layer(x, dims, seg_fn, tm, n_out, g_norm, sc, sh, g1, w_in, conv_w, conv_b, dt_b, a_log, d_skip,
               norm_g, w_out):
    b, seq, ctxl = dims
    n = b * (seq + ctxl)
    p3 = _inproj(x, g_norm, sc, sh, w_in.astype(BF16), seg_fn, min(2 * tm, seq), 9 * LANES)
    nconv = conv_w.shape[1]
    xbc = _conv_slabs(p3, SSD_S_X, nconv // LANES, _conv_pack(conv_w, conv_b, jnp.ones((nconv,), F32)), dims, 16)
    dt = p3[SSD_S_DT, :n]
    dtrow = dt.reshape(n // CHUNK, CHUNK, 2 * SSD_H).transpose(0, 2, 1)
    yf, yb = _ssd_scan(xbc, p3, dtrow, dt_b.reshape(-1), -jnp.exp(a_log.astype(F32)).reshape(-1), dims)
    dskip = jnp.repeat(d_skip.astype(F32), SSD_P).reshape(1, D_INNER)
    return _ssd_out(yf, yb, xbc, p3, dskip, norm_g, w_out.astype(BF16), x, g1, seg_fn, n_out, tm)


def _regroup_kernel(x_ref, o_ref):
    a = x_ref.shape[0]
    for cc in range(SUBLANES):
        o_ref[cc * a:(cc + 1) * a, :] = x_ref[:, cc, :]


def _regroup(x, b, a, c, n_out):
    d = x.shape[1]
    cblk = c // SUBLANES
    return pl.pallas_call(
        _regroup_kernel,
        grid=(b, cblk),
        in_specs=[pl.BlockSpec((a, SUBLANES, d), lambda bi, j: (bi, j, 0))],
        out_specs=pl.BlockSpec((a * SUBLANES, d), lambda bi, j: (bi * cblk + j, 0)),
        out_shape=jax.ShapeDtypeStruct((n_out, d), F32),
        compiler_params=_cparams(("parallel", "parallel")),
        name="regroup",
    )(x.reshape(x.shape[0] // c, c, d))


MOE_TILE = 512


def _router_kernel(x_ref, g_ref, sc_ref, sh_ref, wt_ref, rb_ref, hn_ref, ii_ref, iw_ref, cnt_ref, carry_ref):
    @pl.when(pl.program_id(0) == 0)
    def _():
        carry_ref[...] = jnp.zeros_like(carry_ref)

    x = x_ref[...]
    tm = x.shape[0]
    y = x * lax.rsqrt(jnp.mean(x * x, axis=-1, keepdims=True) + EPS) * g_ref[...]
    hn = y * (1.0 + sc_ref[...]) + sh_ref[...]
    _store_token_rows(hn_ref, hn)
    logits = lax.dot_general(wt_ref[...], hn, (((1,), (1,)), ((), ())), precision=HI,
                             preferred_element_type=F32)
    score = jax.nn.sigmoid(logits)
    biased = score + rb_ref[...]
    rb = [biased[e:e + 1, :] for e in range(N_EXPERTS)]
    rs = [score[e:e + 1, :] for e in range(N_EXPERTS)]
    gsc = []
    for g in range(N_GROUPS):
        a, b, c, d = rb[EXPERTS_PER_GROUP * g:EXPERTS_PER_GROUP * (g + 1)]
        hi1, lo1, hi2, lo2 = jnp.maximum(a, b), jnp.minimum(a, b), jnp.maximum(c, d), jnp.minimum(c, d)
        gsc.append(jnp.maximum(hi1, hi2) + jnp.maximum(jnp.minimum(hi1, hi2), jnp.maximum(lo1, lo2)))
    best = jnp.zeros((1, tm), jnp.int32)
    bsc = gsc[0]
    for g in range(1, N_GROUPS):
        upd = gsc[g] > bsc
        best = jnp.where(upd, g, best)
        bsc = jnp.where(upd, gsc[g], bsc)

    def pick(rows, p):
        out = rows[p]
        for g in range(1, N_GROUPS):
            out = jnp.where(best == g, rows[EXPERTS_PER_GROUP * g + p], out)
        return out

    vals = [pick(rb, p) for p in range(EXPERTS_PER_GROUP)]
    scs = [pick(rs, p) for p in range(EXPERTS_PER_GROUP)]
    p1, v1, s1 = jnp.zeros((1, tm), jnp.int32), vals[0], scs[0]
    for p in range(1, EXPERTS_PER_GROUP):
        upd = vals[p] > v1
        p1, v1, s1 = jnp.where(upd, p, p1), jnp.where(upd, vals[p], v1), jnp.where(upd, scs[p], s1)
    p2 = jnp.zeros((1, tm), jnp.int32)
    v2 = jnp.full((1, tm), -jnp.inf, F32)
    s2 = jnp.zeros((1, tm), F32)
    for p in range(EXPERTS_PER_GROUP):
        upd = (p1 != p) & (vals[p] > v2)
        p2, v2, s2 = jnp.where(upd, p, p2), jnp.where(upd, vals[p], v2), jnp.where(upd, scs[p], s2)
    e1 = best * EXPERTS_PER_GROUP + p1
    e2 = best * EXPERTS_PER_GROUP + p2
    tot = s1 + s2
    eiota = lax.broadcasted_iota(jnp.int32, (N_EXPERTS, tm), 0)
    oh1 = (eiota == e1).astype(F32)
    oh2 = (eiota == e2).astype(F32)
    oh = oh1 + oh2
    before = (lax.broadcasted_iota(jnp.int32, (tm, tm), 0) < lax.broadcasted_iota(jnp.int32, (tm, tm), 1))
    cnt = _dot(oh, before.astype(BF16)) + carry_ref[:, :1]
    r1 = jnp.sum(oh1 * cnt, axis=0, keepdims=True).astype(jnp.int32)
    r2 = jnp.sum(oh2 * cnt, axis=0, keepdims=True).astype(jnp.int32)
    zi = jnp.zeros((SUBLANES - 4, tm), jnp.int32)
    ii_ref[...] = jnp.concatenate([e1, e2, r1, r2, zi], axis=0)
    iw_ref[...] = jnp.concatenate([s1 / tot, s2 / tot, jnp.zeros((SUBLANES - 2, tm), F32)], axis=0)
    carry = carry_ref[...] + jnp.sum(oh, axis=1, keepdims=True)
    carry_ref[...] = carry
    cnt_ref[...] = carry


def _router(x, n_rows, g, sc, sh, seg_fn, router_wt, router_b):
    d = x.shape[1]
    tm = MOE_TILE
    return pl.pallas_call(
        _router_kernel,
        grid=(n_rows // tm,),
        in_specs=[pl.BlockSpec((tm, d), lambda i: (i, 0)),
                  pl.BlockSpec((1, d), lambda i: (0, 0)),
                  pl.BlockSpec((None, 1, d), lambda i: (seg_fn(i * tm), 0, 0)),
                  pl.BlockSpec((None, 1, d), lambda i: (seg_fn(i * tm), 0, 0)),
                  pl.BlockSpec((N_EXPERTS, d), lambda i: (0, 0)),
                  pl.BlockSpec((N_EXPERTS, 1), lambda i: (0, 0))],
        out_specs=[pl.BlockSpec((tm, TOK_SUB, LANES), lambda i: (i, 0, 0)),
                   pl.BlockSpec((SUBLANES, tm), lambda i: (0, i)),
                   pl.BlockSpec((SUBLANES, tm), lambda i: (0, i)),
                   pl.BlockSpec((N_EXPERTS, LANES), lambda i: (0, 0))],
        out_shape=[jax.ShapeDtypeStruct((n_rows, TOK_SUB, LANES), U32),
                   jax.ShapeDtypeStruct((SUBLANES, n_rows), jnp.int32),
                   jax.ShapeDtypeStruct((SUBLANES, n_rows), F32),
                   jax.ShapeDtypeStruct((N_EXPERTS, LANES), F32)],
        scratch_shapes=[pltpu.VMEM((N_EXPERTS, LANES), F32)],
        compiler_params=_cparams(("arbitrary",)),
        name="router",
    )(x, g.reshape(1, d), sc, sh, router_wt, router_b.reshape(N_EXPERTS, 1))


TOK_SUB = D_MODEL // LANES // 2
U32 = jnp.uint32


def _pack_pair(lo, hi):
    lo_b = lax.bitcast_convert_type(lo.astype(BF16).astype(F32), U32) >> 16
    hi_b = lax.bitcast_convert_type(hi.astype(BF16).astype(F32), U32) & jnp.uint32(0xFFFF0000)
    return hi_b | lo_b


def _unpack_pair(u):
    return (lax.bitcast_convert_type(u << 16, F32), lax.bitcast_convert_type(u & jnp.uint32(0xFFFF0000), F32))


def _store_token_rows(ref, val):
    for s in range(TOK_SUB):
        ref[:, s, :] = _pack_pair(val[:, s * LANES:(s + 1) * LANES],
                                  val[:, (TOK_SUB + s) * LANES:(TOK_SUB + s + 1) * LANES])


def _row_copy(src_ref, o_ref, sem, src_row, dst_row):
    return pltpu.make_async_copy(src_ref.at[pl.ds(src_row, 1)], o_ref.at[pl.ds(dst_row, 1)], sem)


def _expert_kernel(te_ref, nu_ref, xn_ref, w1_ref, w3_ref, w2_ref, o_ref, xb_ref, acc_ref, *, nf):
    i, f = pl.program_id(0), pl.program_id(1)
    compute = (i >= 1) & (i - 1 < nu_ref[0])
    part = xn_ref.shape[0] // nf

    def convert(fv):
        rows = slice(fv * part, (fv + 1) * part)
        for s in range(TOK_SUB):
            lo, hi = _unpack_pair(xn_ref[rows, s, :])
            xb_ref[i % 2, rows, s * LANES:(s + 1) * LANES] = lo.astype(BF16)
            xb_ref[i % 2, rows, (TOK_SUB + s) * LANES:(TOK_SUB + s + 1) * LANES] = hi.astype(BF16)

    for fv in range(nf):
        @pl.when((f == fv) & (i == 0))
        def _(fv=fv):
            convert(fv)

        @pl.when((f == fv) & compute)
        def _(fv=fv):
            xb = xb_ref[(i - 1) % 2]
            a = _silu(_dot(xb, w1_ref[...])) * _dot(xb, w3_ref[...])
            y = _dot(a, w2_ref[...])
            convert(fv)
            if fv > 0:
                y = acc_ref[...] + y
            if fv < nf - 1:
                acc_ref[...] = y
            else:
                _store_token_rows(o_ref, y)

    @pl.when((f == nf - 1) & (i >= 1) & jnp.logical_not(compute))
    def _():
        o_ref[...] = jnp.zeros_like(o_ref)


def _experts(tile_e, n_used, xs, w1, w3, w2, layer):
    p = xs.shape[0]
    d = D_MODEL
    te, tf = MOE_TILE, 512
    nf = D_EXPERT // tf
    ntile = p // te
    tile = lambda i: jnp.maximum(i - 1, 0)
    fidx = lambda i, f, nu: jnp.where((i >= 1) & (i - 1 < nu[0]), f, nf - 1)
    grid_spec = pltpu.PrefetchScalarGridSpec(
        num_scalar_prefetch=2,
        grid=(ntile + 1, nf),
        in_specs=[pl.BlockSpec((te, TOK_SUB, LANES), lambda i, f, te_r, nu: (jnp.minimum(i, ntile - 1), 0, 0)),
                  pl.BlockSpec((None, None, d, tf),
                               lambda i, f, te_r, nu: (layer, te_r[tile(i)], 0, fidx(i, f, nu))),
                  pl.BlockSpec((None, None, d, tf),
                               lambda i, f, te_r, nu: (layer, te_r[tile(i)], 0, fidx(i, f, nu))),
                  pl.BlockSpec((None, None, tf, d),
                               lambda i, f, te_r, nu: (layer, te_r[tile(i)], fidx(i, f, nu), 0))],
        out_specs=pl.BlockSpec((te, TOK_SUB, LANES), lambda i, f, te_r, nu: (tile(i), 0, 0)),
        scratch_shapes=[pltpu.VMEM((2, te, d), BF16), pltpu.VMEM((te, d), F32)])
    return pl.pallas_call(
        functools.partial(_expert_kernel, nf=nf),
        grid_spec=grid_spec,
        out_shape=jax.ShapeDtypeStruct((p, TOK_SUB, LANES), U32),
        compiler_params=_cparams(("arbitrary", "arbitrary")),
        name="experts",
    )(tile_e, n_used, xs, w1, w3, w2)


def _combine_kernel(p1_ref, p2_ref, q1_ref, q2_ref, ys_ref, x_ref, w_ref, g2_ref, fg_ref, o_ref, y_ref, sem, *,
                    final):
    i = pl.program_id(0)
    tm = x_ref.shape[0]
    slot = i % 2

    def fetch(pa_ref, pb_ref, sl):
        def issue(r8, carry):
            for u in range(SUBLANES):
                r = r8 * SUBLANES + u
                _row_copy(ys_ref, y_ref.at[sl, 0], sem.at[sl], pa_ref[r], r).start()
                _row_copy(ys_ref, y_ref.at[sl, 1], sem.at[sl], pb_ref[r], r).start()
            return carry

        lax.fori_loop(0, tm // SUBLANES, issue, 0)

    @pl.when(i == 0)
    def _():
        fetch(p1_ref, p2_ref, 0)

    @pl.when(i + 1 < pl.num_programs(0))
    def _():
        fetch(q1_ref, q2_ref, 1 - slot)

    for k in range(TOP_K):
        pltpu.make_async_copy(ys_ref.at[pl.ds(0, tm)], y_ref.at[slot, k], sem.at[slot]).wait()
    w = w_ref[...]
    w1 = jnp.broadcast_to(w[:, 0:1], (tm, LANES))
    w2 = jnp.broadcast_to(w[:, 1:2], (tm, LANES))
    for s in range(TOK_SUB):
        halves = zip(_unpack_pair(y_ref[slot, 0, :, s, :]), _unpack_pair(y_ref[slot, 1, :, s, :]))
        for cb, (y1, y2) in zip((s, TOK_SUB + s), halves):
            cs = slice(cb * LANES, (cb + 1) * LANES)
            o_ref[:, cs] = x_ref[:, cs] + g2_ref[:, cs] * (w1 * y1 + w2 * y2)
    if final:
        out = o_ref[...]
        o_ref[...] = out * lax.rsqrt(jnp.mean(out * out, axis=-1, keepdims=True) + EPS) * fg_ref[...]


def _combine(x, n_rows, pos, ys, wts, g2, seg_fn, final_g):
    d = x.shape[1]
    tm = MOE_TILE
    nt = n_rows // tm
    fg = jnp.ones((1, d), F32) if final_g is None else final_g.reshape(1, d)
    nxt = lambda i: jnp.minimum(i + 1, nt - 1)
    smem = lambda fn: pl.BlockSpec((tm,), fn, memory_space=pltpu.SMEM)
    return pl.pallas_call(
        functools.partial(_combine_kernel, final=final_g is not None),
        grid=(nt,),
        in_specs=[smem(lambda i: (i,)), smem(lambda i: (nt + i,)),
                  smem(lambda i: (nxt(i),)), smem(lambda i: (nt + nxt(i),)),
                  pl.BlockSpec(memory_space=pl.ANY),
                  pl.BlockSpec((tm, d), lambda i: (i, 0)),
                  pl.BlockSpec((tm, 2), lambda i: (i, 0)),
                  pl.BlockSpec((None, 1, d), lambda i: (seg_fn(i * tm), 0, 0)),
                  pl.BlockSpec((1, d), lambda i: (0, 0))],
        out_specs=pl.BlockSpec((tm, d), lambda i: (i, 0)),
        out_shape=jax.ShapeDtypeStruct((n_rows, d), F32),
        scratch_shapes=[pltpu.VMEM((2, TOP_K, tm, TOK_SUB, LANES), U32), pltpu.SemaphoreType.DMA((2,))],
        compiler_params=_cparams(("arbitrary",)),
        name="combine",
    )(pos, pos, pos, pos, ys, x, wts, g2, fg)


def _dispatch_kernel(p1_ref, p2_ref, ends_ref, hn_ref, xs_ref, zero_ref, sem, zsem, *, n_pos):
    tm = hn_ref.shape[0]
    zrows = zero_ref.shape[0]

    @pl.when(pl.program_id(0) == 0)
    def _():
        zero_ref[...] = jnp.zeros_like(zero_ref)
        starts = [jnp.maximum(ends_ref[e] - zrows, 0) for e in range(N_EXPERTS)]
        tail = [ends_ref[N_EXPERTS - 1] + j * zrows for j in range(N_EXPERTS)]
        for st in starts:
            pltpu.make_async_copy(zero_ref, xs_ref.at[pl.ds(st, zrows)], zsem).start()
        for st in tail:
            @pl.when(st < n_pos)
            def _(st=st):
                pltpu.make_async_copy(zero_ref, xs_ref.at[pl.ds(st, zrows)], zsem).start()
        for st in starts:
            pltpu.make_async_copy(zero_ref, xs_ref.at[pl.ds(st, zrows)], zsem).wait()
        for st in tail:
            @pl.when(st < n_pos)
            def _(st=st):
                pltpu.make_async_copy(zero_ref, xs_ref.at[pl.ds(st, zrows)], zsem).wait()

    def issue(r8, carry):
        for u in range(SUBLANES):
            r = r8 * SUBLANES + u
            _row_copy(hn_ref, xs_ref, sem, r, p1_ref[r]).start()
            _row_copy(hn_ref, xs_ref, sem, r, p2_ref[r]).start()
        return carry

    lax.fori_loop(0, tm // SUBLANES, issue, 0)
    for k in range(TOP_K):
        pltpu.make_async_copy(hn_ref, xs_ref.at[pl.ds(0, tm)], sem).wait()


def _dispatch(pos, ends, hn, n_rows, n_pos):
    tm = MOE_TILE
    nt = n_rows // tm
    return pl.pallas_call(
        functools.partial(_dispatch_kernel, n_pos=n_pos),
        grid=(nt,),
        in_specs=[pl.BlockSpec((tm,), lambda i: (i,), memory_space=pltpu.SMEM),
                  pl.BlockSpec((tm,), lambda i: (nt + i,), memory_space=pltpu.SMEM),
                  pl.BlockSpec(memory_space=pltpu.SMEM),
                  pl.BlockSpec((tm,) + hn.shape[1:], lambda i: (i, 0, 0))],
        out_specs=pl.BlockSpec(memory_space=pl.ANY),
        out_shape=jax.ShapeDtypeStruct((n_pos,) + hn.shape[1:], hn.dtype),
        scratch_shapes=[pltpu.VMEM((tm,) + hn.shape[1:], hn.dtype), pltpu.SemaphoreType.DMA,
                        pltpu.SemaphoreType.DMA],
        compiler_params=_cparams(("arbitrary",)),
        name="dispatch",
    )(pos, pos, ends, hn)


def _moe(x, n_rows, seg_fn, g, sc, sh, g2, router_wt, router_b, w1, w3, w2, layer, final_g=None):
    te = MOE_TILE
    hn, ii, iw, cnt = _router(x, n_rows, g, sc, sh, seg_fn, router_wt, router_b)
    counts = cnt[:, 0].astype(jnp.int32)
    padded = (counts + te - 1) // te * te
    ends = jnp.cumsum(padded)
    base = ends - padded
    pos = jnp.concatenate([base[ii[0]] + ii[2], base[ii[1]] + ii[3]])
    p = 2 * n_rows + N_EXPERTS * te
    ntile = p // te
    n_used = ends[-1] // te
    tiles = jnp.arange(ntile, dtype=jnp.int32)
    tile_e = jnp.sum((jnp.minimum(tiles, n_used - 1)[:, None] * te >= ends[None, :]).astype(jnp.int32), axis=1)
    xs = _dispatch(pos, ends, hn, n_rows, p)
    ys = _experts(tile_e, n_used.reshape(1).astype(jnp.int32), xs, w1, w3, w2, layer)
    return _combine(x, n_rows, pos, ys, iw[:2].T, g2, seg_fn, final_g)


def kernel(x, c, ctx, c_ctx, ada_w, ada_b, norm1_g, norm2_g, ab_w_in, ab_conv_w, ab_conv_b, ml_ig_b, ml_fg_b,
           hg_lb, ml_norm_g, hg_norm_g, ab_w_out, ssd_w_in, ssd_conv_w, ssd_conv_b, ssd_dt_b, ssd_a_log, ssd_d,
           ssd_norm_g, ssd_w_out, router_w, router_b, moe_w1, moe_w3, moe_w2, final_g):
    b, seq, d = x.shape
    ctxl = ctx.shape[1]
    depth = ada_w.shape[0]
    dims = (b, seq, ctxl)
    nl, nc = b * seq, b * ctxl
    rows = seq // GRID_W
    tm = min(512, nc)
    seg_fn = lambda row: jnp.where(row < nl, row // seq, b)
    c8 = jnp.concatenate([c, c_ctx[None], jnp.zeros((SUBLANES - b - 1, d), F32)])
    mods = _modulation(c8, ada_w, ada_b)[:, :b + 1].reshape(depth, b + 1, 6, 1, d)
    lb_all = jnp.cumsum(jax.nn.softmax(hg_lb.astype(F32), axis=0), axis=0)
    router_wt = router_w.T
    pad = jnp.zeros((-(nl + nc) % min(2 * tm, seq), d), F32)
    xr = jnp.concatenate([x.reshape(nl, d), ctx.reshape(nc, d), pad])
    transposed = False
    for l in range(depth):
        sh1, sc1, g1, sh2, sc2, g2 = (mods[l][:, k] for k in range(6))
        keep_ctx = l < depth - 1
        n_out = nl + nc if keep_ctx else nl
        j = l // 2
        if (l % 2 == 1) != transposed:
            xt = _regroup(xr, b, GRID_W if transposed else rows, rows if transposed else GRID_W, nl)
            xr = jnp.concatenate([xt, xr[nl:nl + nc], pad])
            transposed = not transposed
        if l % 2 == 0:
            xr = _ab_layer(xr, dims, seg_fn, tm, n_out, norm1_g[l], sc1, sh1, g1, ab_w_in[j], ab_conv_w[j],
                           ab_conv_b[j], ml_ig_b[j], ml_fg_b[j], lb_all[l], ml_norm_g[j], hg_norm_g[j], ab_w_out[j])
        else:
            xr = _ssd_layer(xr, dims, seg_fn, tm, n_out, norm1_g[l], sc1, sh1, g1, ssd_w_in[j], ssd_conv_w[j],
                            ssd_conv_b[j], ssd_dt_b[j], ssd_a_log[j], ssd_d[j], ssd_norm_g[j], ssd_w_out[j])
        xr = _moe(xr, n_out, seg_fn, norm2_g[l], sc2, sh2, g2, router_wt, router_b, moe_w1, moe_w3, moe_w2, l,
                  final_g if l == depth - 1 else None)
    if transposed:
        xr = _regroup(xr, b, GRID_W, rows, nl)
    return xr[:nl].reshape(b, seq, d)
```

```python
import functools
import math

import jax
import jax.numpy as jnp
from jax import lax
from jax.experimental import pallas as pl
from jax.experimental.pallas import tpu as pltpu

F32 = jnp.float32
BF16 = jnp.bfloat16
HI = lax.Precision.HIGHEST

D_MODEL = 2048
GRID_W = 64
EPS = 1e-6
CHUNK = 64
CONV_K = 5
ML_H, ML_DK, ML_DV = 4, 128, 256
HG_H, HG_DK, HG_DV = 8, 128, 128
ML_QK, ML_V = ML_H * ML_DK, ML_H * ML_DV
HG_K, HG_V = HG_H * HG_DK, HG_H * HG_DV
AB_OUT = ML_V + HG_V
D_INNER = 2 * D_MODEL
SSD_P, SSD_G, SSD_N = 64, 8, 128
SSD_H = D_INNER // SSD_P
SSD_R = SSD_H // SSD_G
N_EXPERTS, N_GROUPS, TOP_K, D_EXPERT = 16, 4, 2, 1024
EXPERTS_PER_GROUP = N_EXPERTS // N_GROUPS

LANES = 128
SUBLANES = 8
VMEM_LIMIT = 56 * 1024 * 1024

AB_S_Q, AB_S_K, AB_S_V, AB_S_OG, AB_S_HQ, AB_S_HF, AB_S_HI, AB_S_HG, AB_S_GATE = 0, 4, 8, 16, 24, 32, 48, 56, 64
AB_SLABS = 66
SSD_S_Z, SSD_S_X, SSD_S_B, SSD_S_C, SSD_S_DT = 0, 32, 64, 72, 80
SSD_SLABS = 81


def _cparams(sem):
    return pltpu.CompilerParams(dimension_semantics=sem, vmem_limit_bytes=VMEM_LIMIT)


def _silu(x):
    return x * jax.nn.sigmoid(x)


def _dot(a, b):
    return jnp.dot(a.astype(BF16), b.astype(BF16), preferred_element_type=F32)


def _dot_nt(a, b):
    return lax.dot_general(a.astype(BF16), b.astype(BF16), (((1,), (1,)), ((), ())),
                           preferred_element_type=F32)


def _dot_tn(a, b):
    return lax.dot_general(a.astype(BF16), b.astype(BF16), (((0,), (0,)), ((), ())),
                           preferred_element_type=F32)


def _dot_hi(a, b):
    return jnp.dot(a, b, precision=HI, preferred_element_type=F32)


def _tri(rev):
    t = lax.broadcasted_iota(jnp.int32, (CHUNK, CHUNK), 0)
    s = lax.broadcasted_iota(jnp.int32, (CHUNK, CHUNK), 1)
    return (s >= t) if rev else (s <= t)


def _mod_kernel(c_ref, w_ref, b_ref, o_ref):
    c = c_ref[...]
    o_ref[...] = _dot(_silu(c), w_ref[...]) + b_ref[...]


def _modulation(c8, ada_w, ada_b):
    depth, d, d6 = ada_w.shape
    tn = 1024
    return pl.pallas_call(
        _mod_kernel,
        grid=(depth, d6 // tn),
        in_specs=[pl.BlockSpec((SUBLANES, d), lambda l, j: (0, 0)),
                  pl.BlockSpec((None, d, tn), lambda l, j: (l, 0, j)),
                  pl.BlockSpec((None, 1, tn), lambda l, j: (l, 0, j))],
        out_specs=pl.BlockSpec((None, SUBLANES, tn), lambda l, j: (l, 0, j)),
        out_shape=jax.ShapeDtypeStruct((depth, SUBLANES, d6), F32),
        compiler_params=_cparams(("parallel", "parallel")),
        name="modulation",
    )(c8, ada_w, ada_b.reshape(depth, 1, d6))


def _inproj_kernel(x_ref, g_ref, sc_ref, sh_ref, w_ref, o_ref, h_ref):
    @pl.when(pl.program_id(1) == 0)
    def _():
        x = x_ref[...]
        y = x * lax.rsqrt(jnp.mean(x * x, axis=-1, keepdims=True) + EPS) * g_ref[...]
        h_ref[...] = (y * (1.0 + sc_ref[...]) + sh_ref[...]).astype(BF16)

    acc = jnp.dot(h_ref[...], w_ref[...], preferred_element_type=F32)
    for s in range(o_ref.shape[0]):
        o_ref[s] = acc[:, s * LANES:(s + 1) * LANES]


def _inproj(x, g, sc, sh, w, seg_fn, tm, tn):
    n, d = x.shape
    ncols = w.shape[1]
    nsl = tn // LANES
    return pl.pallas_call(
        _inproj_kernel,
        grid=(n // tm, ncols // tn),
        in_specs=[pl.BlockSpec((tm, d), lambda i, j: (i, 0)),
                  pl.BlockSpec((1, d), lambda i, j: (0, 0)),
                  pl.BlockSpec((None, 1, d), lambda i, j: (seg_fn(i * tm), 0, 0)),
                  pl.BlockSpec((None, 1, d), lambda i, j: (seg_fn(i * tm), 0, 0)),
                  pl.BlockSpec((d, tn), lambda i, j: (0, j))],
        out_specs=pl.BlockSpec((nsl, tm, LANES), lambda i, j: (j, i, 0)),
        out_shape=jax.ShapeDtypeStruct((ncols // LANES, n, LANES), F32),
        scratch_shapes=[pltpu.VMEM((tm, d), BF16)],
        compiler_params=_cparams(("parallel", "arbitrary")),
        name="inproj",
    )(x, g.reshape(1, d), sc, sh, w)


def _conv_kernel(prev_ref, cur_ref, next_ref, w_ref, o_ref, *, tt, tiles_lat, n_lat_tiles, tiles_ctx):
    i = pl.program_id(1)
    in_lat = i < n_lat_tiles
    pos = jnp.where(in_lat, i % tiles_lat, (i - n_lat_tiles) % tiles_ctx)
    last = jnp.where(in_lat, tiles_lat - 1, tiles_ctx - 1)
    keep_prev = (pos != 0).astype(F32)
    keep_next = (pos != last).astype(F32)
    ext = jnp.concatenate([prev_ref[...] * keep_prev, cur_ref[...], next_ref[...] * keep_next], axis=1)
    w = w_ref[...]
    acc = jnp.zeros(cur_ref.shape, F32) + w[:, CONV_K:CONV_K + 1, :]
    for k in range(CONV_K):
        shift = (CONV_K // 2 - k) % (tt + 2 * SUBLANES)
        r = ext if shift == 0 else pltpu.roll(ext, shift, 1)
        acc = acc + r[:, SUBLANES:SUBLANES + tt, :] * w[:, k:k + 1, :]
    o_ref[...] = (_silu(acc) * w[:, CONV_K + 1:CONV_K + 2, :]).astype(o_ref.dtype)


def _conv_slabs(p3, slab0, nslab, wpack, dims, sb):
    b, seq, ctxl = dims
    n = b * (seq + ctxl)
    tt = min(256, ctxl)
    t8 = tt // SUBLANES
    nblk8 = n // SUBLANES
    s0 = slab0 // sb
    kern = functools.partial(_conv_kernel, tt=tt, tiles_lat=seq // tt, n_lat_tiles=b * seq // tt,
                             tiles_ctx=ctxl // tt)
    return pl.pallas_call(
        kern,
        grid=(nslab // sb, n // tt),
        in_specs=[pl.BlockSpec((sb, SUBLANES, LANES), lambda s, i: (s0 + s, jnp.maximum(i * t8 - 1, 0), 0)),
                  pl.BlockSpec((sb, tt, LANES), lambda s, i: (s0 + s, i, 0)),
                  pl.BlockSpec((sb, SUBLANES, LANES),
                               lambda s, i: (s0 + s, jnp.minimum((i + 1) * t8, nblk8 - 1), 0)),
                  pl.BlockSpec((sb, SUBLANES, LANES), lambda s, i: (s, 0, 0))],
        out_specs=pl.BlockSpec((sb, tt, LANES), lambda s, i: (s, i, 0)),
        out_shape=jax.ShapeDtypeStruct((nslab, n, LANES), BF16),
        compiler_params=_cparams(("parallel", "parallel")),
        name="conv",
    )(p3, p3, p3, wpack)


def _conv_pack(conv_w, conv_b, scale):
    c = conv_w.shape[1]
    rows = jnp.concatenate([conv_w, conv_b[None], scale[None], jnp.zeros((1, c), F32)], axis=0)
    return rows.reshape(SUBLANES, c // LANES, LANES).transpose(1, 0, 2)


def _chunk_maps(dims):
    b, seq, ctxl = dims
    ncc, nlc = ctxl // CHUNK, seq // CHUNK

    def fwd(bi, i):
        return jnp.where(i < ncc, b * nlc + bi * ncc + i, bi * nlc + (i - ncc))

    def bwd(bi, i):
        return jnp.where(i < ncc, b * nlc + bi * ncc + (ncc - 1 - i), bi * nlc + (nlc - 1 - (i - ncc)))

    return fwd, bwd, ncc + nlc


def _mlstm_kernel(qkf_ref, vf_ref, gcf_ref, grf_ref, qkb_ref, vb_ref, gcb_ref, grb_ref, bc_ref, br_ref,
                  of_ref, ob_ref, c_ref, n_ref, m_ref):
    @pl.when(pl.program_id(1) == 0)
    def _():
        c_ref[...] = jnp.zeros_like(c_ref)
        n_ref[...] = jnp.zeros_like(n_ref)
        m_ref[...] = jnp.zeros_like(m_ref)

    ng = 2 * ML_H
    for d, (qk_ref, v_ref, gc_ref, gr_ref, o_ref) in enumerate(
            ((qkf_ref, vf_ref, gcf_ref, grf_ref, of_ref), (qkb_ref, vb_ref, gcb_ref, grb_ref, ob_ref))):
        rev = d == 1
        mask = _tri(rev)
        gcol = gc_ref[0][:, :2 * ng] + bc_ref[...]
        grow = gr_ref[...] + br_ref[...]
        ic_all = gcol[:, d * ML_H:(d + 1) * ML_H]
        lfc_all = jax.nn.log_sigmoid(gcol[:, ng + d * ML_H:ng + (d + 1) * ML_H])
        ir_all = grow[d * ML_H:(d + 1) * ML_H, :]
        lfr_all = jax.nn.log_sigmoid(grow[ng + d * ML_H:ng + (d + 1) * ML_H, :])
        incl = mask.astype(F32)
        bcol_all = _dot_hi(incl, lfc_all)
        brow_all = _dot_hi(lfr_all, _tri(not rev).astype(F32))
        last = 0 if rev else CHUNK - 1
        for h in range(ML_H):
            q = qk_ref[h]
            k = qk_ref[ML_H + h]
            v = jnp.concatenate([v_ref[2 * h], v_ref[2 * h + 1]], axis=1)
            bcol, icol = bcol_all[:, h:h + 1], ic_all[:, h:h + 1]
            brow, irow = brow_all[h:h + 1, :], ir_all[h:h + 1, :]
            cst = c_ref[d, h]
            nst = n_ref[d, h]
            mprev = m_ref[d, h][:, :1]
            logd = jnp.where(mask, bcol - brow + irow, -jnp.inf)
            inter = bcol + mprev
            m_t = jnp.maximum(inter, jnp.max(logd, axis=1, keepdims=True))
            s = _dot_nt(q, k) * jnp.exp(logd - m_t)
            sc = jnp.exp(inter - m_t)
            num = _dot(s, v) + sc * _dot_nt(q, cst)
            den = jnp.sum(s, axis=1, keepdims=True) + sc * jnp.sum(q * nst, axis=1, keepdims=True)
            hout = num / jnp.maximum(jnp.abs(den), jnp.exp(-m_t))
            o_ref[2 * h] = hout[:, :LANES]
            o_ref[2 * h + 1] = hout[:, LANES:]
            b_last = bcol[last:last + 1, :]
            wlog = b_last - bcol + icol
            m_new = jnp.maximum(b_last + mprev, jnp.max(wlog, axis=0, keepdims=True))
            w = jnp.exp(wlog - m_new)
            dec = jnp.exp(b_last + mprev - m_new)
            c_ref[d, h] = dec * cst + _dot_tn(w * v, k)
            n_ref[d, h] = dec * nst + jnp.sum(w * k, axis=0, keepdims=True)
            m_ref[d, h] = jnp.broadcast_to(m_new, (1, LANES))


def _mlstm(qk, p3, grow, bcol, brow, dims):
    b = dims[0]
    n = qk.shape[1]
    fwd, bwd, nch = _chunk_maps(dims)

    def specs(cm):
        return [pl.BlockSpec((2 * ML_H, CHUNK, LANES), lambda bi, i: (0, cm(bi, i), 0)),
                pl.BlockSpec((2 * ML_H, CHUNK, LANES), lambda bi, i: (AB_S_V // (2 * ML_H), cm(bi, i), 0)),
                pl.BlockSpec((1, CHUNK, LANES), lambda bi, i: (AB_S_GATE, cm(bi, i), 0)),
                pl.BlockSpec((None, 4 * ML_H, CHUNK), lambda bi, i: (cm(bi, i), 0, 0))]

    out_sd = jax.ShapeDtypeStruct((2 * ML_H, n, LANES), F32)
    return pl.pallas_call(
        _mlstm_kernel,
        grid=(b, nch),
        in_specs=specs(fwd) + specs(bwd) + [pl.BlockSpec((1, 4 * ML_H), lambda bi, i: (0, 0)),
                                            pl.BlockSpec((4 * ML_H, 1), lambda bi, i: (0, 0))],
        out_specs=[pl.BlockSpec((2 * ML_H, CHUNK, LANES), lambda bi, i: (0, fwd(bi, i), 0)),
                   pl.BlockSpec((2 * ML_H, CHUNK, LANES), lambda bi, i: (0, bwd(bi, i), 0))],
        out_shape=[out_sd, out_sd],
        scratch_shapes=[pltpu.VMEM((2, ML_H, ML_DV, ML_DK), F32),
                        pltpu.VMEM((2, ML_H, 1, ML_DK), F32),
                        pltpu.VMEM((2, ML_H, 1, LANES), F32)],
        compiler_params=_cparams(("parallel", "arbitrary")),
        name="mlstm",
    )(qk, p3, p3, grow, qk, p3, p3, grow, bcol, brow)


def _bcast_rows(a, rows, span):
    parts = [jnp.broadcast_to(a[r:r + 1, :], (span, a.shape[1])) for r in rows]
    return parts[0] if len(parts) == 1 else jnp.concatenate(parts, axis=0)


def _hgrn2_kernel(qf_ref, ff_ref, vf_ref, qb_ref, fb_ref, vb_ref, lb_ref, of_ref, ob_ref, s_ref):
    @pl.when(pl.program_id(1) == 0)
    def _():
        s_ref[...] = jnp.zeros_like(s_ref)

    t = lax.broadcasted_iota(jnp.int32, (CHUNK, CHUNK), 0)
    s = lax.broadcasted_iota(jnp.int32, (CHUNK, CHUNK), 1)
    refs = ((qf_ref, ff_ref, vf_ref, of_ref), (qb_ref, fb_ref, vb_ref, ob_ref))
    units = [(d, h) for d in range(2) for h in range(HG_H)]
    levels = (32, 16, 8)

    def level_mask(m, rev):
        same = (t // (2 * m)) == (s // (2 * m))
        t_late = ((t // m) % 2 == 0) if rev else ((t // m) % 2 == 1)
        s_early = ((s // m) % 2 == 1) if rev else ((s // m) % 2 == 0)
        return same & t_late & s_early

    p1 = {}
    for d, h in units:
        q_ref, f_ref, v_ref, _ = refs[d]
        lb = lb_ref[h]
        f = lb + (1.0 - lb) * jax.nn.sigmoid(f_ref[h])
        lg = jnp.log(f)
        a = _dot_hi(_tri(d == 1).astype(F32), lg)
        p1[d, h] = (q_ref[h], 1.0 - f, lg, v_ref[h], a, s_ref[d, h])
    p2 = {}
    for d, h in units:
        rev = d == 1
        q, k, lg, v, a, st = p1[d, h]
        last = 0 if rev else CHUNK - 1
        qs, ks = [], []
        for m in levels:
            nb = CHUNK // (2 * m)
            aref = _bcast_rows(a, [bi * 2 * m + (m if rev else m - 1) for bi in range(nb)], 2 * m)
            qs.append((q * jnp.exp(jnp.minimum(a - aref, 0.0))).astype(BF16))
            ks.append((k * jnp.exp(jnp.minimum(aref - a, 0.0))).astype(BF16))
        aref = _bcast_rows(a - lg, [bi * SUBLANES + (SUBLANES - 1 if rev else 0) for bi in range(CHUNK // SUBLANES)],
                           SUBLANES)
        qs.append((q * jnp.exp(a - aref)).astype(BF16))
        ks.append((k * jnp.exp(aref - a)).astype(BF16))
        a_last = a[last:last + 1, :]
        p2[d, h] = (qs, ks, (q * jnp.exp(a)).astype(BF16), (k * jnp.exp(a_last - a)).astype(BF16), jnp.exp(a_last))
    p3 = {}
    for d, h in units:
        qs, ks, qdec, kdec, edec = p2[d, h]
        st = p1[d, h][5]
        p3[d, h] = ([_dot_nt(qq, kk) for qq, kk in zip(qs, ks)], _dot_nt(qdec, st))
    for d, h in units:
        rev = d == 1
        prods, qst = p3[d, h]
        scores = jnp.zeros((CHUNK, CHUNK), F32)
        for m, pr in zip(levels, prods[:-1]):
            scores = scores + jnp.where(level_mask(m, rev), pr, 0.0)
        diag = ((t // SUBLANES) == (s // SUBLANES)) & _tri(rev)
        scores = scores + jnp.where(diag, prods[-1], 0.0)
        refs[d][3][h] = _dot(scores, p1[d, h][3]) + qst
    for d, h in units:
        v, st = p1[d, h][3], p1[d, h][5]
        kdec, edec = p2[d, h][3], p2[d, h][4]
        s_ref[d, h] = st * edec + _dot_tn(v, kdec)


def _hgrn2(p3, lb, dims):
    b = dims[0]
    n = b * (dims[1] + dims[2])
    fwd, bwd, nch = _chunk_maps(dims)

    def specs(cm, d):
        return [pl.BlockSpec((HG_H, CHUNK, LANES), lambda bi, i: (AB_S_HQ // HG_H, cm(bi, i), 0)),
                pl.BlockSpec((HG_H, CHUNK, LANES), lambda bi, i: (AB_S_HF // HG_H + d, cm(bi, i), 0)),
                pl.BlockSpec((HG_H, CHUNK, LANES), lambda bi, i: (AB_S_HI // HG_H, cm(bi, i), 0))]

    out_sd = jax.ShapeDtypeStruct((HG_H, n, LANES), F32)
    return pl.pallas_call(
        _hgrn2_kernel,
        grid=(b, nch),
        in_specs=specs(fwd, 0) + specs(bwd, 1) + [pl.BlockSpec((HG_H, 1, LANES), lambda bi, i: (0, 0, 0))],
        out_specs=[pl.BlockSpec((HG_H, CHUNK, LANES), lambda bi, i: (0, fwd(bi, i), 0)),
                   pl.BlockSpec((HG_H, CHUNK, LANES), lambda bi, i: (0, bwd(bi, i), 0))],
        out_shape=[out_sd, out_sd],
        scratch_shapes=[pltpu.VMEM((2, HG_H, HG_DV, HG_DK), F32)],
        compiler_params=_cparams(("parallel", "arbitrary")),
        name="hgrn2",
    )(p3, p3, p3, p3, p3, p3, lb)


def _ab_out_kernel(mf_ref, mb_ref, hf_ref, hb_ref, og_ref, hg_ref, mlg_ref, hgg_ref, w_ref, x_ref, g1_ref,
                   o_ref, lhs_ref):
    for h in range(ML_H):
        hs = jnp.concatenate([mf_ref[2 * h] + mb_ref[2 * h], mf_ref[2 * h + 1] + mb_ref[2 * h + 1]], axis=1)
        r = hs * lax.rsqrt(jnp.mean(hs * hs, axis=-1, keepdims=True) + EPS)
        og = jnp.concatenate([og_ref[2 * h], og_ref[2 * h + 1]], axis=1)
        y = jax.nn.sigmoid(og) * (r * mlg_ref[:, h * ML_DV:(h + 1) * ML_DV])
        lhs_ref[:, h * ML_DV:(h + 1) * ML_DV] = y.astype(BF16)
    for h in range(HG_H):
        hs = hf_ref[h] + hb_ref[h]
        r = hs * lax.rsqrt(jnp.mean(hs * hs, axis=-1, keepdims=True) + EPS)
        y = _silu(hg_ref[h]) * (r * hgg_ref[:, h * HG_DV:(h + 1) * HG_DV])
        lhs_ref[:, ML_V + h * HG_DV:ML_V + (h + 1) * HG_DV] = y.astype(BF16)
    acc = jnp.dot(lhs_ref[...], w_ref[...], preferred_element_type=F32)
    o_ref[...] = x_ref[...] + g1_ref[...] * acc


def _ab_out(hm, ho, p3, ml_g, hg_g, w_out, x, g1, seg_fn, n, tm):
    d = x.shape[1]
    slab8 = lambda idx: pl.BlockSpec((SUBLANES, tm, LANES), lambda i: (idx, i, 0))
    return pl.pallas_call(
        _ab_out_kernel,
        grid=(n // tm,),
        in_specs=[slab8(0), slab8(0), slab8(0), slab8(0), slab8(AB_S_OG // SUBLANES), slab8(AB_S_HG // SUBLANES),
                  pl.BlockSpec((1, ML_V), lambda i: (0, 0)),
                  pl.BlockSpec((1, HG_V), lambda i: (0, 0)),
                  pl.BlockSpec((AB_OUT, d), lambda i: (0, 0)),
                  pl.BlockSpec((tm, d), lambda i: (i, 0)),
                  pl.BlockSpec((None, 1, d), lambda i: (seg_fn(i * tm), 0, 0))],
        out_specs=pl.BlockSpec((tm, d), lambda i: (i, 0)),
        out_shape=jax.ShapeDtypeStruct((n, d), F32),
        scratch_shapes=[pltpu.VMEM((tm, AB_OUT), BF16)],
        compiler_params=_cparams(("parallel",)),
        name="ab_out",
    )(hm[0], hm[1], ho[0], ho[1], p3, p3, ml_g.reshape(1, ML_V), hg_g.reshape(1, HG_V), w_out, x, g1)


def _ab_weight(w_in):
    d = w_in.shape[0]
    o = [0, 2 * ML_QK, 2 * ML_QK + ML_V, 2 * ML_QK + 2 * ML_V]
    g0 = o[3]
    h0 = g0 + 4 * ML_H
    pad = AB_SLABS * LANES - (w_in.shape[1])
    return jnp.concatenate([w_in[:, :g0], w_in[:, h0:], w_in[:, g0:h0], jnp.zeros((d, pad), w_in.dtype)],
                           axis=1).astype(BF16)


def _ab_layer(x, dims, seg_fn, tm, n_out, g_norm, sc, sh, g1, w_in, conv_w, conv_b, ig_b, fg_b, lb, ml_g, hg_g,
              w_out):
    b, seq, ctxl = dims
    n = b * (seq + ctxl)
    p3 = _inproj(x, g_norm, sc, sh, _ab_weight(w_in), seg_fn, min(2 * tm, seq), 11 * LANES)
    kscale = jnp.concatenate([jnp.ones((ML_QK,), F32), jnp.full((ML_QK,), ML_DK ** -0.5, F32)])
    qk = _conv_slabs(p3, AB_S_Q, 2 * ML_H, _conv_pack(conv_w, conv_b, kscale), dims, 2 * ML_H)
    gates = p3[AB_S_GATE, :n, :4 * ML_H]
    grow = gates.reshape(n // CHUNK, CHUNK, 4 * ML_H).transpose(0, 2, 1)
    gbias = jnp.concatenate([ig_b.reshape(-1), fg_b.reshape(-1)])
    hm = _mlstm(qk, p3, grow, gbias.reshape(1, -1), gbias.reshape(-1, 1), dims)
    ho = _hgrn2(p3, lb.reshape(HG_H, 1, HG_DK), dims)
    return _ab_out(hm, ho, p3, ml_g, hg_g, w_out.astype(BF16), x, g1, seg_fn, n_out, min(tm, 256))


def _ssd_kernel(xf_ref, dcf_ref, drf_ref, xb_ref, dcb_ref, drb_ref, bc_ref, br_ref, ac_ref, ar_ref,
                of_ref, ob_ref, h_ref):
    @pl.when(pl.program_id(1) == 0)
    def _():
        h_ref[...] = jnp.zeros_like(h_ref)

    gw = SSD_R * SSD_P
    t_idx = lax.broadcasted_iota(jnp.int32, (CHUNK, gw), 0)
    s_idx = lax.broadcasted_iota(jnp.int32, (CHUNK, gw), 1) % SSD_P
    e_r = lax.broadcasted_iota(jnp.int32, (3 * SSD_R, gw), 0) % SSD_R
    e_c = lax.broadcasted_iota(jnp.int32, (3 * SSD_R, gw), 1) // SSD_P
    expand3 = (e_r == e_c).astype(BF16)
    low_half = lax.broadcasted_iota(jnp.int32, (CHUNK, LANES), 1) < SSD_P

    def expand(v):
        hi = v.astype(BF16).astype(F32)
        mid = (v - hi).astype(BF16).astype(F32)
        lo = v - hi - mid
        return jnp.dot(jnp.concatenate([hi, mid, lo], axis=1).astype(BF16), expand3, preferred_element_type=F32)

    refs = ((xf_ref, dcf_ref, drf_ref, of_ref), (xb_ref, dcb_ref, drb_ref, ob_ref))
    units = [(d, g) for d in range(2) for g in range(SSD_G)]
    gate = []
    for d, (x_ref, dc_ref, dr_ref, o_ref) in enumerate(refs):
        rev = d == 1
        mask = (s_idx >= t_idx) if rev else (s_idx <= t_idx)
        last = 0 if rev else CHUNK - 1
        hs = slice(d * SSD_H, (d + 1) * SSD_H)
        dtc = jax.nn.softplus(dc_ref[0][:, hs] + bc_ref[:, hs])
        lac = dtc * ac_ref[:, hs]
        dtr = jax.nn.softplus(dr_ref[hs, :] + br_ref[hs, :])
        lar = dtr * ar_ref[hs, :]
        cum_c = _dot_hi(_tri(rev).astype(F32), lac)
        cum_r = _dot_hi(lar, _tri(not rev).astype(F32))
        wgt = jnp.exp(cum_c[last:last + 1, :] - cum_c) * dtc
        gate.append((mask, last, cum_c, cum_r, dtr, wgt))
    p1 = {}
    for d, g in units:
        x_ref = refs[d][0]
        mask, last, cum_c, cum_r, dtr, wgt = gate[d]
        heads = slice(g * SSD_R, (g + 1) * SSD_R)
        x = jnp.concatenate([x_ref[4 * g + j] for j in range(4)], axis=1)
        bm = x_ref[4 * SSD_G + g]
        cm = x_ref[5 * SSD_G + g]
        hst = h_ref[d, g]
        p1[d, g] = (x, bm, hst, expand(cum_c[:, heads]), expand(wgt[:, heads]), _dot_nt(cm, bm), _dot(cm, hst))
    p2 = {}
    for d, g in units:
        mask, last, cum_c, cum_r, dtr, wgt = gate[d]
        x, bm, hst, cum_x, wgt_x, cb, yoff = p1[d, g]
        rows = lambda a: jnp.concatenate([a[g * SSD_R + r:g * SSD_R + r + 1, :] for r in range(SSD_R)], axis=1)
        seg = jnp.exp(jnp.where(mask, cum_x - rows(cum_r), -jnp.inf)) * rows(dtr)
        cb2 = jnp.concatenate([cb, cb], axis=1)
        ms = [(cb2 * seg[:, j * LANES:(j + 1) * LANES]).astype(BF16) for j in range(4)]
        p2[d, g] = (ms, jnp.exp(cum_x), (x * wgt_x).astype(BF16))
    for d, g in units:
        o_ref = refs[d][3]
        x, yoff = p1[d, g][0], p1[d, g][6]
        ms, ecum_x, xw = p2[d, g]
        for j in range(4):
            ls = slice(j * LANES, (j + 1) * LANES)
            xp = x[:, ls]
            bd = jnp.concatenate([jnp.where(low_half, xp, 0.0), jnp.where(low_half, 0.0, xp)], axis=0)
            o_ref[4 * g + j] = (_dot(ms[j], bd) + ecum_x[:, ls] * yoff[:, ls]).astype(o_ref.dtype)
    for d, g in units:
        last = gate[d][1]
        bm, hst = p1[d, g][1], p1[d, g][2]
        ms, ecum_x, xw = p2[d, g]
        h_ref[d, g] = hst * ecum_x[last:last + 1, :] + _dot_tn(bm, xw)


def _ssd_scan(xbc, p3, dtrow, dt_b, neg_a, dims):
    b = dims[0]
    n = xbc.shape[1]
    fwd, bwd, nch = _chunk_maps(dims)
    nxs = xbc.shape[0]

    def specs(cm):
        return [pl.BlockSpec((nxs, CHUNK, LANES), lambda bi, i: (0, cm(bi, i), 0)),
                pl.BlockSpec((1, CHUNK, LANES), lambda bi, i: (SSD_S_DT, cm(bi, i), 0)),
                pl.BlockSpec((None, 2 * SSD_H, CHUNK), lambda bi, i: (cm(bi, i), 0, 0))]

    vec = lambda shape: pl.BlockSpec(shape, lambda bi, i: (0, 0))
    out_sd = jax.ShapeDtypeStruct((D_INNER // LANES, n, LANES), BF16)
    return pl.pallas_call(
        _ssd_kernel,
        grid=(b, nch),
        in_specs=specs(fwd) + specs(bwd) + [vec((1, 2 * SSD_H)), vec((2 * SSD_H, 1)),
                                            vec((1, 2 * SSD_H)), vec((2 * SSD_H, 1))],
        out_specs=[pl.BlockSpec((D_INNER // LANES, CHUNK, LANES), lambda bi, i: (0, fwd(bi, i), 0)),
                   pl.BlockSpec((D_INNER // LANES, CHUNK, LANES), lambda bi, i: (0, bwd(bi, i), 0))],
        out_shape=[out_sd, out_sd],
        scratch_shapes=[pltpu.VMEM((2, SSD_G, SSD_N, SSD_R * SSD_P), F32)],
        compiler_params=_cparams(("parallel", "arbitrary")),
        name="ssd_scan",
    )(xbc, p3, dtrow, xbc, p3, dtrow, dt_b.reshape(1, -1), dt_b.reshape(-1, 1),
      neg_a.reshape(1, -1), neg_a.reshape(-1, 1))


def _ssd_out_kernel(yf_ref, yb_ref, xs_ref, z_ref, dsk_ref, ng_ref, w_ref, x_ref, g1_ref, o_ref, lhs_ref, acc_ref):
    k = pl.program_id(1)

    @pl.when(k == 0)
    def _():
        acc_ref[...] = jnp.zeros_like(acc_ref)

    gw = D_INNER // SSD_G
    for gg in range(lhs_ref.shape[1] // gw):
        cat = lambda ref: jnp.concatenate([ref[4 * gg + j] for j in range(4)], axis=1).astype(F32)
        cs = slice(gg * gw, (gg + 1) * gw)
        y = cat(yf_ref) + cat(yb_ref) + dsk_ref[:, cs] * cat(xs_ref)
        u = y * _silu(cat(z_ref))
        u = u * lax.rsqrt(jnp.mean(u * u, axis=-1, keepdims=True) + EPS) * ng_ref[:, cs]
        lhs_ref[:, cs] = u.astype(BF16)
    acc_ref[...] += jnp.dot(lhs_ref[...], w_ref[...], preferred_element_type=F32)

    @pl.when(k == pl.num_programs(1) - 1)
    def _():
        o_ref[...] = x_ref[...] + g1_ref[...] * acc_ref[...]


def _ssd_out(yf, yb, xbc, p3, dskip, norm_g, w_out, x, g1, seg_fn, n_rows, tm):
    d = x.shape[1]
    tk = 1024
    nsl = tk // LANES
    slab = lambda: pl.BlockSpec((nsl, tm, LANES), lambda i, k: (k, i, 0))
    return pl.pallas_call(
        _ssd_out_kernel,
        grid=(n_rows // tm, D_INNER // tk),
        in_specs=[slab(), slab(), slab(), slab(),
                  pl.BlockSpec((1, tk), lambda i, k: (0, k)),
                  pl.BlockSpec((1, tk), lambda i, k: (0, k)),
                  pl.BlockSpec((tk, d), lambda i, k: (k, 0)),
                  pl.BlockSpec((tm, d), lambda i, k: (i, 0)),
                  pl.BlockSpec((None, 1, d), lambda i, k: (seg_fn(i * tm), 0, 0))],
        out_specs=pl.BlockSpec((tm, d), lambda i, k: (i, 0)),
        out_shape=jax.ShapeDtypeStruct((n_rows, d), F32),
        scratch_shapes=[pltpu.VMEM((tm, tk), BF16), pltpu.VMEM((tm, d), F32)],
        compiler_params=_cparams(("parallel", "arbitrary")),
        name="ssd_out",
    )(yf, yb, xbc, p3, dskip, norm_g.reshape(1, -1), w_out, x, g1)


def _ssd_layer(x, dims, seg_fn, tm, n_out, g_norm, sc, sh, g1, w_in, conv_w, conv_b, dt_b, a_log, d_skip,
               norm_g, w_out):
    b, seq, ctxl = dims
    n = b * (seq + ctxl)
    p3 = _inproj(x, g_norm, sc, sh, w_in.astype(BF16), seg_fn, min(2 * tm, seq), 9 * LANES)
    nconv = conv_w.shape[1]
    xbc = _conv_slabs(p3, SSD_S_X, nconv // LANES, _conv_pack(conv_w, conv_b, jnp.ones((nconv,), F32)), dims, 16)
    dt = p3[SSD_S_DT, :n]
    dtrow = dt.reshape(n // CHUNK, CHUNK, 2 * SSD_H).transpose(0, 2, 1)
    yf, yb = _ssd_scan(xbc, p3, dtrow, dt_b.reshape(-1), -jnp.exp(a_log.astype(F32)).reshape(-1), dims)
    dskip = jnp.repeat(d_skip.astype(F32), SSD_P).reshape(1, D_INNER)
    return _ssd_out(yf, yb, xbc, p3, dskip, norm_g, w_out.astype(BF16), x, g1, seg_fn, n_out, tm)


def _regroup_kernel(x_ref, o_ref):
    a = x_ref.shape[0]
    for cc in range(SUBLANES):
        o_ref[cc * a:(cc + 1) * a, :] = x_ref[:, cc, :]


def _regroup(x, b, a, c, n_out):
    d = x.shape[1]
    cblk = c // SUBLANES
    return pl.pallas_call(
        _regroup_kernel,
        grid=(b, cblk),
        in_specs=[pl.BlockSpec((a, SUBLANES, d), lambda bi, j: (bi, j, 0))],
        out_specs=pl.BlockSpec((a * SUBLANES, d), lambda bi, j: (bi * cblk + j, 0)),
        out_shape=jax.ShapeDtypeStruct((n_out, d), F32),
        compiler_params=_cparams(("parallel", "parallel")),
        name="regroup",
    )(x.reshape(x.shape[0] // c, c, d))


MOE_TILE = 512


def _router_kernel(x_ref, g_ref, sc_ref, sh_ref, wt_ref, rb_ref, hn_ref, ii_ref, iw_ref, cnt_ref, carry_ref):
    @pl.when(pl.program_id(0) == 0)
    def _():
        carry_ref[...] = jnp.zeros_like(carry_ref)

    x = x_ref[...]
    tm = x.shape[0]
    y = x * lax.rsqrt(jnp.mean(x * x, axis=-1, keepdims=True) + EPS) * g_ref[...]
    hn = y * (1.0 + sc_ref[...]) + sh_ref[...]
    _store_packed_rows(hn_ref, hn)
    logits = lax.dot_general(wt_ref[...], hn, (((1,), (1,)), ((), ())), precision=HI,
                             preferred_element_type=F32)
    score = jax.nn.sigmoid(logits)
    biased = score + rb_ref[...]
    rb = [biased[e:e + 1, :] for e in range(N_EXPERTS)]
    rs = [score[e:e + 1, :] for e in range(N_EXPERTS)]
    gsc = []
    for g in range(N_GROUPS):
        a, b, c, d = rb[EXPERTS_PER_GROUP * g:EXPERTS_PER_GROUP * (g + 1)]
        hi1, lo1, hi2, lo2 = jnp.maximum(a, b), jnp.minimum(a, b), jnp.maximum(c, d), jnp.minimum(c, d)
        gsc.append(jnp.maximum(hi1, hi2) + jnp.maximum(jnp.minimum(hi1, hi2), jnp.maximum(lo1, lo2)))
    best = jnp.zeros((1, tm), jnp.int32)
    bsc = gsc[0]
    for g in range(1, N_GROUPS):
        upd = gsc[g] > bsc
        best = jnp.where(upd, g, best)
        bsc = jnp.where(upd, gsc[g], bsc)

    def pick(rows, p):
        out = rows[p]
        for g in range(1, N_GROUPS):
            out = jnp.where(best == g, rows[EXPERTS_PER_GROUP * g + p], out)
        return out

    vals = [pick(rb, p) for p in range(EXPERTS_PER_GROUP)]
    scs = [pick(rs, p) for p in range(EXPERTS_PER_GROUP)]
    p1, v1, s1 = jnp.zeros((1, tm), jnp.int32), vals[0], scs[0]
    for p in range(1, EXPERTS_PER_GROUP):
        upd = vals[p] > v1
        p1, v1, s1 = jnp.where(upd, p, p1), jnp.where(upd, vals[p], v1), jnp.where(upd, scs[p], s1)
    p2 = jnp.zeros((1, tm), jnp.int32)
    v2 = jnp.full((1, tm), -jnp.inf, F32)
    s2 = jnp.zeros((1, tm), F32)
    for p in range(EXPERTS_PER_GROUP):
        upd = (p1 != p) & (vals[p] > v2)
        p2, v2, s2 = jnp.where(upd, p, p2), jnp.where(upd, vals[p], v2), jnp.where(upd, scs[p], s2)
    e1 = best * EXPERTS_PER_GROUP + p1
    e2 = best * EXPERTS_PER_GROUP + p2
    tot = s1 + s2
    eiota = lax.broadcasted_iota(jnp.int32, (N_EXPERTS, tm), 0)
    oh1 = (eiota == e1).astype(F32)
    oh2 = (eiota == e2).astype(F32)
    oh = oh1 + oh2
    before = (lax.broadcasted_iota(jnp.int32, (tm, tm), 0) < lax.broadcasted_iota(jnp.int32, (tm, tm), 1))
    cnt = _dot(oh, before.astype(BF16)) + carry_ref[:, :1]
    r1 = jnp.sum(oh1 * cnt, axis=0, keepdims=True).astype(jnp.int32)
    r2 = jnp.sum(oh2 * cnt, axis=0, keepdims=True).astype(jnp.int32)
    zi = jnp.zeros((SUBLANES - 4, tm), jnp.int32)
    ii_ref[...] = jnp.concatenate([e1, e2, r1, r2, zi], axis=0)
    iw_ref[...] = jnp.concatenate([s1 / tot, s2 / tot, jnp.zeros((SUBLANES - 2, tm), F32)], axis=0)
    carry = carry_ref[...] + jnp.sum(oh, axis=1, keepdims=True)
    carry_ref[...] = carry
    cnt_ref[...] = carry


def _router(x, n_rows, g, sc, sh, seg_fn, router_wt, router_b):
    d = x.shape[1]
    tm = MOE_TILE
    return pl.pallas_call(
        _router_kernel,
        grid=(n_rows // tm,),
        in_specs=[pl.BlockSpec((tm, d), lambda i: (i, 0)),
                  pl.BlockSpec((1, d), lambda i: (0, 0)),
                  pl.BlockSpec((None, 1, d), lambda i: (seg_fn(i * tm), 0, 0)),
                  pl.BlockSpec((None, 1, d), lambda i: (seg_fn(i * tm), 0, 0)),
                  pl.BlockSpec((N_EXPERTS, d), lambda i: (0, 0)),
                  pl.BlockSpec((N_EXPERTS, 1), lambda i: (0, 0))],
        out_specs=[pl.BlockSpec((tm, PACK_SUB, LANES), lambda i: (i, 0, 0)),
                   pl.BlockSpec((SUBLANES, tm), lambda i: (0, i)),
                   pl.BlockSpec((SUBLANES, tm), lambda i: (0, i)),
                   pl.BlockSpec((N_EXPERTS, LANES), lambda i: (0, 0))],
        out_shape=[jax.ShapeDtypeStruct((n_rows, PACK_SUB, LANES), U32),
                   jax.ShapeDtypeStruct((SUBLANES, n_rows), jnp.int32),
                   jax.ShapeDtypeStruct((SUBLANES, n_rows), F32),
                   jax.ShapeDtypeStruct((N_EXPERTS, LANES), F32)],
        scratch_shapes=[pltpu.VMEM((N_EXPERTS, LANES), F32)],
        compiler_params=_cparams(("arbitrary",)),
        name="router",
    )(x, g.reshape(1, d), sc, sh, router_wt, router_b.reshape(N_EXPERTS, 1))


TOK_SUB = D_MODEL // LANES


def _store_token_rows(ref, val):
    for s in range(TOK_SUB):
        ref[:, s, :] = val[:, s * LANES:(s + 1) * LANES]


PACK_SUB = TOK_SUB // 2
U32 = jnp.uint32


def _store_packed_rows(ref, val):
    for s in range(PACK_SUB):
        lo = val[:, s * LANES:(s + 1) * LANES].astype(BF16).astype(F32)
        hi = val[:, (PACK_SUB + s) * LANES:(PACK_SUB + s + 1) * LANES].astype(BF16).astype(F32)
        ref[:, s, :] = (lax.bitcast_convert_type(hi, U32) & jnp.uint32(0xFFFF0000)) | (
            lax.bitcast_convert_type(lo, U32) >> 16)


def _unpack_pair(u):
    return (lax.bitcast_convert_type(u << 16, F32), lax.bitcast_convert_type(u & jnp.uint32(0xFFFF0000), F32))


def _row_copy(src_ref, o_ref, sem, src_row, dst_row):
    return pltpu.make_async_copy(src_ref.at[pl.ds(src_row, 1)], o_ref.at[pl.ds(dst_row, 1)], sem)


def _expert_kernel(te_ref, nu_ref, xn_ref, w1_ref, w3_ref, w2_ref, o_ref, xb_ref, acc_ref, *, nf):
    i, f = pl.program_id(0), pl.program_id(1)
    compute = (i >= 1) & (i - 1 < nu_ref[0])
    part = xn_ref.shape[0] // nf

    def convert(fv):
        rows = slice(fv * part, (fv + 1) * part)
        for s in range(PACK_SUB):
            lo, hi = _unpack_pair(xn_ref[rows, s, :])
            xb_ref[i % 2, rows, s * LANES:(s + 1) * LANES] = lo.astype(BF16)
            xb_ref[i % 2, rows, (PACK_SUB + s) * LANES:(PACK_SUB + s + 1) * LANES] = hi.astype(BF16)

    for fv in range(nf):
        @pl.when((f == fv) & (i == 0))
        def _(fv=fv):
            convert(fv)

        @pl.when((f == fv) & compute)
        def _(fv=fv):
            xb = xb_ref[(i - 1) % 2]
            a = _silu(_dot(xb, w1_ref[...])) * _dot(xb, w3_ref[...])
            y = _dot(a, w2_ref[...])
            convert(fv)
            if fv > 0:
                y = acc_ref[...] + y
            if fv < nf - 1:
                acc_ref[...] = y
            else:
                _store_token_rows(o_ref, y)

    @pl.when((f == nf - 1) & (i >= 1) & jnp.logical_not(compute))
    def _():
        o_ref[...] = jnp.zeros_like(o_ref)


def _experts(tile_e, n_used, xs, w1, w3, w2, layer):
    p = xs.shape[0]
    d = D_MODEL
    te, tf = MOE_TILE, 512
    nf = D_EXPERT // tf
    ntile = p // te
    tile = lambda i: jnp.maximum(i - 1, 0)
    fidx = lambda i, f, nu: jnp.where((i >= 1) & (i - 1 < nu[0]), f, nf - 1)
    grid_spec = pltpu.PrefetchScalarGridSpec(
        num_scalar_prefetch=2,
        grid=(ntile + 1, nf),
        in_specs=[pl.BlockSpec((te, PACK_SUB, LANES), lambda i, f, te_r, nu: (jnp.minimum(i, ntile - 1), 0, 0)),
                  pl.BlockSpec((None, None, d, tf),
                               lambda i, f, te_r, nu: (layer, te_r[tile(i)], 0, fidx(i, f, nu))),
                  pl.BlockSpec((None, None, d, tf),
                               lambda i, f, te_r, nu: (layer, te_r[tile(i)], 0, fidx(i, f, nu))),
                  pl.BlockSpec((None, None, tf, d),
                               lambda i, f, te_r, nu: (layer, te_r[tile(i)], fidx(i, f, nu), 0))],
        out_specs=pl.BlockSpec((te, TOK_SUB, LANES), lambda i, f, te_r, nu: (tile(i), 0, 0)),
        scratch_shapes=[pltpu.VMEM((2, te, d), BF16), pltpu.VMEM((te, d), F32)])
    return pl.pallas_call(
        functools.partial(_expert_kernel, nf=nf),
        grid_spec=grid_spec,
        out_shape=jax.ShapeDtypeStruct((p, TOK_SUB, LANES), F32),
        compiler_params=_cparams(("arbitrary", "arbitrary")),
        name="experts",
    )(tile_e, n_used, xs, w1, w3, w2)


def _combine_kernel(p1_ref, p2_ref, q1_ref, q2_ref, ys_ref, x_ref, w_ref, g2_ref, fg_ref, o_ref, y_ref, sem, *,
                    final):
    i = pl.program_id(0)
    tm = x_ref.shape[0]
    slot = i % 2

    def fetch(pa_ref, pb_ref, sl):
        def issue(r8, carry):
            for u in range(SUBLANES):
                r = r8 * SUBLANES + u
                _row_copy(ys_ref, y_ref.at[sl, 0], sem.at[sl], pa_ref[r], r).start()
                _row_copy(ys_ref, y_ref.at[sl, 1], sem.at[sl], pb_ref[r], r).start()
            return carry

        lax.fori_loop(0, tm // SUBLANES, issue, 0)

    @pl.when(i == 0)
    def _():
        fetch(p1_ref, p2_ref, 0)

    @pl.when(i + 1 < pl.num_programs(0))
    def _():
        fetch(q1_ref, q2_ref, 1 - slot)

    for k in range(TOP_K):
        pltpu.make_async_copy(ys_ref.at[pl.ds(0, tm)], y_ref.at[slot, k], sem.at[slot]).wait()
    w = w_ref[...]
    w1 = jnp.broadcast_to(w[:, 0:1], (tm, LANES))
    w2 = jnp.broadcast_to(w[:, 1:2], (tm, LANES))
    for s in range(TOK_SUB):
        cs = slice(s * LANES, (s + 1) * LANES)
        o_ref[:, cs] = x_ref[:, cs] + g2_ref[:, cs] * (w1 * y_ref[slot, 0, :, s, :] + w2 * y_ref[slot, 1, :, s, :])
    if final:
        out = o_ref[...]
        o_ref[...] = out * lax.rsqrt(jnp.mean(out * out, axis=-1, keepdims=True) + EPS) * fg_ref[...]


def _combine(x, n_rows, pos, ys, wts, g2, seg_fn, final_g):
    d = x.shape[1]
    tm = MOE_TILE
    nt = n_rows // tm
    fg = jnp.ones((1, d), F32) if final_g is None else final_g.reshape(1, d)
    nxt = lambda i: jnp.minimum(i + 1, nt - 1)
    smem = lambda fn: pl.BlockSpec((tm,), fn, memory_space=pltpu.SMEM)
    return pl.pallas_call(
        functools.partial(_combine_kernel, final=final_g is not None),
        grid=(nt,),
        in_specs=[smem(lambda i: (i,)), smem(lambda i: (nt + i,)),
                  smem(lambda i: (nxt(i),)), smem(lambda i: (nt + nxt(i),)),
                  pl.BlockSpec(memory_space=pl.ANY),
                  pl.BlockSpec((tm, d), lambda i: (i, 0)),
                  pl.BlockSpec((tm, 2), lambda i: (i, 0)),
                  pl.BlockSpec((None, 1, d), lambda i: (seg_fn(i * tm), 0, 0)),
                  pl.BlockSpec((1, d), lambda i: (0, 0))],
        out_specs=pl.BlockSpec((tm, d), lambda i: (i, 0)),
        out_shape=jax.ShapeDtypeStruct((n_rows, d), F32),
        scratch_shapes=[pltpu.VMEM((2, TOP_K, tm, TOK_SUB, LANES), F32), pltpu.SemaphoreType.DMA((2,))],
        compiler_params=_cparams(("arbitrary",)),
        name="combine",
    )(pos, pos, pos, pos, ys, x, wts, g2, fg)


def _dispatch_kernel(p1_ref, p2_ref, ends_ref, hn_ref, xs_ref, zero_ref, sem, zsem, *, n_pos):
    tm = hn_ref.shape[0]
    zrows = zero_ref.shape[0]

    @pl.when(pl.program_id(0) == 0)
    def _():
        zero_ref[...] = jnp.zeros_like(zero_ref)
        starts = [jnp.maximum(ends_ref[e] - zrows, 0) for e in range(N_EXPERTS)]
        tail = [ends_ref[N_EXPERTS - 1] + j * zrows for j in range(N_EXPERTS)]
        for st in starts:
            pltpu.make_async_copy(zero_ref, xs_ref.at[pl.ds(st, zrows)], zsem).start()
        for st in tail:
            @pl.when(st < n_pos)
            def _(st=st):
                pltpu.make_async_copy(zero_ref, xs_ref.at[pl.ds(st, zrows)], zsem).start()
        for st in starts:
            pltpu.make_async_copy(zero_ref, xs_ref.at[pl.ds(st, zrows)], zsem).wait()
        for st in tail:
            @pl.when(st < n_pos)
            def _(st=st):
                pltpu.make_async_copy(zero_ref, xs_ref.at[pl.ds(st, zrows)], zsem).wait()

    def issue(r8, carry):
        for u in range(SUBLANES):
            r = r8 * SUBLANES + u
            _row_copy(hn_ref, xs_ref, sem, r, p1_ref[r]).start()
            _row_copy(hn_ref, xs_ref, sem, r, p2_ref[r]).start()
        return carry

    lax.fori_loop(0, tm // SUBLANES, issue, 0)
    for k in range(TOP_K):
        pltpu.make_async_copy(hn_ref, xs_ref.at[pl.ds(0, tm)], sem).wait()


def _dispatch(pos, ends, hn, n_rows, n_pos):
    tm = MOE_TILE
    nt = n_rows // tm
    return pl.pallas_call(
        functools.partial(_dispatch_kernel, n_pos=n_pos),
        grid=(nt,),
        in_specs=[pl.BlockSpec((tm,), lambda i: (i,), memory_space=pltpu.SMEM),
                  pl.BlockSpec((tm,), lambda i: (nt + i,), memory_space=pltpu.SMEM),
                  pl.BlockSpec(memory_space=pltpu.SMEM),
                  pl.BlockSpec((tm,) + hn.shape[1:], lambda i: (i, 0, 0))],
        out_specs=pl.BlockSpec(memory_space=pl.ANY),
        out_shape=jax.ShapeDtypeStruct((n_pos,) + hn.shape[1:], hn.dtype),
        scratch_shapes=[pltpu.VMEM((tm,) + hn.shape[1:], hn.dtype), pltpu.SemaphoreType.DMA,
                        pltpu.SemaphoreType.DMA],
        compiler_params=_cparams(("arbitrary",)),
        name="dispatch",
    )(pos, pos, ends, hn)


def _moe(x, n_rows, seg_fn, g, sc, sh, g2, router_wt, router_b, w1, w3, w2, layer, final_g=None):
    te = MOE_TILE
    hn, ii, iw, cnt = _router(x, n_rows, g, sc, sh, seg_fn, router_wt, router_b)
    counts = cnt[:, 0].astype(jnp.int32)
    padded = (counts + te - 1) // te * te
    ends = jnp.cumsum(padded)
    base = ends - padded
    pos = jnp.concatenate([base[ii[0]] + ii[2], base[ii[1]] + ii[3]])
    p = 2 * n_rows + N_EXPERTS * te
    ntile = p // te
    n_used = ends[-1] // te
    tiles = jnp.arange(ntile, dtype=jnp.int32)
    tile_e = jnp.sum((jnp.minimum(tiles, n_used - 1)[:, None] * te >= ends[None, :]).astype(jnp.int32), axis=1)
    xs = _dispatch(pos, ends, hn, n_rows, p)
    ys = _experts(tile_e, n_used.reshape(1).astype(jnp.int32), xs, w1, w3, w2, layer)
    return _combine(x, n_rows, pos, ys, iw[:2].T, g2, seg_fn, final_g)


def kernel(x, c, ctx, c_ctx, ada_w, ada_b, norm1_g, norm2_g, ab_w_in, ab_conv_w, ab_conv_b, ml_ig_b, ml_fg_b,
           hg_lb, ml_norm_g, hg_norm_g, ab_w_out, ssd_w_in, ssd_conv_w, ssd_conv_b, ssd_dt_b, ssd_a_log, ssd_d,
           ssd_norm_g, ssd_w_out, router_w, router_b, moe_w1, moe_w3, moe_w2, final_g):
    b, seq, d = x.shape
    ctxl = ctx.shape[1]
    depth = ada_w.shape[0]
    dims = (b, seq, ctxl)
    nl, nc = b * seq, b * ctxl
    rows = seq // GRID_W
    tm = min(512, nc)
    seg_fn = lambda row: jnp.where(row < nl, row // seq, b)
    c8 = jnp.concatenate([c, c_ctx[None], jnp.zeros((SUBLANES - b - 1, d), F32)])
    mods = _modulation(c8, ada_w, ada_b)[:, :b + 1].reshape(depth, b + 1, 6, 1, d)
    lb_all = jnp.cumsum(jax.nn.softmax(hg_lb.astype(F32), axis=0), axis=0)
    router_wt = router_w.T
    pad = jnp.zeros((-(nl + nc) % min(2 * tm, seq), d), F32)
    xr = jnp.concatenate([x.reshape(nl, d), ctx.reshape(nc, d), pad])
    transposed = False
    for l in range(depth):
        sh1, sc1, g1, sh2, sc2, g2 = (mods[l][:, k] for k in range(6))
        keep_ctx = l < depth - 1
        n_out = nl + nc if keep_ctx else nl
        j = l // 2
        if (l % 2 == 1) != transposed:
            xt = _regroup(xr, b, GRID_W if transposed else rows, rows if transposed else GRID_W, nl)
            xr = jnp.concatenate([xt, xr[nl:nl + nc], pad])
            transposed = not transposed
        if l % 2 == 0:
            xr = _ab_layer(xr, dims, seg_fn, tm, n_out, norm1_g[l], sc1, sh1, g1, ab_w_in[j], ab_conv_w[j],
                           ab_conv_b[j], ml_ig_b[j], ml_fg_b[j], lb_all[l], ml_norm_g[j], hg_norm_g[j], ab_w_out[j])
        else:
            xr = _ssd_layer(xr, dims, seg_fn, tm, n_out, norm1_g[l], sc1, sh1, g1, ssd_w_in[j], ssd_conv_w[j],
                            ssd_conv_b[j], ssd_dt_b[j], ssd_a_log[j], ssd_d[j], ssd_norm_g[j], ssd_w_out[j])
        xr = _moe(xr, n_out, seg_fn, norm2_g[l], sc2, sh2, g2, router_wt, router_b, moe_w1, moe_w3, moe_w2, l,
                  final_g if l == depth - 1 else None)
    if transposed:
        xr = _regroup(xr, b, GRID_W, rows, nl)
    return xr[:nl].reshape(b, seq, d)
```
